```python
import math
import jax, jax.numpy as jnp
from jax import lax
import numpy as np

D_MODEL = 4096
BATCH = 4
SEQ = 2048
DEPTH = 4
DEC_BATCH = 32
DEC_SEQ = 4
PAST_LEN = 8192
PAGE_SIZE = 128

MIX_A = D_MODEL // 4
MIX_B = D_MODEL // 4
MIX_C = D_MODEL // 4
MIX_D = D_MODEL - MIX_A - MIX_B - MIX_C

MLSTM_DH = 128
MLSTM_HEADS = MIX_A // MLSTM_DH
MLSTM_CHUNK = 64
GATE_CAP = 15.0

SWA_DH = 64
SWA_HEADS = MIX_B // SWA_DH
SWA_KV_HEADS = max(1, SWA_HEADS // 8)
SWA_GROUP = SWA_HEADS // SWA_KV_HEADS
WINDOW = 128
ROPE_THETA = 10000.0

SSD_DH = 64
SSD_HEADS = MIX_C // SSD_DH
SSD_GROUPS = 4
SSD_HPG = SSD_HEADS // SSD_GROUPS
SSD_DSTATE = 128
SSD_CONV = 4
SSD_CONV_DIM = MIX_C + 2 * SSD_GROUPS * SSD_DSTATE
SSD_CHUNK = 128

POOL_WINDOWS = (2, 4, 8, 16)
POOL_GROUPS = len(POOL_WINDOWS)
POOL_GC = MIX_D // POOL_GROUPS
POOL_MAX_WINDOW = max(POOL_WINDOWS)

RMS_EPS = 1e-6

IN_SIZES = (MIX_A, MIX_A, MIX_A, MIX_A, MIX_A, MLSTM_HEADS, MLSTM_HEADS,
            SWA_HEADS * SWA_DH, SWA_KV_HEADS * SWA_DH, SWA_KV_HEADS * SWA_DH, MIX_B,
            MIX_C, SSD_CONV_DIM, SSD_HEADS,
            MIX_D, MIX_D)
IN_COLS = sum(IN_SIZES)
IN_SPLITS = tuple(int(s) for s in np.cumsum(IN_SIZES)[:-1])

kernel_name = 'hybrid_mlstm_swa_ssd_pool_decode_step'


def rmsnorm(x, w):
    xf = x.astype(jnp.float32)
    y = xf * lax.rsqrt(jnp.mean(xf * xf, axis=-1, keepdims=True) + RMS_EPS)
    return (y * w.astype(jnp.float32)).astype(x.dtype)


def rope(x, pos):
    half = x.shape[-1] // 2
    inv = ROPE_THETA ** (-jnp.arange(half, dtype=jnp.float32) / half)
    ang = pos.astype(jnp.float32)[:, None] * inv[None, :]
    cos = jnp.cos(ang)[None, :, None, :]
    sin = jnp.sin(ang)[None, :, None, :]
    xf = x.astype(jnp.float32)
    x1, x2 = xf[..., :half], xf[..., half:]
    return jnp.concatenate([x1 * cos - x2 * sin, x2 * cos + x1 * sin], axis=-1).astype(x.dtype)


def sink_attend(q, k, v, mask, sinks):
    s = jnp.einsum('...qhgd,...khd->...hgqk', q.astype(jnp.float32), k.astype(jnp.float32)) * (SWA_DH ** -0.5)
    s = jnp.where(mask[..., None, None, :, :], s, -jnp.inf)
    sink = jnp.broadcast_to(sinks.astype(jnp.float32)[:, :, None, None], s.shape[:-1] + (1,))
    p = jax.nn.softmax(jnp.concatenate([s, sink], axis=-1), axis=-1)[..., :-1]
    return jnp.einsum('...hgqk,...khd->...qhgd', p, v.astype(jnp.float32))


def swa_prompt(q, k, v, sinks):
    B, T = q.shape[:2]
    blk = WINDOW
    nb = T // blk
    qb = q.reshape(B, nb, blk, SWA_KV_HEADS, SWA_GROUP, SWA_DH)
    kb = k.reshape(B, nb, blk, SWA_KV_HEADS, SWA_DH)
    vb = v.reshape(B, nb, blk, SWA_KV_HEADS, SWA_DH)
    pad = ((0, 0), (1, 0), (0, 0), (0, 0), (0, 0))
    kk = jnp.concatenate([jnp.pad(kb, pad)[:, :-1], kb], axis=2)
    vv = jnp.concatenate([jnp.pad(vb, pad)[:, :-1], vb], axis=2)
    qi = jnp.arange(blk)[:, None]
    kj = jnp.arange(2 * blk)[None, :]
    rel = qi + blk - kj
    band = (rel >= 0) & (rel < WINDOW)
    bidx = jnp.arange(nb)[:, None, None]
    mask = band[None] & ((bidx > 0) | (kj >= blk)[None])
    out = sink_attend(qb, kk, vv, mask, sinks)
    return out.reshape(B, T, SWA_HEADS * SWA_DH)


def swa_sample(q, k, v, kbuf, vbuf, sinks, pos):
    B, T = q.shape[:2]
    W = kbuf.shape[1]
    kk = jnp.concatenate([kbuf.astype(k.dtype), k], axis=1)
    vv = jnp.concatenate([vbuf.astype(v.dtype), v], axis=1)
    kpos = jnp.concatenate([pos[0] - W + jnp.arange(W), pos])
    mask = (kpos[None, :] <= pos[:, None]) & (pos[:, None] - kpos[None, :] < WINDOW)
    qg = q.reshape(B, T, SWA_KV_HEADS, SWA_GROUP, SWA_DH)
    out = sink_attend(qg, kk, vv, mask, sinks)
    return out.reshape(B, T, SWA_HEADS * SWA_DH)


def mlstm_scan(q, k, v, ig, fg, C0, n0, m0, chunk):
    B, T, H, D = q.shape
    nc = T // chunk

    def to_chunks(a):
        a = a.reshape((B, nc, chunk) + a.shape[2:])
        return jnp.moveaxis(jnp.moveaxis(a, 1, 0), 2, 3)

    tril = jnp.tril(jnp.ones((chunk, chunk), dtype=bool))

    def step(carry, inp):
        C, n, m = carry
        qc, kc, vc, ic, fc = inp
        b = jnp.cumsum(jax.nn.log_sigmoid(fc), axis=-1)
        dmat = jnp.where(tril, b[..., :, None] - b[..., None, :] + ic[..., None, :], -jnp.inf)
        inter = b + m[..., None]
        mt = jnp.maximum(inter, jnp.max(dmat, axis=-1))
        S = jnp.exp(dmat - mt[..., None]) * jnp.einsum('bhtd,bhsd->bhts', qc, kc)
        wprev_t = jnp.exp(inter - mt)
        num = wprev_t[..., None] * jnp.einsum('bhtd,bhde->bhte', qc, C) + jnp.einsum('bhts,bhse->bhte', S, vc)
        den = wprev_t * jnp.einsum('bhtd,bhd->bht', qc, n) + jnp.sum(S, axis=-1)
        h = num / jnp.maximum(jnp.abs(den), jnp.exp(-mt))[..., None]
        bL = b[..., -1]
        dec = bL[..., None] - b + ic
        m_new = jnp.maximum(bL + m, jnp.max(dec, axis=-1))
        wprev = jnp.exp(bL + m - m_new)
        wk = jnp.exp(dec - m_new[..., None])
        C_new = wprev[..., None, None] * C + jnp.einsum('bhs,bhsd,bhse->bhde', wk, kc, vc)
        n_new = wprev[..., None] * n + jnp.einsum('bhs,bhsd->bhd', wk, kc)
        return (C_new, n_new, m_new), h

    xs = (to_chunks(q), to_chunks(k), to_chunks(v), to_chunks(ig), to_chunks(fg))
    (C, n, m), h = lax.scan(step, (C0, n0, m0), xs)
    h = jnp.swapaxes(jnp.moveaxis(h, 0, 1), 2, 3).reshape(B, T, H, D)
    return h, C, n, m


def ssd_scan(x, dt, A, Bm, Cm, state0, chunk):
    Bsz, T = x.shape[:2]
    nc = T // chunk

    def to_chunks(a):
        return jnp.moveaxis(a.reshape((Bsz, nc, chunk) + a.shape[2:]), 1, 0)

    tril = jnp.tril(jnp.ones((chunk, chunk), dtype=bool))

    def step(state, inp):
        xc, dtc, Bc, Cc = inp
        a = jnp.moveaxis(dtc * A, 1, -1)
        cum = jnp.cumsum(a, axis=-1)
        seg = cum[..., :, None] - cum[..., None, :]
        Lm = jnp.where(tril, jnp.exp(jnp.where(tril, seg, 0.0)), 0.0)
        xdt = xc * dtc[..., None]
        CB = jnp.einsum('blgn,bsgn->bgls', Cc, Bc)
        y = jnp.einsum('bgls,bghls,bsghp->blghp', CB, Lm, xdt)
        y = y + jnp.einsum('blgn,bghpn,bghl->blghp', Cc, state, jnp.exp(cum))
        wend = jnp.exp(cum[..., -1:] - cum)
        new = jnp.exp(cum[..., -1])[..., None, None] * state + jnp.einsum('bghs,bsghp,bsgn->bghpn', wend, xdt, Bc)
        return new, y

    state, y = lax.scan(step, state0, (to_chunks(x), to_chunks(dt), to_chunks(Bm), to_chunks(Cm)))
    y = jnp.moveaxis(y, 0, 1).reshape(x.shape)
    return y, state


def causal_conv(full, w, b):
    T = full.shape[1] - (SSD_CONV - 1)
    out = b.astype(jnp.float32)
    for j in range(SSD_CONV):
        out = out + full[:, j:j + T].astype(jnp.float32) * w[j].astype(jnp.float32)
    return out


def pool_mix(u_full, n_hist, lin, scale):
    uf = u_full.astype(jnp.float32)
    B, L, C = uf.shape
    T = L - n_hist
    cs = jnp.concatenate([jnp.zeros((B, 1, C), jnp.float32), jnp.cumsum(uf, axis=1)], axis=1)
    idx = jnp.arange(n_hist, L)
    hi = jnp.take(cs, idx + 1, axis=1)
    cur = uf[:, n_hist:]
    outs = []
    for g, w in enumerate(POOL_WINDOWS):
        sl = slice(g * POOL_GC, (g + 1) * POOL_GC)
        lo = jnp.maximum(idx + 1 - w, 0)
        cnt = (idx + 1 - lo).astype(jnp.float32)
        mean = (hi[..., sl] - jnp.take(cs[..., sl], lo, axis=1)) / cnt[None, :, None]
        outs.append(mean - cur[..., sl])
    d = jnp.stack(outs, axis=2)
    y = jnp.einsum('btgc,gcd->btgd', d, lin.astype(jnp.float32)).reshape(B, T, C)
    return y * scale.astype(jnp.float32)


def mixer_layer(x, prm, C0, n0, m0, kbuf, vbuf, ssd0, conv0, pool0, prompt):
    (g_pre, g_post, w_in, w_out, b_i, b_f, mnorm, sinks, conv_w, conv_b,
     dt_bias, A_log, D_skip, snorm, pool_lin, pool_scale) = prm
    f32 = jnp.float32
    B, T, _ = x.shape
    pos = (0 if prompt else PAST_LEN) + jnp.arange(T)
    h = rmsnorm(x, g_pre)
    proj = jnp.einsum('btd,dc->btc', h, w_in)
    (qa, ka, va, oa, za, ia, fa, qb, kb, vb, zb, zc, xbc, dtc, ud, zd) = jnp.split(proj, IN_SPLITS, axis=-1)

    hd = (B, T, MLSTM_HEADS, MLSTM_DH)
    ig = GATE_CAP * jnp.tanh((ia.astype(f32) + b_i.astype(f32)) / GATE_CAP)
    fg = GATE_CAP * jnp.tanh((fa.astype(f32) + b_f.astype(f32)) / GATE_CAP)
    chunk_a = MLSTM_CHUNK if T % MLSTM_CHUNK == 0 else T
    ha, C1, n1, m1 = mlstm_scan(qa.reshape(hd).astype(f32) * (MLSTM_DH ** -0.5), ka.reshape(hd).astype(f32),
                                va.reshape(hd).astype(f32), ig, fg, C0.astype(f32), n0.astype(f32),
                                m0.astype(f32), chunk_a)
    ha = rmsnorm(ha, mnorm.reshape(MLSTM_HEADS, MLSTM_DH)).reshape(B, T, MIX_A)
    y_a = ha * jax.nn.sigmoid(oa.astype(f32)) * jax.nn.silu(za.astype(f32))

    qb = rope(qb.reshape(B, T, SWA_HEADS, SWA_DH), pos)
    kb = rope(kb.reshape(B, T, SWA_KV_HEADS, SWA_DH), pos)
    vb = vb.reshape(B, T, SWA_KV_HEADS, SWA_DH)
    sk = sinks.reshape(SWA_KV_HEADS, SWA_GROUP)
    if prompt:
        ob = swa_prompt(qb, kb, vb, sk)
        k_all, v_all = kb, vb
    else:
        ob = swa_sample(qb, kb, vb, kbuf, vbuf, sk, pos)
        k_all = jnp.concatenate([kbuf.astype(kb.dtype), kb], axis=1)
        v_all = jnp.concatenate([vbuf.astype(vb.dtype), vb], axis=1)
    nbuf = min(WINDOW, k_all.shape[1])
    k1, v1 = k_all[:, -nbuf:], v_all[:, -nbuf:]
    y_b = ob * jax.nn.silu(zb.astype(f32))

    conv_full = jnp.concatenate([conv0.astype(xbc.dtype), xbc], axis=1)
    xbc_c = jax.nn.silu(causal_conv(conv_full, conv_w, conv_b))
    conv1 = conv_full[:, -(SSD_CONV - 1):]
    nbc = SSD_GROUPS * SSD_DSTATE
    xs = xbc_c[..., :MIX_C].reshape(B, T, SSD_GROUPS, SSD_HPG, SSD_DH)
    bs = xbc_c[..., MIX_C:MIX_C + nbc].reshape(B, T, SSD_GROUPS, SSD_DSTATE)
    cs = xbc_c[..., MIX_C + nbc:].reshape(B, T, SSD_GROUPS, SSD_DSTATE)
    dt = jax.nn.softplus(dtc.astype(f32) + dt_bias.astype(f32)).reshape(B, T, SSD_GROUPS, SSD_HPG)
    A = -jnp.exp(A_log.astype(f32)).reshape(SSD_GROUPS, SSD_HPG)
    chunk_c = SSD_CHUNK if T % SSD_CHUNK == 0 else T
    s0 = ssd0.astype(f32).reshape(B, SSD_GROUPS, SSD_HPG, SSD_DH, SSD_DSTATE)
    yc, s1 = ssd_scan(xs, dt, A, bs, cs, s0, chunk_c)
    yc = yc + D_skip.astype(f32).reshape(SSD_GROUPS, SSD_HPG)[..., None] * xs
    y_c = rmsnorm(yc.reshape(B, T, MIX_C) * jax.nn.silu(zc.astype(f32)), snorm)
    s1 = s1.reshape(B, SSD_HEADS, SSD_DH, SSD_DSTATE)

    pool_full = jnp.concatenate([pool0.astype(ud.dtype), ud], axis=1)
    y_d = pool_mix(pool_full, pool0.shape[1], pool_lin, pool_scale) * jax.nn.silu(zd.astype(f32))
    pool1 = pool_full[:, -(POOL_MAX_WINDOW - 1):]

    mix = jnp.concatenate([y_a, y_b, y_c, y_d], axis=-1).astype(x.dtype)
    out = jnp.einsum('btc,cd->btd', mix, w_out)
    x = x + rmsnorm(out, g_post)
    return x, (C1, n1, m1, k1, v1, s1, conv1, pool1)


def stack_states(lst):
    return tuple(jnp.stack([st[i] for st in lst], axis=0) for i in range(len(lst[0])))


def setup_inputs(seed: int = 0) -> dict:
    key = jax.random.key(seed)
    ks = jax.random.split(key, 32)
    f32 = jnp.float32

    def nrm(k, shape, s):
        return s * jax.random.normal(k, shape, f32)

    W = min(WINDOW, PAST_LEN)
    dt0 = jnp.exp(jax.random.uniform(ks[20], (DEPTH, SSD_HEADS), f32, math.log(1e-3), math.log(1e-1)))
    return {
        'x_prompt': nrm(ks[0], (BATCH, SEQ, D_MODEL), 1.0),
        'x_sample': nrm(ks[1], (DEC_BATCH, DEC_SEQ, D_MODEL), 1.0),
        'state_mlstm_C': nrm(ks[2], (DEPTH, DEC_BATCH, MLSTM_HEADS, MLSTM_DH, MLSTM_DH), 0.5),
        'state_mlstm_n': nrm(ks[3], (DEPTH, DEC_BATCH, MLSTM_HEADS, MLSTM_DH), 2.0),
        'state_mlstm_m': nrm(ks[4], (DEPTH, DEC_BATCH, MLSTM_HEADS), 1.0),
        'cache_swa_k': nrm(ks[5], (DEPTH, DEC_BATCH, W, SWA_KV_HEADS, SWA_DH), 1.0),
        'cache_swa_v': nrm(ks[6], (DEPTH, DEC_BATCH, W, SWA_KV_HEADS, SWA_DH), 1.0),
        'state_ssd': nrm(ks[7], (DEPTH, DEC_BATCH, SSD_HEADS, SSD_DH, SSD_DSTATE), 0.1),
        'state_ssd_conv': nrm(ks[8], (DEPTH, DEC_BATCH, SSD_CONV - 1, SSD_CONV_DIM), 1.0),
        'state_pool': nrm(ks[9], (DEPTH, DEC_BATCH, POOL_MAX_WINDOW - 1, MIX_D), 1.0),
        'norm_pre': 1.0 + nrm(ks[10], (DEPTH, D_MODEL), 0.02),
        'norm_post': 1.0 + nrm(ks[11], (DEPTH, D_MODEL), 0.02),
        'w_in': nrm(ks[12], (DEPTH, D_MODEL, IN_COLS), D_MODEL ** -0.5),
        'w_out': nrm(ks[13], (DEPTH, D_MODEL, D_MODEL), D_MODEL ** -0.5),
        'mlstm_b_i': nrm(ks[14], (DEPTH, MLSTM_HEADS), 0.1),
        'mlstm_b_f': jnp.linspace(3.0, 6.0, MLSTM_HEADS, dtype=f32)[None, :] + nrm(ks[15], (DEPTH, MLSTM_HEADS), 0.1),
        'mlstm_norm': 1.0 + nrm(ks[16], (DEPTH, MIX_A), 0.02),
        'swa_sinks': nrm(ks[17], (DEPTH, SWA_HEADS), 0.5),
        'ssd_conv_w': nrm(ks[18], (DEPTH, SSD_CONV, SSD_CONV_DIM), SSD_CONV ** -0.5),
        'ssd_conv_b': nrm(ks[19], (DEPTH, SSD_CONV_DIM), 0.02),
        'ssd_dt_bias': dt0 + jnp.log(-jnp.expm1(-dt0)),
        'ssd_A_log': jnp.log(jax.random.uniform(ks[21], (DEPTH, SSD_HEADS), f32, 1.0, 16.0)),
        'ssd_D': 1.0 + nrm(ks[22], (DEPTH, SSD_HEADS), 0.1),
        'ssd_norm': 1.0 + nrm(ks[23], (DEPTH, MIX_C), 0.02),
        'pool_lin': nrm(ks[24], (DEPTH, POOL_GROUPS, POOL_GC, POOL_GC), POOL_GC ** -0.5),
        'pool_scale': 1.0 + nrm(ks[25], (DEPTH, MIX_D), 0.05),
    }


def reference(x_prompt, x_sample, state_mlstm_C, state_mlstm_n, state_mlstm_m, cache_swa_k, cache_swa_v,
              state_ssd, state_ssd_conv, state_pool, norm_pre, norm_post, w_in, w_out, mlstm_b_i, mlstm_b_f,
              mlstm_norm, swa_sinks, ssd_conv_w, ssd_conv_b, ssd_dt_bias, ssd_A_log, ssd_D, ssd_norm,
              pool_lin, pool_scale):
    f32 = jnp.float32
    params = (norm_pre, norm_post, w_in, w_out, mlstm_b_i, mlstm_b_f, mlstm_norm, swa_sinks, ssd_conv_w,
              ssd_conv_b, ssd_dt_bias, ssd_A_log, ssd_D, ssd_norm, pool_lin, pool_scale)
    Bp = x_prompt.shape[0]
    pC0 = jnp.zeros((Bp, MLSTM_HEADS, MLSTM_DH, MLSTM_DH), f32)
    pn0 = jnp.zeros((Bp, MLSTM_HEADS, MLSTM_DH), f32)
    pm0 = jnp.zeros((Bp, MLSTM_HEADS), f32)
    pssd0 = jnp.zeros((Bp, SSD_HEADS, SSD_DH, SSD_DSTATE), f32)
    pconv0 = jnp.zeros((Bp, SSD_CONV - 1, SSD_CONV_DIM), x_prompt.dtype)
    ppool0 = jnp.zeros((Bp, 0, MIX_D), x_prompt.dtype)

    xp, xs = x_prompt, x_sample
    p_list, s_list = [], []
    for l in range(DEPTH):
        prm = tuple(a[l] for a in params)
        xp, st_p = mixer_layer(xp, prm, pC0, pn0, pm0, None, None, pssd0, pconv0, ppool0, True)
        p_list.append(st_p)
        xs, st_s = mixer_layer(xs, prm, state_mlstm_C[l], state_mlstm_n[l], state_mlstm_m[l], cache_swa_k[l],
                               cache_swa_v[l], state_ssd[l], state_ssd_conv[l], state_pool[l], False)
        s_list.append(st_s)
    p_C, p_n, p_m, p_k, p_v, p_ssd, p_conv, p_pool = stack_states(p_list)
    s_C, s_n, s_m, s_k, s_v, s_ssd, s_conv, s_pool = stack_states(s_list)
    return (xp, xs, p_C, p_n, p_m, p_k, p_v, p_ssd, p_conv, p_pool, s_C, s_n, s_m, s_k, s_v, s_ssd, s_conv, s_pool)
```

```python
import functools
import math

import jax
import jax.numpy as jnp
from jax import lax
from jax.experimental import pallas as pl
from jax.experimental.pallas import tpu as pltpu

F32 = jnp.float32
BF16 = jnp.bfloat16

D_MODEL = 4096
DEPTH = 4
PAST_LEN = 8192
MIX = D_MODEL // 4
MLSTM_DH = 128
MLSTM_HEADS = MIX // MLSTM_DH
MLSTM_CHUNK = 64
GATE_CAP = 15.0
SWA_DH = 64
SWA_HEADS = MIX // SWA_DH
SWA_KV_HEADS = 2
SWA_GROUP = SWA_HEADS // SWA_KV_HEADS
WINDOW = 128
ROPE_THETA = 10000.0
SSD_DH = 64
SSD_HEADS = MIX // SSD_DH
SSD_GROUPS = 4
SSD_HPG = SSD_HEADS // SSD_GROUPS
SSD_DSTATE = 128
SSD_CONV = 4
SSD_CONV_DIM = MIX + 2 * SSD_GROUPS * SSD_DSTATE
SSD_CHUNK = 128
POOL_WINDOWS = (2, 4, 8, 16)
POOL_GC = MIX // len(POOL_WINDOWS)
POOL_HIST = max(POOL_WINDOWS) - 1
RMS_EPS = 1e-6

LANE = 128
SUBLANE = 8
NEG = -1e30

_COL = {}
_off = 0
for _name, _width in (("qa", MIX), ("ka", MIX), ("va", MIX), ("oa", MIX), ("za", MIX),
                      ("qb", MIX), ("zb", MIX), ("zc", MIX), ("xbc", SSD_CONV_DIM),
                      ("ud", MIX), ("zd", MIX),
                      ("gate", LANE), ("kb", LANE), ("vb", LANE), ("dt", LANE)):
    _COL[_name] = (_off, _width)
    _off += _width
NPROJ = _off


def _colblock(name):
    off, width = _COL[name]
    assert off % width == 0
    return off // width


def _silu(x):
    return x * jax.nn.sigmoid(x)


def _dot(a, b):
    return jnp.dot(a, b, preferred_element_type=F32)


def _dot_nt(a, b):
    return lax.dot_general(a, b, (((1,), (1,)), ((), ())), preferred_element_type=F32)


def _dot_tn(a, b):
    return lax.dot_general(a, b, (((0,), (0,)), ((), ())), preferred_element_type=F32)


def _cumsum_rows(x):
    n = x.shape[0]
    r = lax.broadcasted_iota(jnp.int32, (n, n), 0)
    c = lax.broadcasted_iota(jnp.int32, (n, n), 1)
    tri = (c <= r).astype(F32)
    return jnp.dot(tri, x, preferred_element_type=F32, precision=lax.Precision.HIGHEST)


def _params(sem, vmem_mb):
    return pltpu.CompilerParams(dimension_semantics=sem, vmem_limit_bytes=vmem_mb * 1024 * 1024)


def _inproj_kernel(x_ref, g_ref, w_ref, o_ref, h_ref):
    @pl.when(pl.program_id(1) == 0)
    def _():
        x = x_ref[...]
        ms = jnp.mean(x * x, axis=-1, keepdims=True)
        h_ref[...] = (x * lax.rsqrt(ms + RMS_EPS) * g_ref[...]).astype(BF16)

    o_ref[...] = _dot(h_ref[...], w_ref[...])


def _inproj(x, g, w, tm, tn):
    m = x.shape[0]
    return pl.pallas_call(
        _inproj_kernel,
        grid=(m // tm, NPROJ // tn),
        in_specs=[pl.BlockSpec((tm, D_MODEL), lambda i, j: (i, 0)),
                  pl.BlockSpec((1, D_MODEL), lambda i, j: (0, 0)),
                  pl.BlockSpec((D_MODEL, tn), lambda i, j: (0, j))],
        out_specs=pl.BlockSpec((tm, tn), lambda i, j: (i, j)),
        out_shape=jax.ShapeDtypeStruct((m, NPROJ), F32),
        scratch_shapes=[pltpu.VMEM((tm, D_MODEL), BF16)],
        compiler_params=_params(("arbitrary", "arbitrary"), 48),
        name="inproj",
    )(x, g, w)


def _outproj_kernel(ya_ref, yb_ref, yc_ref, yd_ref, w_ref, x_ref, g_ref, o_ref, mix_ref, acc_ref, *, tn):
    j = pl.program_id(1)

    @pl.when(j == 0)
    def _():
        for i, r in enumerate((ya_ref, yb_ref, yc_ref, yd_ref)):
            mix_ref[:, i * MIX:(i + 1) * MIX] = r[...].astype(BF16)

    acc_ref[:, pl.ds(pl.multiple_of(j * tn, tn), tn)] = _dot(mix_ref[...], w_ref[...])

    @pl.when(j == pl.num_programs(1) - 1)
    def _():
        out = acc_ref[...]
        ms = jnp.mean(out * out, axis=-1, keepdims=True)
        o_ref[...] = x_ref[...] + out * lax.rsqrt(ms + RMS_EPS) * g_ref[...]


def _outproj(ys, w, x, g, tm, tn):
    m = x.shape[0]
    yspec = pl.BlockSpec((tm, MIX), lambda i, j: (i, 0))
    return pl.pallas_call(
        functools.partial(_outproj_kernel, tn=tn),
        grid=(m // tm, D_MODEL // tn),
        in_specs=[yspec, yspec, yspec, yspec,
                  pl.BlockSpec((D_MODEL, tn), lambda i, j: (0, j)),
                  pl.BlockSpec((tm, D_MODEL), lambda i, j: (i, 0)),
                  pl.BlockSpec((1, D_MODEL), lambda i, j: (0, 0))],
        out_specs=pl.BlockSpec((tm, D_MODEL), lambda i, j: (i, 0)),
        out_shape=jax.ShapeDtypeStruct((m, D_MODEL), F32),
        scratch_shapes=[pltpu.VMEM((tm, D_MODEL), BF16), pltpu.VMEM((tm, D_MODEL), F32)],
        compiler_params=_params(("arbitrary", "arbitrary"), 48),
        name="outproj",
    )(*ys, w, x, g)


def _mlstm_kernel(q_ref, k_ref, v_ref, o_ref, z_ref, gate_ref, bias_ref, mn_ref, c0_ref, n0_ref, m0_ref,
                  y_ref, c_ref, n_ref, m_ref, *, L, valid):
    @pl.when(pl.program_id(1) == 0)
    def _():
        c_ref[...] = c0_ref[...]
        n_ref[...] = n0_ref[...]
        m_ref[...] = m0_ref[...]

    row = lax.broadcasted_iota(jnp.int32, (L, LANE), 0)
    row_ok = row < valid
    gates = GATE_CAP * jnp.tanh((gate_ref[...] + bias_ref[...]) / GATE_CAP)
    logf = jnp.where(row_ok, jax.nn.log_sigmoid(gates), 0.0)
    b_all = _cumsum_rows(pltpu.roll(logf, LANE - MLSTM_HEADS, axis=1))
    r_all = jnp.where(row_ok, gates - b_all, NEG)
    r_all_t = r_all.T

    tr = lax.broadcasted_iota(jnp.int32, (L, L), 0)
    tc = lax.broadcasted_iota(jnp.int32, (L, L), 1)
    tril = tc <= tr

    for h in range(MLSTM_HEADS):
        sl = slice(h * MLSTM_DH, (h + 1) * MLSTM_DH)
        qh = q_ref[:, sl] * (MLSTM_DH ** -0.5)
        kh = k_ref[:, sl]
        vh = v_ref[:, sl]
        c_prev = c_ref[0, h]
        n_prev = n_ref[0, h:h + 1, :]
        m_prev = m_ref[0, h:h + 1, 0:1]
        b_col = b_all[:, h:h + 1]
        r_row = r_all_t[h:h + 1, :]

        dmat = jnp.where(tril, b_col + r_row, NEG)
        inter = b_col + m_prev
        mt = jnp.maximum(inter, jnp.max(dmat, axis=1, keepdims=True))
        s = jnp.exp(dmat - mt) * _dot_nt(qh, kh)
        w_prev_t = jnp.exp(inter - mt)
        num = w_prev_t * _dot(qh, c_prev) + _dot(s, vh)
        den = w_prev_t * jnp.sum(qh * n_prev, axis=1, keepdims=True) + jnp.sum(s, axis=1, keepdims=True)
        hh = num / jnp.maximum(jnp.abs(den), jnp.exp(-mt))

        b_last = b_all[L - 1:L, h:h + 1]
        dec = b_last + r_all[:, h:h + 1]
        m_new = jnp.maximum(b_last + m_prev, jnp.max(dec, axis=0, keepdims=True))
        w_prev = jnp.exp(b_last + m_prev - m_new)
        kw = kh * jnp.exp(dec - m_new)
        c_ref[0, h] = w_prev * c_prev + _dot_tn(kw, vh)
        n_ref[0, h:h + 1, :] = w_prev * n_prev + jnp.sum(kw, axis=0, keepdims=True)
        m_ref[0, h:h + 1, :] = jnp.broadcast_to(m_new, (1, LANE))

        hn = hh * lax.rsqrt(jnp.mean(hh * hh, axis=1, keepdims=True) + RMS_EPS) * mn_ref[:, sl]
        y_ref[:, sl] = (hn * jax.nn.sigmoid(o_ref[:, sl]) * _silu(z_ref[:, sl])).astype(y_ref.dtype)


def _mlstm(proj, bias, mnorm, c0, n0, m0, n_seq, t_pad, L, valid, y_dtype):
    nc = t_pad // L
    wide = lambda name: pl.BlockSpec((L, MIX), lambda s, c, cb=_colblock(name): (s * nc + c, cb))
    vec = lambda n: pl.BlockSpec((1, n), lambda s, c: (0, 0))
    st4 = pl.BlockSpec((1, MLSTM_HEADS, MLSTM_DH, MLSTM_DH), lambda s, c: (s, 0, 0, 0))
    st3 = pl.BlockSpec((1, MLSTM_HEADS, LANE), lambda s, c: (s, 0, 0))
    return pl.pallas_call(
        functools.partial(_mlstm_kernel, L=L, valid=valid),
        grid=(n_seq, nc),
        in_specs=[wide("qa"), wide("ka"), wide("va"), wide("oa"), wide("za"),
                  pl.BlockSpec((L, LANE), lambda s, c, cb=_colblock("gate"): (s * nc + c, cb)),
                  vec(LANE), vec(MIX), st4, st3, st3],
        out_specs=[pl.BlockSpec((L, MIX), lambda s, c: (s * nc + c, 0)), st4, st3, st3],
        out_shape=[jax.ShapeDtypeStruct((n_seq * t_pad, MIX), y_dtype),
                   jax.ShapeDtypeStruct(c0.shape, F32),
                   jax.ShapeDtypeStruct(n0.shape, F32),
                   jax.ShapeDtypeStruct(m0.shape, F32)],
        compiler_params=_params(("arbitrary", "arbitrary"), 32),
        name="mlstm",
    )(proj, proj, proj, proj, proj, proj, bias, mnorm, c0, n0, m0)


def _swa_kernel(q_ref, k_ref, v_ref, z_ref, cos_ref, sin_ref, sink_ref, kc_ref, vc_ref,
                y_ref, ko_ref, vo_ref, *, Lq, valid, has_cache):
    blk = pl.program_id(1)

    @pl.when(blk == 0)
    def _():
        ko_ref[...] = kc_ref[...]
        vo_ref[...] = vc_ref[...]

    cos = cos_ref[...]
    sin = sin_ref[...]
    lane = lax.broadcasted_iota(jnp.int32, (Lq, LANE), 1)
    first_half = (lane % SWA_DH) < (SWA_DH // 2)

    def rope(x):
        partner = jnp.where(first_half, pltpu.roll(x, LANE - SWA_DH // 2, axis=1), pltpu.roll(x, SWA_DH // 2, axis=1))
        return x * cos + partner * sin

    k_cur = rope(k_ref[...])
    v_cur = v_ref[...]
    k_prev = ko_ref[0]
    v_prev = vo_ref[0]

    qr = lax.broadcasted_iota(jnp.int32, (Lq, Lq), 0)
    qc = lax.broadcasted_iota(jnp.int32, (Lq, Lq), 1)
    mask_cur = (qc <= qr) & (qc < valid)
    pr = lax.broadcasted_iota(jnp.int32, (Lq, WINDOW), 0)
    pc = lax.broadcasted_iota(jnp.int32, (Lq, WINDOW), 1)
    prev_ok = jnp.logical_or(has_cache, blk > 0)
    mask_prev = (pc > pr) & prev_ok

    for pair in range(SWA_HEADS // 2):
        csl = slice(pair * LANE, (pair + 1) * LANE)
        q_pair = rope(q_ref[:, csl]) * (SWA_DH ** -0.5)
        outs = []
        for sub in range(2):
            h = 2 * pair + sub
            g = h // SWA_GROUP
            gsl = slice(g * SWA_DH, (g + 1) * SWA_DH)
            qh = q_pair[:, sub * SWA_DH:(sub + 1) * SWA_DH]
            s_cur = jnp.where(mask_cur, _dot_nt(qh, k_cur[:, gsl]), NEG)
            s_prev = jnp.where(mask_prev, _dot_nt(qh, k_prev[:, gsl]), NEG)
            sink = sink_ref[0:1, h:h + 1]
            mx = jnp.maximum(jnp.maximum(jnp.max(s_cur, axis=1, keepdims=True),
                                         jnp.max(s_prev, axis=1, keepdims=True)), sink)
            p_cur = jnp.exp(s_cur - mx)
            p_prev = jnp.exp(s_prev - mx)
            den = (jnp.sum(p_cur, axis=1, keepdims=True) + jnp.sum(p_prev, axis=1, keepdims=True)
                   + jnp.exp(sink - mx))
            outs.append((_dot(p_cur, v_cur[:, gsl]) + _dot(p_prev, v_prev[:, gsl])) / den)
        y_ref[:, csl] = (jnp.concatenate(outs, axis=1) * _silu(z_ref[:, csl])).astype(y_ref.dtype)

    if valid == WINDOW:
        ko_ref[0] = k_cur
        vo_ref[0] = v_cur
    else:
        ko_ref[0, 0:WINDOW - valid, :] = k_prev[valid:, :]
        ko_ref[0, WINDOW - valid:WINDOW, :] = k_cur[0:valid, :]
        vo_ref[0, 0:WINDOW - valid, :] = v_prev[valid:, :]
        vo_ref[0, WINDOW - valid:WINDOW, :] = v_cur[0:valid, :]


def _swa(proj, cos, sin, sinks, kc, vc, n_seq, t_pad, Lq, valid, has_cache, table_per_block, y_dtype):
    nb = t_pad // Lq
    rowblk = lambda s, b: s * nb + b
    wide = lambda name: pl.BlockSpec((Lq, MIX), lambda s, b, cb=_colblock(name): (rowblk(s, b), cb))
    narrow = lambda name: pl.BlockSpec((Lq, LANE), lambda s, b, cb=_colblock(name): (rowblk(s, b), cb))
    table = pl.BlockSpec((Lq, LANE), (lambda s, b: (b, 0)) if table_per_block else (lambda s, b: (0, 0)))
    cache = pl.BlockSpec((1, WINDOW, LANE), lambda s, b: (s, 0, 0))
    return pl.pallas_call(
        functools.partial(_swa_kernel, Lq=Lq, valid=valid, has_cache=has_cache),
        grid=(n_seq, nb),
        in_specs=[wide("qb"), narrow("kb"), narrow("vb"), wide("zb"), table, table,
                  pl.BlockSpec((1, LANE), lambda s, b: (0, 0)), cache, cache],
        out_specs=[pl.BlockSpec((Lq, MIX), lambda s, b: (rowblk(s, b), 0)), cache, cache],
        out_shape=[jax.ShapeDtypeStruct((n_seq * t_pad, MIX), y_dtype),
                   jax.ShapeDtypeStruct(kc.shape, F32),
                   jax.ShapeDtypeStruct(vc.shape, F32)],
        compiler_params=_params(("arbitrary", "arbitrary"), 32),
        name="swa",
    )(proj, proj, proj, proj, cos, sin, sinks, kc, vc)


_CONV_PAD = SUBLANE


def _ssd_kernel(xbc_ref, z_ref, dt_ref, cw_ref, cb_ref, dtb_ref, alog_ref, dskip_ref, sn_ref, conv0_ref, s0_ref,
                y_ref, convo_ref, so_ref, full_ref, *, L, valid):
    hist = SSD_CONV - 1

    @pl.when(pl.program_id(1) == 0)
    def _():
        so_ref[...] = s0_ref[...]
        full_ref[_CONV_PAD - hist:_CONV_PAD, :] = conv0_ref[0]

    full_ref[_CONV_PAD:_CONV_PAD + L, :] = xbc_ref[...]
    acc = cb_ref[...] + full_ref[_CONV_PAD - hist:_CONV_PAD - hist + L, :] * cw_ref[0:1, :]
    for j in range(1, SSD_CONV):
        acc = acc + full_ref[_CONV_PAD - hist + j:_CONV_PAD - hist + j + L, :] * cw_ref[j:j + 1, :]
    xc = _silu(acc)
    new_hist = full_ref[_CONV_PAD + valid - hist:_CONV_PAD + valid, :]
    convo_ref[0] = new_hist
    full_ref[_CONV_PAD - hist:_CONV_PAD, :] = new_hist

    nbc = SSD_GROUPS * SSD_DSTATE
    row_ok = lax.broadcasted_iota(jnp.int32, (L, LANE), 0) < valid
    dtv = jnp.where(row_ok, jax.nn.softplus(dt_ref[...] + dtb_ref[...]), 0.0)
    a = dtv * (-jnp.exp(alog_ref[...]))
    cum = _cumsum_rows(a)
    cum_t = cum.T
    tr = lax.broadcasted_iota(jnp.int32, (L, L), 0)
    tc = lax.broadcasted_iota(jnp.int32, (L, L), 1)
    tril = tc <= tr

    ys = []
    for g in range(SSD_GROUPS):
        bg = xc[:, MIX + g * SSD_DSTATE:MIX + (g + 1) * SSD_DSTATE]
        cg = xc[:, MIX + nbc + g * SSD_DSTATE:MIX + nbc + (g + 1) * SSD_DSTATE]
        cb = _dot_nt(cg, bg)
        for hh in range(SSD_HPG):
            h = g * SSD_HPG + hh
            c_col = cum[:, h:h + 1]
            c_row = cum_t[h:h + 1, :]
            c_last = cum[L - 1:L, h:h + 1]
            lm = jnp.where(tril, jnp.exp(jnp.where(tril, c_col - c_row, 0.0)), 0.0)
            xh = xc[:, h * SSD_DH:(h + 1) * SSD_DH]
            xdt = xh * dtv[:, h:h + 1]
            st = so_ref[0, h]
            yh = _dot(cb * lm, xdt) + jnp.exp(c_col) * _dot_nt(cg, st)
            so_ref[0, h] = jnp.exp(c_last) * st + _dot_tn(xdt * jnp.exp(c_last - c_col), bg)
            ys.append(yh)
    xs = xc[:, :MIX]
    yc = (jnp.concatenate(ys, axis=1) + dskip_ref[...] * xs) * _silu(z_ref[...])
    y_ref[...] = (yc * lax.rsqrt(jnp.mean(yc * yc, axis=1, keepdims=True) + RMS_EPS) * sn_ref[...]).astype(y_ref.dtype)


def _ssd(proj, conv_w, conv_b, dt_bias, a_log, d_skip, snorm, conv0, s0, n_seq, t_pad, L, valid, y_dtype):
    nc = t_pad // L
    rowblk = lambda s, c: s * nc + c
    vec = lambda n: pl.BlockSpec((1, n), lambda s, c: (0, 0))
    conv_st = pl.BlockSpec((1, SSD_CONV - 1, SSD_CONV_DIM), lambda s, c: (s, 0, 0))
    ssd_st = pl.BlockSpec((1, SSD_HEADS, SSD_DH, SSD_DSTATE), lambda s, c: (s, 0, 0, 0))
    return pl.pallas_call(
        functools.partial(_ssd_kernel, L=L, valid=valid),
        grid=(n_seq, nc),
        in_specs=[pl.BlockSpec((L, SSD_CONV_DIM), lambda s, c, cb=_colblock("xbc"): (rowblk(s, c), cb)),
                  pl.BlockSpec((L, MIX), lambda s, c, cb=_colblock("zc"): (rowblk(s, c), cb)),
                  pl.BlockSpec((L, LANE), lambda s, c, cb=_colblock("dt"): (rowblk(s, c), cb)),
                  pl.BlockSpec((SSD_CONV, SSD_CONV_DIM), lambda s, c: (0, 0)),
                  vec(SSD_CONV_DIM), vec(LANE), vec(LANE), vec(MIX), vec(MIX), conv_st, ssd_st],
        out_specs=[pl.BlockSpec((L, MIX), lambda s, c: (rowblk(s, c), 0)), conv_st, ssd_st],
        out_shape=[jax.ShapeDtypeStruct((n_seq * t_pad, MIX), y_dtype),
                   jax.ShapeDtypeStruct(conv0.shape, F32),
                   jax.ShapeDtypeStruct(s0.shape, F32)],
        scratch_shapes=[pltpu.VMEM((_CONV_PAD + L, SSD_CONV_DIM), F32)],
        compiler_params=_params(("arbitrary", "arbitrary"), 32),
        name="ssd",
    )(proj, proj, proj, conv_w, conv_b, dt_bias, a_log, d_skip, snorm, conv0, s0)


_POOL_PAD = 2 * SUBLANE


def _pool_kernel(u_ref, z_ref, lin_ref, scale_ref, p0_ref, y_ref, po_ref, full_ref, *, L, valid, n_hist):
    step = pl.program_id(1)

    @pl.when(step == 0)
    def _():
        full_ref[_POOL_PAD - POOL_HIST:_POOL_PAD, :] = p0_ref[0]

    full_ref[_POOL_PAD:_POOL_PAD + L, :] = u_ref[...]
    pos = n_hist + step * L + lax.broadcasted_iota(jnp.int32, (L, 1), 0)
    for g, w in enumerate(POOL_WINDOWS):
        sl = slice(g * POOL_GC, (g + 1) * POOL_GC)
        cur = full_ref[_POOL_PAD:_POOL_PAD + L, sl]
        tot = cur
        for j in range(1, w):
            tot = tot + full_ref[_POOL_PAD - j:_POOL_PAD - j + L, sl]
        cnt = jnp.minimum(pos + 1, w).astype(F32)
        d = tot / cnt - cur
        y = _dot(d, lin_ref[g]) * scale_ref[:, sl] * _silu(z_ref[:, sl])
        y_ref[:, sl] = y.astype(y_ref.dtype)
    new_hist = full_ref[_POOL_PAD + valid - POOL_HIST:_POOL_PAD + valid, :]
    po_ref[0] = new_hist
    full_ref[_POOL_PAD - POOL_HIST:_POOL_PAD, :] = new_hist


def _pool(proj, lin, scale, p0, n_seq, t_pad, L, valid, n_hist, y_dtype):
    nt = t_pad // L
    rowblk = lambda s, c: s * nt + c
    hist = pl.BlockSpec((1, POOL_HIST, MIX), lambda s, c: (s, 0, 0))
    return pl.pallas_call(
        functools.partial(_pool_kernel, L=L, valid=valid, n_hist=n_hist),
        grid=(n_seq, nt),
        in_specs=[pl.BlockSpec((L, MIX), lambda s, c, cb=_colblock("ud"): (rowblk(s, c), cb)),
                  pl.BlockSpec((L, MIX), lambda s, c, cb=_colblock("zd"): (rowblk(s, c), cb)),
                  pl.BlockSpec((len(POOL_WINDOWS), POOL_GC, POOL_GC), lambda s, c: (0, 0, 0)),
                  pl.BlockSpec((1, MIX), lambda s, c: (0, 0)), hist],
        out_specs=[pl.BlockSpec((L, MIX), lambda s, c: (rowblk(s, c), 0)), hist],
        out_shape=[jax.ShapeDtypeStruct((n_seq * t_pad, MIX), y_dtype),
                   jax.ShapeDtypeStruct(p0.shape, F32)],
        scratch_shapes=[pltpu.VMEM((_POOL_PAD + L, MIX), F32)],
        compiler_params=_params(("arbitrary", "arbitrary"), 32),
        name="pool",
    )(proj, proj, lin, scale, p0)


_REF_SIZES = (MIX, MIX, MIX, MIX, MIX, MLSTM_HEADS, MLSTM_HEADS,
              SWA_HEADS * SWA_DH, SWA_KV_HEADS * SWA_DH, SWA_KV_HEADS * SWA_DH, MIX,
              MIX, SSD_CONV_DIM, SSD_HEADS, MIX, MIX)
_REF_NAMES = ("qa", "ka", "va", "oa", "za", "ia", "fa", "qb", "kb", "vb", "zb", "zc", "xbc", "dt", "ud", "zd")


def _regroup_w_in(w_in):
    offs = {}
    o = 0
    for name, size in zip(_REF_NAMES, _REF_SIZES):
        offs[name] = (o, size)
        o += size
    take = lambda name: lax.slice_in_dim(w_in, offs[name][0], offs[name][0] + offs[name][1], axis=2)
    zeros = lambda n: jnp.zeros(w_in.shape[:2] + (n,), w_in.dtype)
    parts = []
    for name in _COL:
        if name == "gate":
            parts += [take("ia"), take("fa"), zeros(LANE - 2 * MLSTM_HEADS)]
        elif name == "dt":
            parts += [take("dt"), zeros(LANE - SSD_HEADS)]
        else:
            parts.append(take(name))
    return jnp.concatenate(parts, axis=2).astype(BF16)


def _pad_lanes(v, n=LANE):
    return jnp.pad(v, (0, n - v.shape[0])).reshape(1, n)


def _rope_tables(pos):
    half = SWA_DH // 2
    inv = ROPE_THETA ** (-jnp.arange(half, dtype=F32) / half)
    ang = pos.astype(F32)[:, None] * inv[None, :]
    cos, sin = jnp.cos(ang), jnp.sin(ang)
    reps = LANE // SWA_DH
    return jnp.tile(jnp.concatenate([cos, cos], axis=1), (1, reps)), jnp.tile(jnp.concatenate([-sin, sin], axis=1), (1, reps))


class _Path:
    def __init__(self, n_seq, t_pad, valid, chunk_a, chunk_b, chunk_c, chunk_d, has_cache, n_hist, tm_in, tm_out, y_dtype):
        self.n_seq, self.t_pad, self.valid = n_seq, t_pad, valid
        self.chunk_a, self.chunk_b, self.chunk_c, self.chunk_d = chunk_a, chunk_b, chunk_c, chunk_d
        self.has_cache, self.n_hist, self.tm_in, self.tm_out, self.y_dtype = has_cache, n_hist, tm_in, tm_out, y_dtype


def _layer(x, prm, states, tables, path):
    (g_pre, g_post, w_in, w_out, gate_bias, mnorm, sinks, conv_w, conv_b, dt_bias, a_log, d_skip, snorm,
     pool_lin, pool_scale) = prm
    c0, n0, m0, kc, vc, s0, conv0, p0 = states
    cos, sin = tables
    p = path
    proj = _inproj(x, g_pre, w_in, p.tm_in, 512)
    ya, c1, n1, m1 = _mlstm(proj, gate_bias, mnorm, c0, n0, m0, p.n_seq, p.t_pad, p.chunk_a,
                            min(p.valid, p.chunk_a), p.y_dtype)
    yb, k1, v1 = _swa(proj, cos, sin, sinks, kc, vc, p.n_seq, p.t_pad, p.chunk_b, min(p.valid, p.chunk_b),
                      p.has_cache, not p.has_cache, p.y_dtype)
    yc, conv1, s1 = _ssd(proj, conv_w, conv_b, dt_bias, a_log, d_skip, snorm, conv0, s0, p.n_seq, p.t_pad,
                         p.chunk_c, min(p.valid, p.chunk_c), p.y_dtype)
    yd, p1 = _pool(proj, pool_lin, pool_scale, p0, p.n_seq, p.t_pad, p.chunk_d, min(p.valid, p.chunk_d),
                   p.n_hist, p.y_dtype)
    x = _outproj((ya, yb, yc, yd), w_out, x, g_post, p.tm_out, 512)
    return x, (c1, n1, m1, k1, v1, s1, conv1, p1)


def kernel(x_prompt, x_sample, state_mlstm_C, state_mlstm_n, state_mlstm_m, cache_swa_k, cache_swa_v, state_ssd,
           state_ssd_conv, state_pool, norm_pre, norm_post, w_in, w_out, mlstm_b_i, mlstm_b_f, mlstm_norm,
           swa_sinks, ssd_conv_w, ssd_conv_b, ssd_dt_bias, ssd_A_log, ssd_D, ssd_norm, pool_lin, pool_scale):
    bp, seq, _ = x_prompt.shape
    bs, dec_seq, _ = x_sample.shape
    t_s = SUBLANE * pl.cdiv(dec_seq, SUBLANE)

    prompt = _Path(bp, seq, seq, MLSTM_CHUNK, WINDOW, SSD_CHUNK, 256, False, 0, 512, 256, BF16)
    sample = _Path(bs, t_s, dec_seq, t_s, t_s, t_s, t_s, True, POOL_HIST, bs * t_s, bs * t_s, F32)

    w_in_r = _regroup_w_in(w_in)
    w_out_b = w_out.astype(BF16)

    xp = x_prompt.reshape(bp * seq, D_MODEL)
    xs = jnp.pad(x_sample, ((0, 0), (0, t_s - dec_seq), (0, 0))).reshape(bs * t_s, D_MODEL)

    tab_p = _rope_tables(jnp.arange(seq))
    tab_s = _rope_tables(PAST_LEN + jnp.arange(t_s))

    zeros = lambda *shape: jnp.zeros(shape, F32)
    p_states0 = (zeros(bp, MLSTM_HEADS, MLSTM_DH, MLSTM_DH), zeros(bp, MLSTM_HEADS, LANE), zeros(bp, MLSTM_HEADS, LANE),
                 zeros(bp, WINDOW, LANE), zeros(bp, WINDOW, LANE),
                 zeros(bp, SSD_HEADS, SSD_DH, SSD_DSTATE), zeros(bp, SSD_CONV - 1, SSD_CONV_DIM),
                 zeros(bp, POOL_HIST, MIX))

    p_list, s_list = [], []
    for l in range(DEPTH):
        gate_bias = _pad_lanes(jnp.concatenate([mlstm_b_i[l], mlstm_b_f[l]]))
        prm = (norm_pre[l].reshape(1, D_MODEL), norm_post[l].reshape(1, D_MODEL), w_in_r[l], w_out_b[l],
               gate_bias, mlstm_norm[l].reshape(1, MIX), _pad_lanes(swa_sinks[l]),
               ssd_conv_w[l], ssd_conv_b[l].reshape(1, SSD_CONV_DIM), _pad_lanes(ssd_dt_bias[l]),
               _pad_lanes(ssd_A_log[l]), jnp.repeat(ssd_D[l], SSD_DH).reshape(1, MIX), ssd_norm[l].reshape(1, MIX),
               pool_lin[l], pool_scale[l].reshape(1, MIX))
        s_states0 = (state_mlstm_C[l], state_mlstm_n[l],
                     jnp.broadcast_to(state_mlstm_m[l][:, :, None], (bs, MLSTM_HEADS, LANE)),
                     cache_swa_k[l].reshape(bs, WINDOW, LANE), cache_swa_v[l].reshape(bs, WINDOW, LANE),
                     state_ssd[l], state_ssd_conv[l], state_pool[l])
        xp, st_p = _layer(xp, prm, p_states0, tab_p, prompt)
        xs, st_s = _layer(xs, prm, s_states0, tab_s, sample)
        p_list.append(st_p)
        s_list.append(st_s)

    def assemble(lst, b):
        c, n, m, k, v, s, conv, pool = (jnp.stack([st[i] for st in lst], axis=0) for i in range(8))
        kv_shape = (DEPTH, b, WINDOW, SWA_KV_HEADS, SWA_DH)
        return c, n, m[..., 0], k.reshape(kv_shape), v.reshape(kv_shape), s, conv, pool

    y_prompt = xp.reshape(bp, seq, D_MODEL)
    y_sample = xs.reshape(bs, t_s, D_MODEL)[:, :dec_seq]
    return (y_prompt, y_sample) + assemble(p_list, bp) + assemble(s_list, bs)
```

```python
import functools
import math

import jax
import jax.numpy as jnp
from jax import lax
from jax.experimental import pallas as pl
from jax.experimental.pallas import tpu as pltpu

F32 = jnp.float32
BF16 = jnp.bfloat16

D_MODEL = 4096
DEPTH = 4
PAST_LEN = 8192
MIX = D_MODEL // 4
MLSTM_DH = 128
MLSTM_HEADS = MIX // MLSTM_DH
MLSTM_CHUNK = 64
GATE_CAP = 15.0
SWA_DH = 64
SWA_HEADS = MIX // SWA_DH
SWA_KV_HEADS = 2
SWA_GROUP = SWA_HEADS // SWA_KV_HEADS
WINDOW = 128
ROPE_THETA = 10000.0
SSD_DH = 64
SSD_HEADS = MIX // SSD_DH
SSD_GROUPS = 4
SSD_HPG = SSD_HEADS // SSD_GROUPS
SSD_DSTATE = 128
SSD_CONV = 4
SSD_CONV_DIM = MIX + 2 * SSD_GROUPS * SSD_DSTATE
SSD_CHUNK = 128
POOL_WINDOWS = (2, 4, 8, 16)
POOL_GC = MIX // len(POOL_WINDOWS)
POOL_HIST = max(POOL_WINDOWS) - 1
RMS_EPS = 1e-6

LANE = 128
SUBLANE = 8
NEG = -1e30

_COL = {}
_off = 0
for _name, _width in (("qa", MIX), ("ka", MIX), ("va", MIX), ("oa", MIX), ("za", MIX),
                      ("qb", MIX), ("zb", MIX), ("zc", MIX), ("xbc", SSD_CONV_DIM),
                      ("ud", MIX), ("zd", MIX),
                      ("gate", LANE), ("kb", LANE), ("vb", LANE), ("dt", LANE)):
    _COL[_name] = (_off, _width)
    _off += _width
NPROJ = _off


def _colblock(name):
    off, width = _COL[name]
    assert off % width == 0
    return off // width


def _silu(x):
    return x * jax.nn.sigmoid(x)


def _dot(a, b):
    return jnp.dot(a, b, preferred_element_type=F32)


def _dot_nt(a, b):
    return lax.dot_general(a, b, (((1,), (1,)), ((), ())), preferred_element_type=F32)


def _dot_tn(a, b):
    return lax.dot_general(a, b, (((0,), (0,)), ((), ())), preferred_element_type=F32)


def _cumsum_rows(x):
    n = x.shape[0]
    r = lax.broadcasted_iota(jnp.int32, (n, n), 0)
    c = lax.broadcasted_iota(jnp.int32, (n, n), 1)
    tri = (c <= r).astype(F32)
    return jnp.dot(tri, x, preferred_element_type=F32, precision=lax.Precision.HIGHEST)


def _params(sem, vmem_mb):
    return pltpu.CompilerParams(dimension_semantics=sem, vmem_limit_bytes=vmem_mb * 1024 * 1024)


_NORM_ROWS = 64


def _inproj_kernel(x_ref, g_ref, w_ref, o_ref, h_ref):
    @pl.when(pl.program_id(1) == 0)
    def _():
        def body(it, carry):
            rs = pl.ds(pl.multiple_of(it * _NORM_ROWS, _NORM_ROWS), _NORM_ROWS)
            x = x_ref[rs, :]
            ms = jnp.mean(x * x, axis=-1, keepdims=True)
            h_ref[rs, :] = (x * lax.rsqrt(ms + RMS_EPS) * g_ref[...]).astype(BF16)
            return carry

        lax.fori_loop(0, x_ref.shape[0] // _NORM_ROWS, body, 0)

    o_ref[...] = _dot(h_ref[...], w_ref[...])


def _inproj(x, g, w_all, layer, tm, tn):
    m = x.shape[0]
    return pl.pallas_call(
        _inproj_kernel,
        grid=(m // tm, NPROJ // tn),
        in_specs=[pl.BlockSpec((tm, D_MODEL), lambda i, j: (i, 0)),
                  pl.BlockSpec((1, D_MODEL), lambda i, j: (0, 0)),
                  pl.BlockSpec((None, D_MODEL, tn), lambda i, j: (layer, 0, j))],
        out_specs=pl.BlockSpec((tm, tn), lambda i, j: (i, j)),
        out_shape=jax.ShapeDtypeStruct((m, NPROJ), F32),
        scratch_shapes=[pltpu.VMEM((tm, D_MODEL), BF16)],
        compiler_params=_params(("arbitrary", "arbitrary"), 56),
        name="inproj",
    )(x, g, w_all)


def _outproj_kernel(ya_ref, yb_ref, yc_ref, yd_ref, w_ref, x_ref, g_ref, o_ref, acc_ref, ss_ref, *, tn, n_row_tiles):
    i = pl.program_id(0)
    j = pl.program_id(1)
    slot = i % 2
    col = pl.ds(pl.multiple_of(j * tn, tn), tn)

    @pl.when(i < n_row_tiles)
    def _():
        out = None
        for part, r in enumerate((ya_ref, yb_ref, yc_ref, yd_ref)):
            term = _dot(r[...].astype(BF16), w_ref[part * MIX:(part + 1) * MIX, :])
            out = term if out is None else out + term
        acc_ref[slot, :, col] = out
        sq = jnp.sum(out * out, axis=1, keepdims=True)
        ss_ref[slot] = jnp.where(j == 0, sq, ss_ref[slot] + sq)

    @pl.when(i > 0)
    def _():
        rs = lax.rsqrt(ss_ref[1 - slot] * (1.0 / D_MODEL) + RMS_EPS)
        o_ref[...] = x_ref[...] + acc_ref[1 - slot, :, col] * rs * g_ref[...]


def _outproj(ys, w_all, layer, x, g, tm, tn):
    m = x.shape[0]
    n_row_tiles = m // tm
    yspec = pl.BlockSpec((tm, MIX), lambda i, j: (jnp.minimum(i, n_row_tiles - 1), 0))
    lagged = pl.BlockSpec((tm, tn), lambda i, j: (jnp.maximum(i - 1, 0), jnp.where(i > 0, j, 0)))
    return pl.pallas_call(
        functools.partial(_outproj_kernel, tn=tn, n_row_tiles=n_row_tiles),
        grid=(n_row_tiles + 1, D_MODEL // tn),
        in_specs=[yspec, yspec, yspec, yspec,
                  pl.BlockSpec((None, D_MODEL, tn), lambda i, j: (layer, 0, j)),
                  lagged,
                  pl.BlockSpec((1, tn), lambda i, j: (0, j))],
        out_specs=lagged,
        out_shape=jax.ShapeDtypeStruct((m, D_MODEL), F32),
        scratch_shapes=[pltpu.VMEM((2, tm, D_MODEL), F32), pltpu.VMEM((2, tm, 1), F32)],
        compiler_params=_params(("arbitrary", "arbitrary"), 48),
        name="outproj",
    )(*ys, w_all, x, g)


def _mlstm_kernel(q_ref, k_ref, v_ref, o_ref, z_ref, gate_ref, bias_ref, mn_ref, c0_ref, n0_ref, m0_ref,
                  y_ref, c_ref, n_ref, m_ref, *, L, valid):
    @pl.when(pl.program_id(1) == 0)
    def _():
        c_ref[...] = c0_ref[...]
        n_ref[...] = n0_ref[...]
        m_ref[...] = m0_ref[...]

    row = lax.broadcasted_iota(jnp.int32, (L, LANE), 0)
    row_ok = row < valid
    gates = GATE_CAP * jnp.tanh((gate_ref[...] + bias_ref[...]) / GATE_CAP)
    logf = jnp.where(row_ok, jax.nn.log_sigmoid(gates), 0.0)
    b_all = _cumsum_rows(pltpu.roll(logf, LANE - MLSTM_HEADS, axis=1))
    r_all = jnp.where(row_ok, gates - b_all, NEG)
    r_all_t = r_all.T

    tr = lax.broadcasted_iota(jnp.int32, (L, L), 0)
    tc = lax.broadcasted_iota(jnp.int32, (L, L), 1)
    tril = tc <= tr

    heads = range(MLSTM_HEADS)
    sl = [slice(h * MLSTM_DH, (h + 1) * MLSTM_DH) for h in heads]
    q = [q_ref[:, sl[h]] * (MLSTM_DH ** -0.5) for h in heads]
    k = [k_ref[:, sl[h]] for h in heads]
    v = [v_ref[:, sl[h]] for h in heads]
    c_prev = [c_ref[0, h] for h in heads]
    n_prev = [n_ref[0, h:h + 1, :] for h in heads]
    m_prev = [m_ref[0, h:h + 1, 0:1] for h in heads]
    qk = [_dot_nt(q[h], k[h]) for h in heads]
    qc = [_dot(q[h], c_prev[h]) for h in heads]
    qn = [jnp.sum(q[h] * n_prev[h], axis=1, keepdims=True) for h in heads]

    b_col = [b_all[:, h:h + 1] for h in heads]
    dmat = [jnp.where(tril, b_col[h] + r_all_t[h:h + 1, :], NEG) for h in heads]
    inter = [b_col[h] + m_prev[h] for h in heads]
    mt = [jnp.maximum(inter[h], jnp.max(dmat[h], axis=1, keepdims=True)) for h in heads]
    s = [jnp.exp(dmat[h] - mt[h]) * qk[h] for h in heads]
    sv = [_dot(s[h], v[h]) for h in heads]
    w_prev_t = [jnp.exp(inter[h] - mt[h]) for h in heads]
    den = [w_prev_t[h] * qn[h] + jnp.sum(s[h], axis=1, keepdims=True) for h in heads]
    hh = [(w_prev_t[h] * qc[h] + sv[h]) / jnp.maximum(jnp.abs(den[h]), jnp.exp(-mt[h])) for h in heads]

    b_last = [b_all[L - 1:L, h:h + 1] for h in heads]
    dec = [b_last[h] + r_all[:, h:h + 1] for h in heads]
    m_new = [jnp.maximum(b_last[h] + m_prev[h], jnp.max(dec[h], axis=0, keepdims=True)) for h in heads]
    w_prev = [jnp.exp(b_last[h] + m_prev[h] - m_new[h]) for h in heads]
    kw = [k[h] * jnp.exp(dec[h] - m_new[h]) for h in heads]
    c_new = [w_prev[h] * c_prev[h] + _dot_tn(kw[h], v[h]) for h in heads]
    n_new = [w_prev[h] * n_prev[h] + jnp.sum(kw[h], axis=0, keepdims=True) for h in heads]

    hn = [hh[h] * lax.rsqrt(jnp.mean(hh[h] * hh[h], axis=1, keepdims=True) + RMS_EPS) * mn_ref[:, sl[h]] for h in heads]
    ys = [hn[h] * jax.nn.sigmoid(o_ref[:, sl[h]]) * _silu(z_ref[:, sl[h]]) for h in heads]

    y_ref[...] = jnp.concatenate(ys, axis=1).astype(y_ref.dtype)
    for h in heads:
        c_ref[0, h] = c_new[h]
    n_ref[0] = jnp.concatenate(n_new, axis=0)
    m_ref[0] = jnp.concatenate([jnp.broadcast_to(m_new[h], (1, LANE)) for h in heads], axis=0)


def _mlstm(proj, bias, mnorm, c0, n0, m0, n_seq, t_pad, L, valid, y_dtype):
    nc = t_pad // L
    wide = lambda name: pl.BlockSpec((L, MIX), lambda s, c, cb=_colblock(name): (s * nc + c, cb))
    vec = lambda n: pl.BlockSpec((1, n), lambda s, c: (0, 0))
    st4 = pl.BlockSpec((1, MLSTM_HEADS, MLSTM_DH, MLSTM_DH), lambda s, c: (s, 0, 0, 0))
    st3 = pl.BlockSpec((1, MLSTM_HEADS, LANE), lambda s, c: (s, 0, 0))
    return pl.pallas_call(
        functools.partial(_mlstm_kernel, L=L, valid=valid),
        grid=(n_seq, nc),
        in_specs=[wide("qa"), wide("ka"), wide("va"), wide("oa"), wide("za"),
                  pl.BlockSpec((L, LANE), lambda s, c, cb=_colblock("gate"): (s * nc + c, cb)),
                  vec(LANE), vec(MIX), st4, st3, st3],
        out_specs=[pl.BlockSpec((L, MIX), lambda s, c: (s * nc + c, 0)), st4, st3, st3],
        out_shape=[jax.ShapeDtypeStruct((n_seq * t_pad, MIX), y_dtype),
                   jax.ShapeDtypeStruct(c0.shape, F32),
                   jax.ShapeDtypeStruct(n0.shape, F32),
                   jax.ShapeDtypeStruct(m0.shape, F32)],
        compiler_params=_params(("arbitrary", "arbitrary"), 32),
        name="mlstm",
    )(proj, proj, proj, proj, proj, proj, bias, mnorm, c0, n0, m0)


def _swa_kernel(q_ref, k_ref, v_ref, z_ref, cos_ref, sin_ref, sink_ref, kc_ref, vc_ref,
                y_ref, ko_ref, vo_ref, *, Lq, valid, has_cache):
    blk = pl.program_id(1)

    @pl.when(blk == 0)
    def _():
        ko_ref[...] = kc_ref[...]
        vo_ref[...] = vc_ref[...]

    cos = cos_ref[...]
    sin = sin_ref[...]
    lane = lax.broadcasted_iota(jnp.int32, (Lq, LANE), 1)
    first_half = (lane % SWA_DH) < (SWA_DH // 2)

    def rope(x):
        partner = jnp.where(first_half, pltpu.roll(x, LANE - SWA_DH // 2, axis=1), pltpu.roll(x, SWA_DH // 2, axis=1))
        return x * cos + partner * sin

    k_cur = rope(k_ref[...])
    v_cur = v_ref[...]
    k_prev = ko_ref[0]
    v_prev = vo_ref[0]
    kk = jnp.concatenate([k_prev, k_cur], axis=0)
    vv = jnp.concatenate([v_prev, v_cur], axis=0)

    rows, nk = SWA_GROUP * Lq, WINDOW + Lq
    t = lax.broadcasted_iota(jnp.int32, (rows, nk), 0) & (Lq - 1)
    c = lax.broadcasted_iota(jnp.int32, (rows, nk), 1)
    prev_ok = jnp.logical_or(has_cache, blk > 0)
    mask = ((c < WINDOW) & (c > t) & prev_ok) | ((c >= WINDOW) & (c - WINDOW <= t) & (c - WINDOW < valid))

    pairs_per_group = SWA_GROUP // 2
    y_chunks = []
    for g in range(SWA_KV_HEADS):
        gsl = slice(g * SWA_DH, (g + 1) * SWA_DH)
        q_pairs = [rope(q_ref[:, (g * pairs_per_group + p) * LANE:(g * pairs_per_group + p + 1) * LANE])
                   * (SWA_DH ** -0.5) for p in range(pairs_per_group)]
        q_st = jnp.concatenate([q_pairs[h // 2][:, (h % 2) * SWA_DH:(h % 2 + 1) * SWA_DH]
                                for h in range(SWA_GROUP)], axis=0)
        sink = jnp.concatenate([jnp.broadcast_to(sink_ref[0:1, g * SWA_GROUP + h:g * SWA_GROUP + h + 1], (Lq, 1))
                                for h in range(SWA_GROUP)], axis=0)
        s = jnp.where(mask, _dot_nt(q_st, kk[:, gsl]), NEG)
        mx = jnp.maximum(jnp.max(s, axis=1, keepdims=True), sink)
        p = jnp.exp(s - mx)
        den = jnp.sum(p, axis=1, keepdims=True) + jnp.exp(sink - mx)
        o = _dot(p, vv[:, gsl]) / den
        for pr in range(pairs_per_group):
            csl = slice((g * pairs_per_group + pr) * LANE, (g * pairs_per_group + pr + 1) * LANE)
            pair = jnp.concatenate([o[(2 * pr) * Lq:(2 * pr + 1) * Lq], o[(2 * pr + 1) * Lq:(2 * pr + 2) * Lq]], axis=1)
            y_chunks.append(pair * _silu(z_ref[:, csl]))
    y_ref[...] = jnp.concatenate(y_chunks, axis=1).astype(y_ref.dtype)

    if valid == WINDOW:
        ko_ref[0] = k_cur
        vo_ref[0] = v_cur
    else:
        ko_ref[0, 0:WINDOW - valid, :] = k_prev[valid:, :]
        ko_ref[0, WINDOW - valid:WINDOW, :] = k_cur[0:valid, :]
        vo_ref[0, 0:WINDOW - valid, :] = v_prev[valid:, :]
        vo_ref[0, WINDOW - valid:WINDOW, :] = v_cur[0:valid, :]


def _swa(proj, cos, sin, sinks, kc, vc, n_seq, t_pad, Lq, valid, has_cache, table_per_block, y_dtype):
    nb = t_pad // Lq
    rowblk = lambda s, b: s * nb + b
    wide = lambda name: pl.BlockSpec((Lq, MIX), lambda s, b, cb=_colblock(name): (rowblk(s, b), cb))
    narrow = lambda name: pl.BlockSpec((Lq, LANE), lambda s, b, cb=_colblock(name): (rowblk(s, b), cb))
    table = pl.BlockSpec((Lq, LANE), (lambda s, b: (b, 0)) if table_per_block else (lambda s, b: (0, 0)))
    cache = pl.BlockSpec((1, WINDOW, LANE), lambda s, b: (s, 0, 0))
    return pl.pallas_call(
        functools.partial(_swa_kernel, Lq=Lq, valid=valid, has_cache=has_cache),
        grid=(n_seq, nb),
        in_specs=[wide("qb"), narrow("kb"), narrow("vb"), wide("zb"), table, table,
                  pl.BlockSpec((1, LANE), lambda s, b: (0, 0)), cache, cache],
        out_specs=[pl.BlockSpec((Lq, MIX), lambda s, b: (rowblk(s, b), 0)), cache, cache],
        out_shape=[jax.ShapeDtypeStruct((n_seq * t_pad, MIX), y_dtype),
                   jax.ShapeDtypeStruct(kc.shape, F32),
                   jax.ShapeDtypeStruct(vc.shape, F32)],
        compiler_params=_params(("arbitrary", "arbitrary"), 32),
        name="swa",
    )(proj, proj, proj, proj, cos, sin, sinks, kc, vc)


_CONV_PAD = SUBLANE


def _ssd_kernel(xbc_ref, z_ref, dt_ref, cw_ref, cb_ref, dtb_ref, alog_ref, dskip_ref, sn_ref, conv0_ref, s0_ref,
                y_ref, convo_ref, so_ref, full_ref, *, L, valid):
    hist = SSD_CONV - 1

    @pl.when(pl.program_id(1) == 0)
    def _():
        so_ref[...] = s0_ref[...]
        full_ref[_CONV_PAD - hist:_CONV_PAD, :] = conv0_ref[0]

    full_ref[_CONV_PAD:_CONV_PAD + L, :] = xbc_ref[...]
    acc = cb_ref[...] + full_ref[_CONV_PAD - hist:_CONV_PAD - hist + L, :] * cw_ref[0:1, :]
    for j in range(1, SSD_CONV):
        acc = acc + full_ref[_CONV_PAD - hist + j:_CONV_PAD - hist + j + L, :] * cw_ref[j:j + 1, :]
    xc = _silu(acc)
    new_hist = full_ref[_CONV_PAD + valid - hist:_CONV_PAD + valid, :]
    convo_ref[0] = new_hist
    full_ref[_CONV_PAD - hist:_CONV_PAD, :] = new_hist

    nbc = SSD_GROUPS * SSD_DSTATE
    row_ok = lax.broadcasted_iota(jnp.int32, (L, LANE), 0) < valid
    dtv = jnp.where(row_ok, jax.nn.softplus(dt_ref[...] + dtb_ref[...]), 0.0)
    a = dtv * (-jnp.exp(alog_ref[...]))
    cum = _cumsum_rows(a)
    cum_t = cum.T
    tr = lax.broadcasted_iota(jnp.int32, (L, L), 0)
    tc = lax.broadcasted_iota(jnp.int32, (L, L), 1)
    tril = tc <= tr

    groups, heads = range(SSD_GROUPS), range(SSD_HEADS)
    gp = SSD_HPG * SSD_DH
    bg = [xc[:, MIX + g * SSD_DSTATE:MIX + (g + 1) * SSD_DSTATE] for g in groups]
    cg = [xc[:, MIX + nbc + g * SSD_DSTATE:MIX + nbc + (g + 1) * SSD_DSTATE] for g in groups]
    st = [so_ref[0, g * SSD_HPG:(g + 1) * SSD_HPG].reshape(gp, SSD_DSTATE) for g in groups]
    cb = [_dot_nt(cg[g], bg[g]) for g in groups]
    cst = [_dot_nt(cg[g], st[g]) for g in groups]
    c_col = [cum[:, h:h + 1] for h in heads]
    c_last = [cum[L - 1:L, h:h + 1] for h in heads]
    lm = [jnp.where(tril, jnp.exp(jnp.where(tril, c_col[h] - cum_t[h:h + 1, :], 0.0)), 0.0) for h in heads]
    xdt = [xc[:, h * SSD_DH:(h + 1) * SSD_DH] * dtv[:, h:h + 1] for h in heads]
    y_in = [_dot(cb[h // SSD_HPG] * lm[h], xdt[h]) for h in heads]
    ys = [y_in[h] + jnp.exp(c_col[h]) * cst[h // SSD_HPG][:, (h % SSD_HPG) * SSD_DH:(h % SSD_HPG + 1) * SSD_DH]
          for h in heads]
    xw = [jnp.concatenate([xdt[h] * jnp.exp(c_last[h] - c_col[h]) for h in range(g * SSD_HPG, (g + 1) * SSD_HPG)],
                          axis=1) for g in groups]
    upd = [_dot_tn(xw[g], bg[g]) for g in groups]
    for h in heads:
        g, hh = divmod(h, SSD_HPG)
        so_ref[0, h] = jnp.exp(c_last[h]) * st[g][hh * SSD_DH:(hh + 1) * SSD_DH] + upd[g][hh * SSD_DH:(hh + 1) * SSD_DH]
    xs = xc[:, :MIX]
    yc = (jnp.concatenate(ys, axis=1) + dskip_ref[...] * xs) * _silu(z_ref[...])
    y_ref[...] = (yc * lax.rsqrt(jnp.mean(yc * yc, axis=1, keepdims=True) + RMS_EPS) * sn_ref[...]).astype(y_ref.dtype)


def _ssd(proj, conv_w, conv_b, dt_bias, a_log, d_skip, snorm, conv0, s0, n_seq, t_pad, L, valid, y_dtype):
    nc = t_pad // L
    rowblk = lambda s, c: s * nc + c
    vec = lambda n: pl.BlockSpec((1, n), lambda s, c: (0, 0))
    conv_st = pl.BlockSpec((1, SSD_CONV - 1, SSD_CONV_DIM), lambda s, c: (s, 0, 0))
    ssd_st = pl.BlockSpec((1, SSD_HEADS, SSD_DH, SSD_DSTATE), lambda s, c: (s, 0, 0, 0))
    return pl.pallas_call(
        functools.partial(_ssd_kernel, L=L, valid=valid),
        grid=(n_seq, nc),
        in_specs=[pl.BlockSpec((L, SSD_CONV_DIM), lambda s, c, cb=_colblock("xbc"): (rowblk(s, c), cb)),
                  pl.BlockSpec((L, MIX), lambda s, c, cb=_colblock("zc"): (rowblk(s, c), cb)),
                  pl.BlockSpec((L, LANE), lambda s, c, cb=_colblock("dt"): (rowblk(s, c), cb)),
                  pl.BlockSpec((SSD_CONV, SSD_CONV_DIM), lambda s, c: (0, 0)),
                  vec(SSD_CONV_DIM), vec(LANE), vec(LANE), vec(MIX), vec(MIX), conv_st, ssd_st],
        out_specs=[pl.BlockSpec((L, MIX), lambda s, c: (rowblk(s, c), 0)), conv_st, ssd_st],
        out_shape=[jax.ShapeDtypeStruct((n_seq * t_pad, MIX), y_dtype),
                   jax.ShapeDtypeStruct(conv0.shape, F32),
                   jax.ShapeDtypeStruct(s0.shape, F32)],
        scratch_shapes=[pltpu.VMEM((_CONV_PAD + L, SSD_CONV_DIM), F32)],
        compiler_params=_params(("arbitrary", "arbitrary"), 32),
        name="ssd",
    )(proj, proj, proj, conv_w, conv_b, dt_bias, a_log, d_skip, snorm, conv0, s0)


_POOL_PAD = 2 * SUBLANE


def _pool_kernel(u_ref, z_ref, lin_ref, scale_ref, p0_ref, y_ref, po_ref, full_ref, *, L, valid, n_hist):
    step = pl.program_id(1)

    @pl.when(step == 0)
    def _():
        full_ref[_POOL_PAD - POOL_HIST:_POOL_PAD, :] = p0_ref[0]

    full_ref[_POOL_PAD:_POOL_PAD + L, :] = u_ref[...]
    pos = n_hist + step * L + lax.broadcasted_iota(jnp.int32, (L, 1), 0)
    for g, w in enumerate(POOL_WINDOWS):
        sl = slice(g * POOL_GC, (g + 1) * POOL_GC)
        cur = full_ref[_POOL_PAD:_POOL_PAD + L, sl]
        tot = cur
        for j in range(1, w):
            tot = tot + full_ref[_POOL_PAD - j:_POOL_PAD - j + L, sl]
        cnt = jnp.minimum(pos + 1, w).astype(F32)
        d = tot / cnt - cur
        y = _dot(d, lin_ref[g]) * scale_ref[:, sl] * _silu(z_ref[:, sl])
        y_ref[:, sl] = y.astype(y_ref.dtype)
    new_hist = full_ref[_POOL_PAD + valid - POOL_HIST:_POOL_PAD + valid, :]
    po_ref[0] = new_hist
    full_ref[_POOL_PAD - POOL_HIST:_POOL_PAD, :] = new_hist


def _pool(proj, lin, scale, p0, n_seq, t_pad, L, valid, n_hist, y_dtype):
    nt = t_pad // L
    rowblk = lambda s, c: s * nt + c
    hist = pl.BlockSpec((1, POOL_HIST, MIX), lambda s, c: (s, 0, 0))
    return pl.pallas_call(
        functools.partial(_pool_kernel, L=L, valid=valid, n_hist=n_hist),
        grid=(n_seq, nt),
        in_specs=[pl.BlockSpec((L, MIX), lambda s, c, cb=_colblock("ud"): (rowblk(s, c), cb)),
                  pl.BlockSpec((L, MIX), lambda s, c, cb=_colblock("zd"): (rowblk(s, c), cb)),
                  pl.BlockSpec((len(POOL_WINDOWS), POOL_GC, POOL_GC), lambda s, c: (0, 0, 0)),
                  pl.BlockSpec((1, MIX), lambda s, c: (0, 0)), hist],
        out_specs=[pl.BlockSpec((L, MIX), lambda s, c: (rowblk(s, c), 0)), hist],
        out_shape=[jax.ShapeDtypeStruct((n_seq * t_pad, MIX), y_dtype),
                   jax.ShapeDtypeStruct(p0.shape, F32)],
        scratch_shapes=[pltpu.VMEM((_POOL_PAD + L, MIX), F32)],
        compiler_params=_params(("arbitrary", "arbitrary"), 32),
        name="pool",
    )(proj, proj, lin, scale, p0)


_REF_SIZES = (MIX, MIX, MIX, MIX, MIX, MLSTM_HEADS, MLSTM_HEADS,
              SWA_HEADS * SWA_DH, SWA_KV_HEADS * SWA_DH, SWA_KV_HEADS * SWA_DH, MIX,
              MIX, SSD_CONV_DIM, SSD_HEADS, MIX, MIX)
_REF_NAMES = ("qa", "ka", "va", "oa", "za", "ia", "fa", "qb", "kb", "vb", "zb", "zc", "xbc", "dt", "ud", "zd")


_REGROUP_ROWS = 256


def _regroup_plan():
    ref_off, o = {}, 0
    for name, size in zip(_REF_NAMES, _REF_SIZES):
        ref_off[name] = o
        o += size
    plan = []
    for name, (off, width) in _COL.items():
        src = ref_off["ia"] if name == "gate" else ref_off[name]
        for b in range(width // LANE):
            plan.append(((src + b * LANE) // LANE, (src + b * LANE) % LANE, off // LANE + b))
    plan.sort()
    assert plan[0][0] == 0 and all(b[0] - a[0] in (0, 1) for a, b in zip(plan, plan[1:]))
    return plan


def _regroup_kernel(cur_tab, adv_tab, shift_tab, dst_tab, w_ref, o_ref, prev_ref, *, shifts):
    t = pl.program_id(1)
    n_iter = prev_ref.shape[0] // _REGROUP_ROWS
    lane = lax.broadcasted_iota(jnp.int32, (_REGROUP_ROWS, LANE), 1)

    def emit(shift):
        def body(it, carry):
            rs = pl.ds(pl.multiple_of(it * _REGROUP_ROWS, _REGROUP_ROWS), _REGROUP_ROWS)
            out = prev_ref[rs, :]
            if shift:
                out = jnp.where(lane < LANE - shift, pltpu.roll(out, LANE - shift, axis=1),
                                pltpu.roll(w_ref[0, rs, :], LANE - shift, axis=1))
            o_ref[0, rs, :] = out.astype(BF16)
            return carry

        lax.fori_loop(0, n_iter, body, 0)

    for shift in shifts:
        pl.when(jnp.logical_and(t > 0, shift_tab[t] == shift))(functools.partial(emit, shift))

    @pl.when(adv_tab[t] == 1)
    def _():
        prev_ref[...] = w_ref[0]


def _regroup_w_in(w_in):
    plan = _regroup_plan()
    n = len(plan)
    last_src = pl.cdiv(w_in.shape[2], LANE) - 1
    cur = [0] + [min(q + 1, last_src) for q, _, _ in plan]
    adv = [1] + [int(t + 1 < n and plan[t + 1][0] == plan[t][0] + 1) for t in range(n)]
    shift = [0] + [r for _, r, _ in plan]
    dst = [plan[0][2]] + [d for _, _, d in plan]
    tabs = [jnp.asarray(v, jnp.int32) for v in (cur, adv, shift, dst)]
    depth, d_in, _ = w_in.shape
    return pl.pallas_call(
        functools.partial(_regroup_kernel, shifts=tuple(sorted({r for _, r, _ in plan}))),
        grid_spec=pltpu.PrefetchScalarGridSpec(
            num_scalar_prefetch=4,
            grid=(depth, n + 1),
            in_specs=[pl.BlockSpec((1, d_in, LANE), lambda l, t, cur, adv, sh, dst: (l, 0, cur[t]))],
            out_specs=pl.BlockSpec((1, d_in, LANE), lambda l, t, cur, adv, sh, dst: (l, 0, dst[t])),
            scratch_shapes=[pltpu.VMEM((d_in, LANE), F32)]),
        out_shape=jax.ShapeDtypeStruct((depth, d_in, NPROJ), BF16),
        compiler_params=_params(("arbitrary", "arbitrary"), 32),
        name="regroup",
    )(*tabs, w_in)


def _pad_lanes(v, n=LANE):
    return jnp.pad(v, (0, n - v.shape[0])).reshape(1, n)


def _rope_tables(pos):
    half = SWA_DH // 2
    inv = ROPE_THETA ** (-jnp.arange(half, dtype=F32) / half)
    ang = pos.astype(F32)[:, None] * inv[None, :]
    cos, sin = jnp.cos(ang), jnp.sin(ang)
    reps = LANE // SWA_DH
    return jnp.tile(jnp.concatenate([cos, cos], axis=1), (1, reps)), jnp.tile(jnp.concatenate([-sin, sin], axis=1), (1, reps))


class _Path:
    def __init__(self, n_seq, t_pad, valid, chunk_a, chunk_b, chunk_c, chunk_d, has_cache, n_hist,
                 tm_in, tn_in, tm_out, tn_out, y_dtype):
        self.n_seq, self.t_pad, self.valid = n_seq, t_pad, valid
        self.chunk_a, self.chunk_b, self.chunk_c, self.chunk_d = chunk_a, chunk_b, chunk_c, chunk_d
        self.has_cache, self.n_hist, self.y_dtype = has_cache, n_hist, y_dtype
        self.tm_in, self.tn_in, self.tm_out, self.tn_out = tm_in, tn_in, tm_out, tn_out


def _layer(x, layer, prm, states, tables, path):
    (g_pre, g_post, w_in, w_out, gate_bias, mnorm, sinks, conv_w, conv_b, dt_bias, a_log, d_skip, snorm,
     pool_lin, pool_scale) = prm
    c0, n0, m0, kc, vc, s0, conv0, p0 = states
    cos, sin = tables
    p = path
    proj = _inproj(x, g_pre, w_in, layer, p.tm_in, p.tn_in)
    ya, c1, n1, m1 = _mlstm(proj, gate_bias, mnorm, c0, n0, m0, p.n_seq, p.t_pad, p.chunk_a,
                            min(p.valid, p.chunk_a), p.y_dtype)
    yb, k1, v1 = _swa(proj, cos, sin, sinks, kc, vc, p.n_seq, p.t_pad, p.chunk_b, min(p.valid, p.chunk_b),
                      p.has_cache, not p.has_cache, p.y_dtype)
    yc, conv1, s1 = _ssd(proj, conv_w, conv_b, dt_bias, a_log, d_skip, snorm, conv0, s0, p.n_seq, p.t_pad,
                         p.chunk_c, min(p.valid, p.chunk_c), p.y_dtype)
    yd, p1 = _pool(proj, pool_lin, pool_scale, p0, p.n_seq, p.t_pad, p.chunk_d, min(p.valid, p.chunk_d),
                   p.n_hist, p.y_dtype)
    x = _outproj((ya, yb, yc, yd), w_out, layer, x, g_post, p.tm_out, p.tn_out)
    return x, (c1, n1, m1, k1, v1, s1, conv1, p1)


def kernel(x_prompt, x_sample, state_mlstm_C, state_mlstm_n, state_mlstm_m, cache_swa_k, cache_swa_v, state_ssd,
           state_ssd_conv, state_pool, norm_pre, norm_post, w_in, w_out, mlstm_b_i, mlstm_b_f, mlstm_norm,
           swa_sinks, ssd_conv_w, ssd_conv_b, ssd_dt_bias, ssd_A_log, ssd_D, ssd_norm, pool_lin, pool_scale):
    bp, seq, _ = x_prompt.shape
    bs, dec_seq, _ = x_sample.shape
    t_s = SUBLANE * pl.cdiv(dec_seq, SUBLANE)

    prompt = _Path(bp, seq, seq, 2 * MLSTM_CHUNK, WINDOW, SSD_CHUNK, 256, False, 0, 512, 1280, 512, 512, BF16)
    sample = _Path(bs, t_s, dec_seq, t_s, t_s, t_s, t_s, True, POOL_HIST, bs * t_s, 1280, bs * t_s, 512, F32)

    w_in_r = _regroup_w_in(w_in)
    w_out_b = w_out.astype(BF16)

    xp = x_prompt.reshape(bp * seq, D_MODEL)
    xs = jnp.pad(x_sample, ((0, 0), (0, t_s - dec_seq), (0, 0))).reshape(bs * t_s, D_MODEL)

    tab_p = _rope_tables(jnp.arange(seq))
    tab_s = _rope_tables(PAST_LEN + jnp.arange(t_s))

    zeros = lambda *shape: jnp.zeros(shape, F32)
    p_states0 = (zeros(bp, MLSTM_HEADS, MLSTM_DH, MLSTM_DH), zeros(bp, MLSTM_HEADS, LANE), zeros(bp, MLSTM_HEADS, LANE),
                 zeros(bp, WINDOW, LANE), zeros(bp, WINDOW, LANE),
                 zeros(bp, SSD_HEADS, SSD_DH, SSD_DSTATE), zeros(bp, SSD_CONV - 1, SSD_CONV_DIM),
                 zeros(bp, POOL_HIST, MIX))

    p_list, s_list = [], []
    for l in range(DEPTH):
        gate_bias = _pad_lanes(jnp.concatenate([mlstm_b_i[l], mlstm_b_f[l]]))
        prm = (norm_pre[l].reshape(1, D_MODEL), norm_post[l].reshape(1, D_MODEL), w_in_r, w_out_b,
               gate_bias, mlstm_norm[l].reshape(1, MIX), _pad_lanes(swa_sinks[l]),
               ssd_conv_w[l], ssd_conv_b[l].reshape(1, SSD_CONV_DIM), _pad_lanes(ssd_dt_bias[l]),
               _pad_lanes(ssd_A_log[l]), jnp.repeat(ssd_D[l], SSD_DH).reshape(1, MIX), ssd_norm[l].reshape(1, MIX),
               pool_lin[l], pool_scale[l].reshape(1, MIX))
        s_states0 = (state_mlstm_C[l], state_mlstm_n[l],
                     jnp.broadcast_to(state_mlstm_m[l][:, :, None], (bs, MLSTM_HEADS, LANE)),
                     cache_swa_k[l].reshape(bs, WINDOW, LANE), cache_swa_v[l].reshape(bs, WINDOW, LANE),
                     state_ssd[l], state_ssd_conv[l], state_pool[l])
        xp, st_p = _layer(xp, l, prm, p_states0, tab_p, prompt)
        xs, st_s = _layer(xs, l, prm, s_states0, tab_s, sample)
        p_list.append(st_p)
        s_list.append(st_s)

    def assemble(lst, b):
        c, n, m, k, v, s, conv, pool = (jnp.stack([st[i] for st in lst], axis=0) for i in range(8))
        kv_shape = (DEPTH, b, WINDOW, SWA_KV_HEADS, SWA_DH)
        return c, n, m[..., 0], k.reshape(kv_shape), v.reshape(kv_shape), s, conv, pool

    y_prompt = xp.reshape(bp, seq, D_MODEL)
    y_sample = xs.reshape(bs, t_s, D_MODEL)[:, :dec_seq]
    return (y_prompt, y_sample) + assemble(p_list, bp) + assemble(s_list, bs)
```

```python
import functools
import math

import jax
import jax.numpy as jnp
from jax import lax
from jax.experimental import pallas as pl
from jax.experimental.pallas import tpu as pltpu

F32 = jnp.float32
BF16 = jnp.bfloat16

D_MODEL = 4096
DEPTH = 4
PAST_LEN = 8192
MIX = D_MODEL // 4
MLSTM_DH = 128
MLSTM_HEADS = MIX // MLSTM_DH
MLSTM_CHUNK = 64
GATE_CAP = 15.0
SWA_DH = 64
SWA_HEADS = MIX // SWA_DH
SWA_KV_HEADS = 2
SWA_GROUP = SWA_HEADS // SWA_KV_HEADS
WINDOW = 128
ROPE_THETA = 10000.0
SSD_DH = 64
SSD_HEADS = MIX // SSD_DH
SSD_GROUPS = 4
SSD_HPG = SSD_HEADS // SSD_GROUPS
SSD_DSTATE = 128
SSD_CONV = 4
SSD_CONV_DIM = MIX + 2 * SSD_GROUPS * SSD_DSTATE
SSD_CHUNK = 128
POOL_WINDOWS = (2, 4, 8, 16)
POOL_GC = MIX // len(POOL_WINDOWS)
POOL_HIST = max(POOL_WINDOWS) - 1
RMS_EPS = 1e-6

LANE = 128
SUBLANE = 8
NEG = -1e30

_COL = {}
_off = 0
for _name, _width in (("qa", MIX), ("ka", MIX), ("va", MIX), ("oa", MIX), ("za", MIX),
                      ("qb", MIX), ("zb", MIX), ("zc", MIX), ("xbc", SSD_CONV_DIM),
                      ("ud", MIX), ("zd", MIX),
                      ("gate", LANE), ("kb", LANE), ("vb", LANE), ("dt", LANE)):
    _COL[_name] = (_off, _width)
    _off += _width
NPROJ = _off


def _colblock(name):
    off, width = _COL[name]
    assert off % width == 0
    return off // width


def _silu(x):
    return x * jax.nn.sigmoid(x)


def _dot(a, b):
    return jnp.dot(a, b, preferred_element_type=F32)


def _dot_nt(a, b):
    return lax.dot_general(a, b, (((1,), (1,)), ((), ())), preferred_element_type=F32)


def _dot_tn(a, b):
    return lax.dot_general(a, b, (((0,), (0,)), ((), ())), preferred_element_type=F32)


def _cumsum_rows(x):
    n = x.shape[0]
    r = lax.broadcasted_iota(jnp.int32, (n, n), 0)
    c = lax.broadcasted_iota(jnp.int32, (n, n), 1)
    tri = (c <= r).astype(F32)
    return jnp.dot(tri, x, preferred_element_type=F32, precision=lax.Precision.HIGHEST)


def _params(sem, vmem_mb):
    return pltpu.CompilerParams(dimension_semantics=sem, vmem_limit_bytes=vmem_mb * 1024 * 1024)


_ANY = pl.BlockSpec(memory_space=pl.ANY)


def _state_spec(stacked, layer):
    tail = stacked.shape[2:]
    return pl.BlockSpec((None, 1) + tail, lambda s, c: (layer, s) + (0,) * len(tail))


_NORM_ROWS = 64


def _inproj_kernel(x_ref, g_ref, w_ref, o_ref, h_ref):
    @pl.when(pl.program_id(1) == 0)
    def _():
        rows = min(_NORM_ROWS, x_ref.shape[0])

        def body(it, carry):
            rs = pl.ds(pl.multiple_of(it * rows, rows), rows)
            x = x_ref[rs, :]
            ms = jnp.mean(x * x, axis=-1, keepdims=True)
            h_ref[rs, :] = (x * lax.rsqrt(ms + RMS_EPS) * g_ref[...]).astype(BF16)
            return carry

        lax.fori_loop(0, x_ref.shape[0] // rows, body, 0)

    o_ref[...] = _dot(h_ref[...], w_ref[...])


def _inproj(x, g, w_all, layer, tm, tn):
    m = x.shape[0]
    return pl.pallas_call(
        _inproj_kernel,
        grid=(m // tm, NPROJ // tn),
        in_specs=[pl.BlockSpec((tm, D_MODEL), lambda i, j: (i, 0)),
                  pl.BlockSpec((1, D_MODEL), lambda i, j: (0, 0)),
                  pl.BlockSpec((None, D_MODEL, tn), lambda i, j: (layer, 0, j))],
        out_specs=pl.BlockSpec((tm, tn), lambda i, j: (i, j)),
        out_shape=jax.ShapeDtypeStruct((m, NPROJ), F32),
        scratch_shapes=[pltpu.VMEM((tm, D_MODEL), BF16)],
        compiler_params=_params(("arbitrary", "arbitrary"), 56),
        name="inproj",
    )(x, g, w_all)


def _outproj_kernel(ya_ref, yb_ref, yc_ref, yd_ref, w_ref, x_ref, g_ref, o_ref, acc_ref, ss_ref, *, tn, n_row_tiles):
    i = pl.program_id(0)
    j = pl.program_id(1)
    slot = i % 2
    col = pl.ds(pl.multiple_of(j * tn, tn), tn)

    @pl.when(i < n_row_tiles)
    def _():
        out = None
        for part, r in enumerate((ya_ref, yb_ref, yc_ref, yd_ref)):
            term = _dot(r[...].astype(BF16), w_ref[part * MIX:(part + 1) * MIX, :])
            out = term if out is None else out + term
        acc_ref[slot, :, col] = out
        sq = jnp.sum(out * out, axis=1, keepdims=True)
        ss_ref[slot] = jnp.where(j == 0, sq, ss_ref[slot] + sq)

    @pl.when(i > 0)
    def _():
        rs = lax.rsqrt(ss_ref[1 - slot] * (1.0 / D_MODEL) + RMS_EPS)
        o_ref[...] = x_ref[...] + acc_ref[1 - slot, :, col] * rs * g_ref[...]


def _outproj(ys, w_all, layer, x, g, tm, tn):
    m = x.shape[0]
    n_row_tiles = m // tm
    yspec = pl.BlockSpec((tm, MIX), lambda i, j: (jnp.minimum(i, n_row_tiles - 1), 0))
    lagged = pl.BlockSpec((tm, tn), lambda i, j: (jnp.maximum(i - 1, 0), jnp.where(i > 0, j, 0)))
    return pl.pallas_call(
        functools.partial(_outproj_kernel, tn=tn, n_row_tiles=n_row_tiles),
        grid=(n_row_tiles + 1, D_MODEL // tn),
        in_specs=[yspec, yspec, yspec, yspec,
                  pl.BlockSpec((None, D_MODEL, tn), lambda i, j: (layer, 0, j)),
                  lagged,
                  pl.BlockSpec((1, tn), lambda i, j: (0, j))],
        out_specs=lagged,
        out_shape=jax.ShapeDtypeStruct((m, D_MODEL), F32),
        scratch_shapes=[pltpu.VMEM((2, tm, D_MODEL), F32), pltpu.VMEM((2, tm, 1), F32)],
        compiler_params=_params(("arbitrary", "arbitrary"), 56),
        name="outproj",
    )(*ys, w_all, x, g)


def _mlstm_kernel(q_ref, k_ref, v_ref, o_ref, z_ref, gate_ref, bias_ref, mn_ref, c0_ref, n0_ref, m0_ref,
                  c_alias, n_alias, m_alias, y_ref, c_ref, n_ref, m_ref, *, L, valid):
    del c_alias, n_alias, m_alias
    @pl.when(pl.program_id(1) == 0)
    def _():
        c_ref[...] = c0_ref[...]
        n_ref[...] = n0_ref[...]
        m_ref[...] = m0_ref[...]

    row = lax.broadcasted_iota(jnp.int32, (L, LANE), 0)
    row_ok = row < valid
    gates = GATE_CAP * jnp.tanh((gate_ref[...] + bias_ref[...]) / GATE_CAP)
    logf = jnp.where(row_ok, jax.nn.log_sigmoid(gates), 0.0)
    b_all = _cumsum_rows(pltpu.roll(logf, LANE - MLSTM_HEADS, axis=1))
    r_all = jnp.where(row_ok, gates - b_all, NEG)
    r_all_t = r_all.T

    tr = lax.broadcasted_iota(jnp.int32, (L, L), 0)
    tc = lax.broadcasted_iota(jnp.int32, (L, L), 1)
    tril = tc <= tr

    heads = range(MLSTM_HEADS)
    sl = [slice(h * MLSTM_DH, (h + 1) * MLSTM_DH) for h in heads]
    q = [q_ref[:, sl[h]] * (MLSTM_DH ** -0.5) for h in heads]
    k = [k_ref[:, sl[h]] for h in heads]
    v = [v_ref[:, sl[h]] for h in heads]
    c_prev = [c_ref[0, h] for h in heads]
    n_prev = [n_ref[0, h:h + 1, :] for h in heads]
    m_prev = [m_ref[0, h:h + 1, 0:1] for h in heads]
    qk = [_dot_nt(q[h], k[h]) for h in heads]
    qc = [_dot(q[h], c_prev[h]) for h in heads]
    qn = [jnp.sum(q[h] * n_prev[h], axis=1, keepdims=True) for h in heads]

    b_col = [b_all[:, h:h + 1] for h in heads]
    dmat = [jnp.where(tril, b_col[h] + r_all_t[h:h + 1, :], NEG) for h in heads]
    inter = [b_col[h] + m_prev[h] for h in heads]
    mt = [jnp.maximum(inter[h], jnp.max(dmat[h], axis=1, keepdims=True)) for h in heads]
    s = [jnp.exp(dmat[h] - mt[h]) * qk[h] for h in heads]
    sv = [_dot(s[h], v[h]) for h in heads]
    w_prev_t = [jnp.exp(inter[h] - mt[h]) for h in heads]
    den = [w_prev_t[h] * qn[h] + jnp.sum(s[h], axis=1, keepdims=True) for h in heads]
    hh = [(w_prev_t[h] * qc[h] + sv[h]) / jnp.maximum(jnp.abs(den[h]), jnp.exp(-mt[h])) for h in heads]

    b_last = [b_all[L - 1:L, h:h + 1] for h in heads]
    dec = [b_last[h] + r_all[:, h:h + 1] for h in heads]
    m_new = [jnp.maximum(b_last[h] + m_prev[h], jnp.max(dec[h], axis=0, keepdims=True)) for h in heads]
    w_prev = [jnp.exp(b_last[h] + m_prev[h] - m_new[h]) for h in heads]
    kw = [k[h] * jnp.exp(dec[h] - m_new[h]) for h in heads]
    c_new = [w_prev[h] * c_prev[h] + _dot_tn(kw[h], v[h]) for h in heads]
    n_new = [w_prev[h] * n_prev[h] + jnp.sum(kw[h], axis=0, keepdims=True) for h in heads]

    hn = [hh[h] * lax.rsqrt(jnp.mean(hh[h] * hh[h], axis=1, keepdims=True) + RMS_EPS) * mn_ref[:, sl[h]] for h in heads]
    ys = [hn[h] * jax.nn.sigmoid(o_ref[:, sl[h]]) * _silu(z_ref[:, sl[h]]) for h in heads]

    y_ref[...] = jnp.concatenate(ys, axis=1).astype(y_ref.dtype)
    for h in heads:
        c_ref[0, h] = c_new[h]
    n_ref[0] = jnp.concatenate(n_new, axis=0)
    m_ref[0] = jnp.concatenate([jnp.broadcast_to(m_new[h], (1, LANE)) for h in heads], axis=0)


def _mlstm(proj, bias, mnorm, st0, l0, acc, layer, n_seq, t_pad, L, valid, y_dtype):
    nc = t_pad // L
    wide = lambda name: pl.BlockSpec((L, MIX), lambda s, c, cb=_colblock(name): (s * nc + c, cb))
    vec = lambda n: pl.BlockSpec((1, n), lambda s, c: (0, 0))
    n_in = 8 + len(st0)
    return pl.pallas_call(
        functools.partial(_mlstm_kernel, L=L, valid=valid),
        grid=(n_seq, nc),
        in_specs=[wide("qa"), wide("ka"), wide("va"), wide("oa"), wide("za"),
                  pl.BlockSpec((L, LANE), lambda s, c, cb=_colblock("gate"): (s * nc + c, cb)),
                  vec(LANE), vec(MIX)] + [_state_spec(a, l0) for a in st0] + [_ANY] * len(acc),
        out_specs=[pl.BlockSpec((L, MIX), lambda s, c: (s * nc + c, 0))] + [_state_spec(a, layer) for a in acc],
        out_shape=[jax.ShapeDtypeStruct((n_seq * t_pad, MIX), y_dtype)]
        + [jax.ShapeDtypeStruct(a.shape, F32) for a in acc],
        input_output_aliases={n_in + i: 1 + i for i in range(len(acc))},
        compiler_params=_params(("arbitrary", "arbitrary"), 32),
        name="mlstm",
    )(proj, proj, proj, proj, proj, proj, bias, mnorm, *st0, *acc)


def _swa_kernel(q_ref, k_ref, v_ref, z_ref, cos_ref, sin_ref, sink_ref, kc_ref, vc_ref, k_alias, v_alias,
                y_ref, ko_ref, vo_ref, *, Lq, valid, has_cache):
    del k_alias, v_alias
    blk = pl.program_id(1)

    @pl.when(blk == 0)
    def _():
        ko_ref[...] = kc_ref[...]
        vo_ref[...] = vc_ref[...]

    cos = cos_ref[...]
    sin = sin_ref[...]
    lane = lax.broadcasted_iota(jnp.int32, (Lq, LANE), 1)
    first_half = (lane % SWA_DH) < (SWA_DH // 2)

    def rope(x):
        partner = jnp.where(first_half, pltpu.roll(x, LANE - SWA_DH // 2, axis=1), pltpu.roll(x, SWA_DH // 2, axis=1))
        return x * cos + partner * sin

    k_cur = rope(k_ref[...])
    v_cur = v_ref[...]
    k_prev = ko_ref[0]
    v_prev = vo_ref[0]
    kk = jnp.concatenate([k_prev, k_cur], axis=0)
    vv = jnp.concatenate([v_prev, v_cur], axis=0)

    rows, nk = SWA_GROUP * Lq, WINDOW + Lq
    t = lax.broadcasted_iota(jnp.int32, (rows, nk), 0) & (Lq - 1)
    c = lax.broadcasted_iota(jnp.int32, (rows, nk), 1)
    prev_ok = jnp.logical_or(has_cache, blk > 0)
    mask = ((c < WINDOW) & (c > t) & prev_ok) | ((c >= WINDOW) & (c - WINDOW <= t) & (c - WINDOW < valid))

    pairs_per_group = SWA_GROUP // 2
    y_chunks = []
    for g in range(SWA_KV_HEADS):
        gsl = slice(g * SWA_DH, (g + 1) * SWA_DH)
        q_pairs = [rope(q_ref[:, (g * pairs_per_group + p) * LANE:(g * pairs_per_group + p + 1) * LANE])
                   * (SWA_DH ** -0.5) for p in range(pairs_per_group)]
        q_st = jnp.concatenate([q_pairs[h // 2][:, (h % 2) * SWA_DH:(h % 2 + 1) * SWA_DH]
                                for h in range(SWA_GROUP)], axis=0)
        sink = jnp.concatenate([jnp.broadcast_to(sink_ref[0:1, g * SWA_GROUP + h:g * SWA_GROUP + h + 1], (Lq, 1))
                                for h in range(SWA_GROUP)], axis=0)
        s = jnp.where(mask, _dot_nt(q_st, kk[:, gsl]), NEG)
        mx = jnp.maximum(jnp.max(s, axis=1, keepdims=True), sink)
        p = jnp.exp(s - mx)
        den = jnp.sum(p, axis=1, keepdims=True) + jnp.exp(sink - mx)
        o = _dot(p, vv[:, gsl]) / den
        for pr in range(pairs_per_group):
            csl = slice((g * pairs_per_group + pr) * LANE, (g * pairs_per_group + pr + 1) * LANE)
            pair = jnp.concatenate([o[(2 * pr) * Lq:(2 * pr + 1) * Lq], o[(2 * pr + 1) * Lq:(2 * pr + 2) * Lq]], axis=1)
            y_chunks.append(pair * _silu(z_ref[:, csl]))
    y_ref[...] = jnp.concatenate(y_chunks, axis=1).astype(y_ref.dtype)

    if valid == WINDOW:
        ko_ref[0] = k_cur
        vo_ref[0] = v_cur
    else:
        ko_ref[0, 0:WINDOW - valid, :] = k_prev[valid:, :]
        ko_ref[0, WINDOW - valid:WINDOW, :] = k_cur[0:valid, :]
        vo_ref[0, 0:WINDOW - valid, :] = v_prev[valid:, :]
        vo_ref[0, WINDOW - valid:WINDOW, :] = v_cur[0:valid, :]


def _swa(proj, cos, sin, sinks, st0, l0, acc, layer, n_seq, t_pad, Lq, valid, has_cache, table_per_block, y_dtype):
    nb = t_pad // Lq
    rowblk = lambda s, b: s * nb + b
    wide = lambda name: pl.BlockSpec((Lq, MIX), lambda s, b, cb=_colblock(name): (rowblk(s, b), cb))
    narrow = lambda name: pl.BlockSpec((Lq, LANE), lambda s, b, cb=_colblock(name): (rowblk(s, b), cb))
    table = pl.BlockSpec((Lq, LANE), (lambda s, b: (b, 0)) if table_per_block else (lambda s, b: (0, 0)))
    n_in = 7 + len(st0)
    return pl.pallas_call(
        functools.partial(_swa_kernel, Lq=Lq, valid=valid, has_cache=has_cache),
        grid=(n_seq, nb),
        in_specs=[wide("qb"), narrow("kb"), narrow("vb"), wide("zb"), table, table,
                  pl.BlockSpec((1, LANE), lambda s, b: (0, 0))] + [_state_spec(a, l0) for a in st0] + [_ANY] * len(acc),
        out_specs=[pl.BlockSpec((Lq, MIX), lambda s, b: (rowblk(s, b), 0))] + [_state_spec(a, layer) for a in acc],
        out_shape=[jax.ShapeDtypeStruct((n_seq * t_pad, MIX), y_dtype)]
        + [jax.ShapeDtypeStruct(a.shape, F32) for a in acc],
        input_output_aliases={n_in + i: 1 + i for i in range(len(acc))},
        compiler_params=_params(("arbitrary", "arbitrary"), 32),
        name="swa",
    )(proj, proj, proj, proj, cos, sin, sinks, *st0, *acc)


_CONV_PAD = SUBLANE


def _spread_heads(x, e):
    lane = lax.broadcasted_iota(jnp.int32, x.shape, 1)
    x = jnp.where(lane < SSD_HEADS, x, 0.0)
    hi = x.astype(BF16)
    rest = x - hi.astype(F32)
    mid = rest.astype(BF16)
    lo = (rest - mid.astype(F32)).astype(BF16)
    return _dot(hi, e) + _dot(mid, e) + _dot(lo, e)


def _ssd_kernel(xbc_ref, z_ref, dt_ref, cw_ref, cb_ref, dtb_ref, alog_ref, dskip_ref, sn_ref, e_ref, conv0_ref, s0_ref,
                conv_alias, s_alias, y_ref, convo_ref, so_ref, full_ref, *, L, valid):
    del conv_alias, s_alias
    hist = SSD_CONV - 1

    @pl.when(pl.program_id(1) == 0)
    def _():
        so_ref[...] = s0_ref[...]
        full_ref[_CONV_PAD - hist:_CONV_PAD, :] = conv0_ref[0]

    full_ref[_CONV_PAD:_CONV_PAD + L, :] = xbc_ref[...]
    full = full_ref[...]
    acc = cb_ref[...] + full[_CONV_PAD:, :] * cw_ref[hist:hist + 1, :]
    for j in range(hist):
        acc = acc + pltpu.roll(full, hist - j, axis=0)[_CONV_PAD:, :] * cw_ref[j:j + 1, :]
    xc = _silu(acc)
    new_hist = full_ref[_CONV_PAD + valid - hist:_CONV_PAD + valid, :]
    convo_ref[0] = new_hist
    full_ref[_CONV_PAD - hist:_CONV_PAD, :] = new_hist

    nbc = SSD_GROUPS * SSD_DSTATE
    row_ok = lax.broadcasted_iota(jnp.int32, (L, LANE), 0) < valid
    dtv = jnp.where(row_ok, jax.nn.softplus(dt_ref[...] + dtb_ref[...]), 0.0)
    a = dtv * (-jnp.exp(alog_ref[...]))
    cum = _cumsum_rows(a)
    cum_t = cum.T
    tr = lax.broadcasted_iota(jnp.int32, (L, L), 0)
    tc = lax.broadcasted_iota(jnp.int32, (L, L), 1)
    tril = tc <= tr

    groups, heads = range(SSD_GROUPS), range(SSD_HEADS)
    gp = SSD_HPG * SSD_DH
    bg = [xc[:, MIX + g * SSD_DSTATE:MIX + (g + 1) * SSD_DSTATE] for g in groups]
    cg = [xc[:, MIX + nbc + g * SSD_DSTATE:MIX + nbc + (g + 1) * SSD_DSTATE] for g in groups]
    st = [so_ref[0, g * SSD_HPG:(g + 1) * SSD_HPG].reshape(gp, SSD_DSTATE) for g in groups]
    cb = [_dot_nt(cg[g], bg[g]) for g in groups]
    cst = [_dot_nt(cg[g], st[g]) for g in groups]
    c_col = [cum[:, h:h + 1] for h in heads]
    c_last = [cum[L - 1:L, h:h + 1] for h in heads]
    lm = [jnp.where(tril, jnp.exp(jnp.where(tril, c_col[h] - cum_t[h:h + 1, :], 0.0)), 0.0) for h in heads]

    spread = _spread_heads(jnp.concatenate([dtv, jnp.exp(cum), jnp.exp(cum[L - 1:L, :] - cum)], axis=0), e_ref[...])
    xs = xc[:, :MIX]
    xdt = xs * spread[0:L]
    xw = xdt * spread[2 * L:3 * L]
    low_half = lax.broadcasted_iota(jnp.int32, (L, LANE), 1) < SSD_DH
    y_in = []
    for pair in range(SSD_HEADS // 2):
        g = (2 * pair) // SSD_HPG
        x_pair = xdt[:, pair * LANE:(pair + 1) * LANE]
        y_in.append(jnp.where(low_half, _dot(cb[g] * lm[2 * pair], x_pair), _dot(cb[g] * lm[2 * pair + 1], x_pair)))
    y_state = jnp.concatenate(cst, axis=1) * spread[L:2 * L]
    upd = [_dot_tn(xw[:, g * gp:(g + 1) * gp], bg[g]) for g in groups]
    for h in heads:
        g, hh = divmod(h, SSD_HPG)
        so_ref[0, h] = jnp.exp(c_last[h]) * st[g][hh * SSD_DH:(hh + 1) * SSD_DH] + upd[g][hh * SSD_DH:(hh + 1) * SSD_DH]
    yc = (jnp.concatenate(y_in, axis=1) + y_state + dskip_ref[...] * xs) * _silu(z_ref[...])
    y_ref[...] = (yc * lax.rsqrt(jnp.mean(yc * yc, axis=1, keepdims=True) + RMS_EPS) * sn_ref[...]).astype(y_ref.dtype)


def _ssd(proj, conv_w, conv_b, dt_bias, a_log, d_skip, snorm, st0, l0, acc, layer, n_seq, t_pad, L, valid, y_dtype):
    nc = t_pad // L
    rowblk = lambda s, c: s * nc + c
    vec = lambda n: pl.BlockSpec((1, n), lambda s, c: (0, 0))
    n_in = 10 + len(st0)
    spread = (jnp.arange(MIX)[None, :] // SSD_DH == jnp.arange(LANE)[:, None]).astype(BF16)
    return pl.pallas_call(
        functools.partial(_ssd_kernel, L=L, valid=valid),
        grid=(n_seq, nc),
        in_specs=[pl.BlockSpec((L, SSD_CONV_DIM), lambda s, c, cb=_colblock("xbc"): (rowblk(s, c), cb)),
                  pl.BlockSpec((L, MIX), lambda s, c, cb=_colblock("zc"): (rowblk(s, c), cb)),
                  pl.BlockSpec((L, LANE), lambda s, c, cb=_colblock("dt"): (rowblk(s, c), cb)),
                  pl.BlockSpec((SSD_CONV, SSD_CONV_DIM), lambda s, c: (0, 0)),
                  vec(SSD_CONV_DIM), vec(LANE), vec(LANE), vec(MIX), vec(MIX),
                  pl.BlockSpec((LANE, MIX), lambda s, c: (0, 0))]
        + [_state_spec(a, l0) for a in st0] + [_ANY] * len(acc),
        out_specs=[pl.BlockSpec((L, MIX), lambda s, c: (rowblk(s, c), 0))] + [_state_spec(a, layer) for a in acc],
        out_shape=[jax.ShapeDtypeStruct((n_seq * t_pad, MIX), y_dtype)]
        + [jax.ShapeDtypeStruct(a.shape, F32) for a in acc],
        input_output_aliases={n_in + i: 1 + i for i in range(len(acc))},
        scratch_shapes=[pltpu.VMEM((_CONV_PAD + L, SSD_CONV_DIM), F32)],
        compiler_params=_params(("arbitrary", "arbitrary"), 32),
        name="ssd",
    )(proj, proj, proj, conv_w, conv_b, dt_bias, a_log, d_skip, snorm, spread, *st0, *acc)


_POOL_PAD = 2 * SUBLANE


def _pool_kernel(u_ref, z_ref, lin_ref, scale_ref, p0_ref, p_alias, y_ref, po_ref, full_ref, *, L, valid, n_hist):
    del p_alias
    step = pl.program_id(1)

    @pl.when(step == 0)
    def _():
        full_ref[_POOL_PAD - POOL_HIST:_POOL_PAD, :] = p0_ref[0]

    full_ref[_POOL_PAD:_POOL_PAD + L, :] = u_ref[...]
    pos = n_hist + step * L + lax.broadcasted_iota(jnp.int32, (L, 1), 0)
    for g, w in enumerate(POOL_WINDOWS):
        sl = slice(g * POOL_GC, (g + 1) * POOL_GC)
        cur = full_ref[_POOL_PAD:_POOL_PAD + L, sl]
        tot = cur
        for j in range(1, w):
            tot = tot + full_ref[_POOL_PAD - j:_POOL_PAD - j + L, sl]
        cnt = jnp.minimum(pos + 1, w).astype(F32)
        d = tot / cnt - cur
        y = _dot(d, lin_ref[g]) * scale_ref[:, sl] * _silu(z_ref[:, sl])
        y_ref[:, sl] = y.astype(y_ref.dtype)
    new_hist = full_ref[_POOL_PAD + valid - POOL_HIST:_POOL_PAD + valid, :]
    po_ref[0] = new_hist
    full_ref[_POOL_PAD - POOL_HIST:_POOL_PAD, :] = new_hist


def _pool(proj, lin, scale, st0, l0, acc, layer, n_seq, t_pad, L, valid, n_hist, y_dtype):
    nt = t_pad // L
    rowblk = lambda s, c: s * nt + c
    n_in = 4 + len(st0)
    return pl.pallas_call(
        functools.partial(_pool_kernel, L=L, valid=valid, n_hist=n_hist),
        grid=(n_seq, nt),
        in_specs=[pl.BlockSpec((L, MIX), lambda s, c, cb=_colblock("ud"): (rowblk(s, c), cb)),
                  pl.BlockSpec((L, MIX), lambda s, c, cb=_colblock("zd"): (rowblk(s, c), cb)),
                  pl.BlockSpec((len(POOL_WINDOWS), POOL_GC, POOL_GC), lambda s, c: (0, 0, 0)),
                  pl.BlockSpec((1, MIX), lambda s, c: (0, 0))] + [_state_spec(a, l0) for a in st0] + [_ANY] * len(acc),
        out_specs=[pl.BlockSpec((L, MIX), lambda s, c: (rowblk(s, c), 0))] + [_state_spec(a, layer) for a in acc],
        out_shape=[jax.ShapeDtypeStruct((n_seq * t_pad, MIX), y_dtype)]
        + [jax.ShapeDtypeStruct(a.shape, F32) for a in acc],
        input_output_aliases={n_in + i: 1 + i for i in range(len(acc))},
        scratch_shapes=[pltpu.VMEM((_POOL_PAD + L, MIX), F32)],
        compiler_params=_params(("arbitrary", "arbitrary"), 32),
        name="pool",
    )(proj, proj, lin, scale, *st0, *acc)


_REF_SIZES = (MIX, MIX, MIX, MIX, MIX, MLSTM_HEADS, MLSTM_HEADS,
              SWA_HEADS * SWA_DH, SWA_KV_HEADS * SWA_DH, SWA_KV_HEADS * SWA_DH, MIX,
              MIX, SSD_CONV_DIM, SSD_HEADS, MIX, MIX)
_REF_NAMES = ("qa", "ka", "va", "oa", "za", "ia", "fa", "qb", "kb", "vb", "zb", "zc", "xbc", "dt", "ud", "zd")


def _regroup_plan():
    ref_off, o = {}, 0
    for name, size in zip(_REF_NAMES, _REF_SIZES):
        ref_off[name] = o
        o += size
    src = []
    for name, (off, width) in _COL.items():
        start = ref_off["ia"] if name == "gate" else ref_off[name]
        src += [start + b * LANE for b in range(width // LANE)]
    assert len(src) == NPROJ // LANE and all(s % SUBLANE == 0 and s + LANE <= o for s in src)
    return src


def _regroup_kernel(src_tab, wt_ref, o_ref):
    del src_tab
    o_ref[...] = wt_ref[...].T.astype(BF16)


def _regroup_w_in(w_in):
    depth, d_in, _ = w_in.shape
    w_t = jnp.transpose(w_in, (0, 2, 1))
    src = jnp.asarray([s // SUBLANE for s in _regroup_plan()], jnp.int32)
    return pl.pallas_call(
        _regroup_kernel,
        grid_spec=pltpu.PrefetchScalarGridSpec(
            num_scalar_prefetch=1,
            grid=(depth, NPROJ // LANE),
            in_specs=[pl.BlockSpec((None, pl.Element(LANE), pl.Element(d_in)), lambda l, t, src: (l, src[t] * SUBLANE, 0))],
            out_specs=pl.BlockSpec((None, d_in, LANE), lambda l, t, src: (l, 0, t))),
        out_shape=jax.ShapeDtypeStruct((depth, d_in, NPROJ), BF16),
        compiler_params=_params(("arbitrary", "arbitrary"), 32),
        name="regroup",
    )(src, w_t)


def _pad_lanes(v, n=LANE):
    return jnp.pad(v, (0, n - v.shape[0])).reshape(1, n)


def _rope_tables(pos):
    half = SWA_DH // 2
    inv = ROPE_THETA ** (-jnp.arange(half, dtype=F32) / half)
    ang = pos.astype(F32)[:, None] * inv[None, :]
    cos, sin = jnp.cos(ang), jnp.sin(ang)
    reps = LANE // SWA_DH
    return jnp.tile(jnp.concatenate([cos, cos], axis=1), (1, reps)), jnp.tile(jnp.concatenate([-sin, sin], axis=1), (1, reps))


class _Path:
    def __init__(self, n_seq, t_pad, valid, chunk_a, chunk_b, chunk_c, chunk_d, has_cache, n_hist,
                 tm_in, tn_in, tm_out, tn_out, y_dtype):
        self.n_seq, self.t_pad, self.valid = n_seq, t_pad, valid
        self.chunk_a, self.chunk_b, self.chunk_c, self.chunk_d = chunk_a, chunk_b, chunk_c, chunk_d
        self.has_cache, self.n_hist, self.y_dtype = has_cache, n_hist, y_dtype
        self.tm_in, self.tn_in, self.tm_out, self.tn_out = tm_in, tn_in, tm_out, tn_out


def _layer(x, layer, prm, states, l0, acc, tables, path):
    (g_pre, g_post, w_in, w_out, gate_bias, mnorm, sinks, conv_w, conv_b, dt_bias, a_log, d_skip, snorm,
     pool_lin, pool_scale) = prm
    c0, n0, m0, kc, vc, s0, conv0, p0 = states
    ca, na, ma, ka, va, sa, conva, pa = acc
    cos, sin = tables
    p = path
    proj = _inproj(x, g_pre, w_in, layer, p.tm_in, p.tn_in)
    ya, ca, na, ma = _mlstm(proj, gate_bias, mnorm, (c0, n0, m0), l0, (ca, na, ma), layer, p.n_seq, p.t_pad,
                            p.chunk_a, min(p.valid, p.chunk_a), p.y_dtype)
    yb, ka, va = _swa(proj, cos, sin, sinks, (kc, vc), l0, (ka, va), layer, p.n_seq, p.t_pad, p.chunk_b,
                      min(p.valid, p.chunk_b), p.has_cache, not p.has_cache, p.y_dtype)
    yc, conva, sa = _ssd(proj, conv_w, conv_b, dt_bias, a_log, d_skip, snorm, (conv0, s0), l0, (conva, sa), layer,
                         p.n_seq, p.t_pad, p.chunk_c, min(p.valid, p.chunk_c), p.y_dtype)
    yd, pa = _pool(proj, pool_lin, pool_scale, (p0,), l0, (pa,), layer, p.n_seq, p.t_pad, p.chunk_d,
                   min(p.valid, p.chunk_d), p.n_hist, p.y_dtype)
    x = _outproj((ya, yb, yc, yd), w_out, layer, x, g_post, p.tm_out, p.tn_out)
    return x, (ca, na, ma, ka, va, sa, conva, pa)


def kernel(x_prompt, x_sample, state_mlstm_C, state_mlstm_n, state_mlstm_m, cache_swa_k, cache_swa_v, state_ssd,
           state_ssd_conv, state_pool, norm_pre, norm_post, w_in, w_out, mlstm_b_i, mlstm_b_f, mlstm_norm,
           swa_sinks, ssd_conv_w, ssd_conv_b, ssd_dt_bias, ssd_A_log, ssd_D, ssd_norm, pool_lin, pool_scale):
    bp, seq, _ = x_prompt.shape
    bs, dec_seq, _ = x_sample.shape
    t_s = SUBLANE * pl.cdiv(dec_seq, SUBLANE)

    prompt = _Path(bp, seq, seq, 2 * MLSTM_CHUNK, WINDOW, SSD_CHUNK, 256, False, 0, 512, 1280, 512, 1024, BF16)
    sample = _Path(bs, t_s, dec_seq, t_s, t_s, t_s, t_s, True, POOL_HIST, bs * t_s, 1280, bs * t_s, 512, F32)

    w_in_r = _regroup_w_in(w_in)
    w_out_b = w_out.astype(BF16)

    xp = x_prompt.reshape(bp * seq, D_MODEL)
    xs = jnp.pad(x_sample, ((0, 0), (0, t_s - dec_seq), (0, 0))).reshape(bs * t_s, D_MODEL)

    tab_p = _rope_tables(jnp.arange(seq))
    tab_s = _rope_tables(PAST_LEN + jnp.arange(t_s))

    def state_shapes(layers, b):
        return ((layers, b, MLSTM_HEADS, MLSTM_DH, MLSTM_DH), (layers, b, MLSTM_HEADS, LANE), (layers, b, MLSTM_HEADS, LANE),
                (layers, b, WINDOW, LANE), (layers, b, WINDOW, LANE),
                (layers, b, SSD_HEADS, SSD_DH, SSD_DSTATE), (layers, b, SSD_CONV - 1, SSD_CONV_DIM),
                (layers, b, POOL_HIST, MIX))

    p_states0 = tuple(jnp.zeros(s, F32) for s in state_shapes(1, bp))
    s_states0 = (state_mlstm_C, state_mlstm_n,
                 jnp.broadcast_to(state_mlstm_m[..., None], (DEPTH, bs, MLSTM_HEADS, LANE)),
                 cache_swa_k.reshape(DEPTH, bs, WINDOW, LANE), cache_swa_v.reshape(DEPTH, bs, WINDOW, LANE),
                 state_ssd, state_ssd_conv, state_pool)
    p_acc = tuple(jnp.zeros(s, F32) for s in state_shapes(DEPTH, bp))
    s_acc = tuple(jnp.zeros(s, F32) for s in state_shapes(DEPTH, bs))

    for l in range(DEPTH):
        gate_bias = _pad_lanes(jnp.concatenate([mlstm_b_i[l], mlstm_b_f[l]]))
        prm = (norm_pre[l].reshape(1, D_MODEL), norm_post[l].reshape(1, D_MODEL), w_in_r, w_out_b,
               gate_bias, mlstm_norm[l].reshape(1, MIX), _pad_lanes(swa_sinks[l]),
               ssd_conv_w[l], ssd_conv_b[l].reshape(1, SSD_CONV_DIM), _pad_lanes(ssd_dt_bias[l]),
               _pad_lanes(ssd_A_log[l]), jnp.repeat(ssd_D[l], SSD_DH).reshape(1, MIX), ssd_norm[l].reshape(1, MIX),
               pool_lin[l], pool_scale[l].reshape(1, MIX))
        xp, p_acc = _layer(xp, l, prm, p_states0, 0, p_acc, tab_p, prompt)
        xs, s_acc = _layer(xs, l, prm, s_states0, l, s_acc, tab_s, sample)

    def finish(acc, b):
        c, n, m, k, v, s, conv, pool = acc
        kv_shape = (DEPTH, b, WINDOW, SWA_KV_HEADS, SWA_DH)
        return c, n, m[..., 0], k.reshape(kv_shape), v.reshape(kv_shape), s, conv, pool

    y_prompt = xp.reshape(bp, seq, D_MODEL)
    y_sample = xs.reshape(bs, t_s, D_MODEL)[:, :dec_seq]
    return (y_prompt, y_sample) + finish(p_acc, bp) + finish(s_acc, bs)
```

```python
import functools
import math

import jax
import jax.numpy as jnp
from jax import lax
from jax.experimental import pallas as pl
from jax.experimental.pallas import tpu as pltpu

F32 = jnp.float32
BF16 = jnp.bfloat16

D_MODEL = 4096
DEPTH = 4
PAST_LEN = 8192
MIX = D_MODEL // 4
MLSTM_DH = 128
MLSTM_HEADS = MIX // MLSTM_DH
MLSTM_CHUNK = 64
GATE_CAP = 15.0
SWA_DH = 64
SWA_HEADS = MIX // SWA_DH
SWA_KV_HEADS = 2
SWA_GROUP = SWA_HEADS // SWA_KV_HEADS
WINDOW = 128
ROPE_THETA = 10000.0
SSD_DH = 64
SSD_HEADS = MIX // SSD_DH
SSD_GROUPS = 4
SSD_HPG = SSD_HEADS // SSD_GROUPS
SSD_DSTATE = 128
SSD_CONV = 4
SSD_CONV_DIM = MIX + 2 * SSD_GROUPS * SSD_DSTATE
SSD_CHUNK = 128
POOL_WINDOWS = (2, 4, 8, 16)
POOL_GC = MIX // len(POOL_WINDOWS)
POOL_HIST = max(POOL_WINDOWS) - 1
RMS_EPS = 1e-6

LANE = 128
SUBLANE = 8
NEG = -1e30

_COL = {}
_off = 0
for _name, _width in (("qa", MIX), ("ka", MIX), ("va", MIX), ("oa", MIX), ("za", MIX),
                      ("qb", MIX), ("zb", MIX), ("zc", MIX), ("xbc", SSD_CONV_DIM),
                      ("ud", MIX), ("zd", MIX),
                      ("gate", LANE), ("kb", LANE), ("vb", LANE), ("dt", LANE)):
    _COL[_name] = (_off, _width)
    _off += _width
NPROJ = _off


def _colblock(name):
    off, width = _COL[name]
    assert off % width == 0
    return off // width


def _silu(x):
    return x * jax.nn.sigmoid(x)


def _dot(a, b):
    return jnp.dot(a, b, preferred_element_type=F32)


def _dot_nt(a, b):
    return lax.dot_general(a, b, (((1,), (1,)), ((), ())), preferred_element_type=F32)


def _dot_tn(a, b):
    return lax.dot_general(a, b, (((0,), (0,)), ((), ())), preferred_element_type=F32)


def _cumsum_rows(x):
    n = x.shape[0]
    r = lax.broadcasted_iota(jnp.int32, (n, n), 0)
    c = lax.broadcasted_iota(jnp.int32, (n, n), 1)
    tri = (c <= r).astype(F32)
    return jnp.dot(tri, x, preferred_element_type=F32, precision=lax.Precision.HIGHEST)


def _params(sem, vmem_mb):
    return pltpu.CompilerParams(dimension_semantics=sem, vmem_limit_bytes=vmem_mb * 1024 * 1024)


_ANY = pl.BlockSpec(memory_space=pl.ANY)


class _Part:
    def __init__(self, body, operands, in_specs, out_specs, out_shape, n_alias, scratch=()):
        self.body, self.operands, self.in_specs, self.out_specs = body, operands, in_specs, out_specs
        self.out_shape, self.n_alias, self.scratch = out_shape, n_alias, list(scratch)


def _fused_call(parts, grid, name, vmem_mb):
    n_in = [len(p.operands) for p in parts]
    n_out = [len(p.out_shape) for p in parts]
    n_scr = [len(p.scratch) for p in parts]

    def body(*refs):
        ins, outs, scr = refs[:sum(n_in)], refs[sum(n_in):sum(n_in) + sum(n_out)], refs[sum(n_in) + sum(n_out):]

        def run(first_chunk_init):
            i = o = s = 0
            for k, p in enumerate(parts):
                p.body(*ins[i:i + n_in[k]], *outs[o:o + n_out[k]], *scr[s:s + n_scr[k]],
                       first_chunk_init=first_chunk_init)
                i, o, s = i + n_in[k], o + n_out[k], s + n_scr[k]

        pl.when(pl.program_id(len(grid) - 1) == 0)(functools.partial(run, True))
        run(False)

    aliases, i, o = {}, 0, 0
    for k, p in enumerate(parts):
        for a in range(p.n_alias):
            aliases[i + n_in[k] - p.n_alias + a] = o + 1 + a
        i, o = i + n_in[k], o + n_out[k]
    flat = pl.pallas_call(
        body,
        grid=grid,
        in_specs=[s for p in parts for s in p.in_specs],
        out_specs=[s for p in parts for s in p.out_specs],
        out_shape=[s for p in parts for s in p.out_shape],
        scratch_shapes=[s for p in parts for s in p.scratch],
        input_output_aliases=aliases,
        compiler_params=_params(("arbitrary",) * len(grid), vmem_mb),
        name=name,
    )(*[x for p in parts for x in p.operands])
    out, o = [], 0
    for k in range(len(parts)):
        out.append(list(flat[o:o + n_out[k]]))
        o += n_out[k]
    return out


def _state_spec(stacked, layer):
    tail = stacked.shape[2:]
    return pl.BlockSpec((None, 1) + tail, lambda s, c: (layer, s) + (0,) * len(tail))


_NORM_ROWS = 64


def _inproj_kernel(x_ref, g_ref, w_ref, o_ref, h_ref):
    @pl.when(pl.program_id(1) == 0)
    def _():
        rows = min(_NORM_ROWS, x_ref.shape[0])

        def body(it, carry):
            rs = pl.ds(pl.multiple_of(it * rows, rows), rows)
            x = x_ref[rs, :]
            ms = jnp.mean(x * x, axis=-1, keepdims=True)
            h_ref[rs, :] = (x * lax.rsqrt(ms + RMS_EPS) * g_ref[...]).astype(BF16)
            return carry

        lax.fori_loop(0, x_ref.shape[0] // rows, body, 0)

    o_ref[...] = _dot(h_ref[...], w_ref[...])


def _inproj(x, g, w_all, layer, tm, tn):
    m = x.shape[0]
    return pl.pallas_call(
        _inproj_kernel,
        grid=(m // tm, NPROJ // tn),
        in_specs=[pl.BlockSpec((tm, D_MODEL), lambda i, j: (i, 0)),
                  pl.BlockSpec((1, D_MODEL), lambda i, j: (0, 0)),
                  pl.BlockSpec((None, D_MODEL, tn), lambda i, j: (layer, 0, j))],
        out_specs=pl.BlockSpec((tm, tn), lambda i, j: (i, j)),
        out_shape=jax.ShapeDtypeStruct((m, NPROJ), F32),
        scratch_shapes=[pltpu.VMEM((tm, D_MODEL), BF16)],
        compiler_params=_params(("arbitrary", "arbitrary"), 56),
        name="inproj",
    )(x, g, w_all)


def _outproj_kernel(ya_ref, yb_ref, yc_ref, yd_ref, w_ref, x_ref, g_ref, o_ref, acc_ref, ss_ref, *, tn, n_row_tiles):
    i = pl.program_id(0)
    j = pl.program_id(1)
    slot = i % 2
    col = pl.ds(pl.multiple_of(j * tn, tn), tn)

    @pl.when(i < n_row_tiles)
    def _():
        out = None
        for part, r in enumerate((ya_ref, yb_ref, yc_ref, yd_ref)):
            term = _dot(r[...].astype(BF16), w_ref[part * MIX:(part + 1) * MIX, :])
            out = term if out is None else out + term
        acc_ref[slot, :, col] = out
        sq = jnp.sum(out * out, axis=1, keepdims=True)
        ss_ref[slot] = jnp.where(j == 0, sq, ss_ref[slot] + sq)

    @pl.when(i > 0)
    def _():
        rs = lax.rsqrt(ss_ref[1 - slot] * (1.0 / D_MODEL) + RMS_EPS)
        o_ref[...] = x_ref[...] + acc_ref[1 - slot, :, col] * rs * g_ref[...]


def _outproj(ys, w_all, layer, x, g, tm, tn):
    m = x.shape[0]
    n_row_tiles = m // tm
    yspec = pl.BlockSpec((tm, MIX), lambda i, j: (jnp.minimum(i, n_row_tiles - 1), 0))
    lagged = pl.BlockSpec((tm, tn), lambda i, j: (jnp.maximum(i - 1, 0), jnp.where(i > 0, j, 0)))
    return pl.pallas_call(
        functools.partial(_outproj_kernel, tn=tn, n_row_tiles=n_row_tiles),
        grid=(n_row_tiles + 1, D_MODEL // tn),
        in_specs=[yspec, yspec, yspec, yspec,
                  pl.BlockSpec((None, D_MODEL, tn), lambda i, j: (layer, 0, j)),
                  lagged,
                  pl.BlockSpec((1, tn), lambda i, j: (0, j))],
        out_specs=lagged,
        out_shape=jax.ShapeDtypeStruct((m, D_MODEL), F32),
        scratch_shapes=[pltpu.VMEM((2, tm, D_MODEL), F32), pltpu.VMEM((2, tm, 1), F32)],
        compiler_params=_params(("arbitrary", "arbitrary"), 56),
        name="outproj",
    )(*ys, w_all, x, g)


def _mlstm_kernel(q_ref, k_ref, v_ref, o_ref, z_ref, gate_ref, bias_ref, mn_ref, c0_ref, n0_ref, m0_ref,
                  c_alias, n_alias, m_alias, y_ref, c_ref, n_ref, m_ref, *, L, valid, first_chunk_init):
    del c_alias, n_alias, m_alias
    if first_chunk_init:
        c_ref[...] = c0_ref[...]
        n_ref[...] = n0_ref[...]
        m_ref[...] = m0_ref[...]
        return

    row = lax.broadcasted_iota(jnp.int32, (L, LANE), 0)
    row_ok = row < valid
    gates = GATE_CAP * jnp.tanh((gate_ref[...] + bias_ref[...]) / GATE_CAP)
    logf = jnp.where(row_ok, jax.nn.log_sigmoid(gates), 0.0)
    b_all = _cumsum_rows(pltpu.roll(logf, LANE - MLSTM_HEADS, axis=1))
    r_all = jnp.where(row_ok, gates - b_all, NEG)
    r_all_t = r_all.T

    tr = lax.broadcasted_iota(jnp.int32, (L, L), 0)
    tc = lax.broadcasted_iota(jnp.int32, (L, L), 1)
    tril = tc <= tr

    heads = range(MLSTM_HEADS)
    sl = [slice(h * MLSTM_DH, (h + 1) * MLSTM_DH) for h in heads]
    q = [q_ref[:, sl[h]] * (MLSTM_DH ** -0.5) for h in heads]
    k = [k_ref[:, sl[h]] for h in heads]
    v = [v_ref[:, sl[h]] for h in heads]
    c_prev = [c_ref[0, h] for h in heads]
    n_prev = [n_ref[0, h:h + 1, :] for h in heads]
    m_prev = [m_ref[0, h:h + 1, 0:1] for h in heads]
    qk = [_dot_nt(q[h], k[h]) for h in heads]
    qc = [_dot(q[h], c_prev[h]) for h in heads]
    qn = [jnp.sum(q[h] * n_prev[h], axis=1, keepdims=True) for h in heads]

    b_col = [b_all[:, h:h + 1] for h in heads]
    dmat = [jnp.where(tril, b_col[h] + r_all_t[h:h + 1, :], NEG) for h in heads]
    inter = [b_col[h] + m_prev[h] for h in heads]
    mt = [jnp.maximum(inter[h], jnp.max(dmat[h], axis=1, keepdims=True)) for h in heads]
    s = [jnp.exp(dmat[h] - mt[h]) * qk[h] for h in heads]
    sv = [_dot(s[h], v[h]) for h in heads]
    w_prev_t = [jnp.exp(inter[h] - mt[h]) for h in heads]
    den = [w_prev_t[h] * qn[h] + jnp.sum(s[h], axis=1, keepdims=True) for h in heads]
    hh = [(w_prev_t[h] * qc[h] + sv[h]) / jnp.maximum(jnp.abs(den[h]), jnp.exp(-mt[h])) for h in heads]

    b_last = [b_all[L - 1:L, h:h + 1] for h in heads]
    dec = [b_last[h] + r_all[:, h:h + 1] for h in heads]
    m_new = [jnp.maximum(b_last[h] + m_prev[h], jnp.max(dec[h], axis=0, keepdims=True)) for h in heads]
    w_prev = [jnp.exp(b_last[h] + m_prev[h] - m_new[h]) for h in heads]
    kw = [k[h] * jnp.exp(dec[h] - m_new[h]) for h in heads]
    c_new = [w_prev[h] * c_prev[h] + _dot_tn(kw[h], v[h]) for h in heads]
    n_new = [w_prev[h] * n_prev[h] + jnp.sum(kw[h], axis=0, keepdims=True) for h in heads]

    hn = [hh[h] * lax.rsqrt(jnp.mean(hh[h] * hh[h], axis=1, keepdims=True) + RMS_EPS) * mn_ref[:, sl[h]] for h in heads]
    ys = [hn[h] * jax.nn.sigmoid(o_ref[:, sl[h]]) * _silu(z_ref[:, sl[h]]) for h in heads]

    y_ref[...] = jnp.concatenate(ys, axis=1).astype(y_ref.dtype)
    for h in heads:
        c_ref[0, h] = c_new[h]
    n_ref[0] = jnp.concatenate(n_new, axis=0)
    m_ref[0] = jnp.concatenate([jnp.broadcast_to(m_new[h], (1, LANE)) for h in heads], axis=0)


def _mlstm(proj, bias, mnorm, st0, l0, acc, layer, n_seq, t_pad, L, valid, y_dtype):
    nc = t_pad // L
    wide = lambda name: pl.BlockSpec((L, MIX), lambda s, c, cb=_colblock(name): (s * nc + c, cb))
    vec = lambda n: pl.BlockSpec((1, n), lambda s, c: (0, 0))
    return _Part(
        functools.partial(_mlstm_kernel, L=L, valid=valid),
        operands=[proj, proj, proj, proj, proj, proj, bias, mnorm, *st0, *acc],
        in_specs=[wide("qa"), wide("ka"), wide("va"), wide("oa"), wide("za"),
                  pl.BlockSpec((L, LANE), lambda s, c, cb=_colblock("gate"): (s * nc + c, cb)),
                  vec(LANE), vec(MIX)] + [_state_spec(a, l0) for a in st0] + [_ANY] * len(acc),
        out_specs=[pl.BlockSpec((L, MIX), lambda s, c: (s * nc + c, 0))] + [_state_spec(a, layer) for a in acc],
        out_shape=[jax.ShapeDtypeStruct((n_seq * t_pad, MIX), y_dtype)]
        + [jax.ShapeDtypeStruct(a.shape, F32) for a in acc],
        n_alias=len(acc))


def _swa_kernel(q_ref, k_ref, v_ref, z_ref, cos_ref, sin_ref, sink_ref, kc_ref, vc_ref, k_alias, v_alias,
                y_ref, ko_ref, vo_ref, *, Lq, valid, has_cache, first_chunk_init):
    del k_alias, v_alias
    if first_chunk_init:
        ko_ref[...] = kc_ref[...]
        vo_ref[...] = vc_ref[...]
        return
    blk = pl.program_id(1)

    cos = cos_ref[...]
    sin = sin_ref[...]
    lane = lax.broadcasted_iota(jnp.int32, (Lq, LANE), 1)
    first_half = (lane % SWA_DH) < (SWA_DH // 2)

    def rope(x):
        partner = jnp.where(first_half, pltpu.roll(x, LANE - SWA_DH // 2, axis=1), pltpu.roll(x, SWA_DH // 2, axis=1))
        return x * cos + partner * sin

    k_cur = rope(k_ref[...])
    v_cur = v_ref[...]
    k_prev = ko_ref[0]
    v_prev = vo_ref[0]
    kk = jnp.concatenate([k_prev, k_cur], axis=0)
    vv = jnp.concatenate([v_prev, v_cur], axis=0)

    rows, nk = SWA_GROUP * Lq, WINDOW + Lq
    t = lax.broadcasted_iota(jnp.int32, (rows, nk), 0) & (Lq - 1)
    c = lax.broadcasted_iota(jnp.int32, (rows, nk), 1)
    prev_ok = jnp.logical_or(has_cache, blk > 0)
    mask = ((c < WINDOW) & (c > t) & prev_ok) | ((c >= WINDOW) & (c - WINDOW <= t) & (c - WINDOW < valid))

    pairs_per_group = SWA_GROUP // 2
    y_chunks = []
    for g in range(SWA_KV_HEADS):
        gsl = slice(g * SWA_DH, (g + 1) * SWA_DH)
        q_pairs = [rope(q_ref[:, (g * pairs_per_group + p) * LANE:(g * pairs_per_group + p + 1) * LANE])
                   * (SWA_DH ** -0.5) for p in range(pairs_per_group)]
        q_st = jnp.concatenate([q_pairs[h // 2][:, (h % 2) * SWA_DH:(h % 2 + 1) * SWA_DH]
                                for h in range(SWA_GROUP)], axis=0)
        sink = jnp.concatenate([jnp.broadcast_to(sink_ref[0:1, g * SWA_GROUP + h:g * SWA_GROUP + h + 1], (Lq, 1))
                                for h in range(SWA_GROUP)], axis=0)
        s = jnp.where(mask, _dot_nt(q_st, kk[:, gsl]), NEG)
        mx = jnp.maximum(jnp.max(s, axis=1, keepdims=True), sink)
        p = jnp.exp(s - mx)
        den = jnp.sum(p, axis=1, keepdims=True) + jnp.exp(sink - mx)
        o = _dot(p, vv[:, gsl]) / den
        for pr in range(pairs_per_group):
            csl = slice((g * pairs_per_group + pr) * LANE, (g * pairs_per_group + pr + 1) * LANE)
            pair = jnp.concatenate([o[(2 * pr) * Lq:(2 * pr + 1) * Lq], o[(2 * pr + 1) * Lq:(2 * pr + 2) * Lq]], axis=1)
            y_chunks.append(pair * _silu(z_ref[:, csl]))
    y_ref[...] = jnp.concatenate(y_chunks, axis=1).astype(y_ref.dtype)

    if valid == WINDOW:
        ko_ref[0] = k_cur
        vo_ref[0] = v_cur
    else:
        ko_ref[0, 0:WINDOW - valid, :] = k_prev[valid:, :]
        ko_ref[0, WINDOW - valid:WINDOW, :] = k_cur[0:valid, :]
        vo_ref[0, 0:WINDOW - valid, :] = v_prev[valid:, :]
        vo_ref[0, WINDOW - valid:WINDOW, :] = v_cur[0:valid, :]


def _swa(proj, cos, sin, sinks, st0, l0, acc, layer, n_seq, t_pad, Lq, valid, has_cache, table_per_block, y_dtype):
    nb = t_pad // Lq
    rowblk = lambda s, b: s * nb + b
    wide = lambda name: pl.BlockSpec((Lq, MIX), lambda s, b, cb=_colblock(name): (rowblk(s, b), cb))
    narrow = lambda name: pl.BlockSpec((Lq, LANE), lambda s, b, cb=_colblock(name): (rowblk(s, b), cb))
    table = pl.BlockSpec((Lq, LANE), (lambda s, b: (b, 0)) if table_per_block else (lambda s, b: (0, 0)))
    return _Part(
        functools.partial(_swa_kernel, Lq=Lq, valid=valid, has_cache=has_cache),
        operands=[proj, proj, proj, proj, cos, sin, sinks, *st0, *acc],
        in_specs=[wide("qb"), narrow("kb"), narrow("vb"), wide("zb"), table, table,
                  pl.BlockSpec((1, LANE), lambda s, b: (0, 0))] + [_state_spec(a, l0) for a in st0] + [_ANY] * len(acc),
        out_specs=[pl.BlockSpec((Lq, MIX), lambda s, b: (rowblk(s, b), 0))] + [_state_spec(a, layer) for a in acc],
        out_shape=[jax.ShapeDtypeStruct((n_seq * t_pad, MIX), y_dtype)]
        + [jax.ShapeDtypeStruct(a.shape, F32) for a in acc],
        n_alias=len(acc))


_CONV_PAD = SUBLANE


def _spread_heads(x, e):
    lane = lax.broadcasted_iota(jnp.int32, x.shape, 1)
    x = jnp.where(lane < SSD_HEADS, x, 0.0)
    hi = x.astype(BF16)
    rest = x - hi.astype(F32)
    mid = rest.astype(BF16)
    lo = (rest - mid.astype(F32)).astype(BF16)
    return _dot(hi, e) + _dot(mid, e) + _dot(lo, e)


def _ssd_kernel(xbc_ref, z_ref, dt_ref, cw_ref, cb_ref, dtb_ref, alog_ref, dskip_ref, sn_ref, e_ref, conv0_ref, s0_ref,
                conv_alias, s_alias, y_ref, convo_ref, so_ref, full_ref, *, L, valid, first_chunk_init):
    del conv_alias, s_alias
    hist = SSD_CONV - 1
    if first_chunk_init:
        so_ref[...] = s0_ref[...]
        full_ref[_CONV_PAD - hist:_CONV_PAD, :] = conv0_ref[0]
        return

    full_ref[_CONV_PAD:_CONV_PAD + L, :] = xbc_ref[...]
    full = full_ref[...]
    acc = cb_ref[...] + full[_CONV_PAD:, :] * cw_ref[hist:hist + 1, :]
    for j in range(hist):
        acc = acc + pltpu.roll(full, hist - j, axis=0)[_CONV_PAD:, :] * cw_ref[j:j + 1, :]
    xc = _silu(acc)
    new_hist = full_ref[_CONV_PAD + valid - hist:_CONV_PAD + valid, :]
    convo_ref[0] = new_hist
    full_ref[_CONV_PAD - hist:_CONV_PAD, :] = new_hist

    nbc = SSD_GROUPS * SSD_DSTATE
    row_ok = lax.broadcasted_iota(jnp.int32, (L, LANE), 0) < valid
    dtv = jnp.where(row_ok, jax.nn.softplus(dt_ref[...] + dtb_ref[...]), 0.0)
    a = dtv * (-jnp.exp(alog_ref[...]))
    cum = _cumsum_rows(a)
    cum_t = cum.T
    tr = lax.broadcasted_iota(jnp.int32, (L, L), 0)
    tc = lax.broadcasted_iota(jnp.int32, (L, L), 1)
    tril = tc <= tr

    groups, heads = range(SSD_GROUPS), range(SSD_HEADS)
    gp = SSD_HPG * SSD_DH
    bg = [xc[:, MIX + g * SSD_DSTATE:MIX + (g + 1) * SSD_DSTATE] for g in groups]
    cg = [xc[:, MIX + nbc + g * SSD_DSTATE:MIX + nbc + (g + 1) * SSD_DSTATE] for g in groups]
    st = [so_ref[0, g * SSD_HPG:(g + 1) * SSD_HPG].reshape(gp, SSD_DSTATE) for g in groups]
    cb = [_dot_nt(cg[g], bg[g]) for g in groups]
    cst = [_dot_nt(cg[g], st[g]) for g in groups]
    c_col = [cum[:, h:h + 1] for h in heads]
    c_last = [cum[L - 1:L, h:h + 1] for h in heads]
    lm = [jnp.where(tril, jnp.exp(jnp.where(tril, c_col[h] - cum_t[h:h + 1, :], 0.0)), 0.0) for h in heads]

    spread = _spread_heads(jnp.concatenate([dtv, jnp.exp(cum), jnp.exp(cum[L - 1:L, :] - cum)], axis=0), e_ref[...])
    xs = xc[:, :MIX]
    xdt = xs * spread[0:L]
    xw = xdt * spread[2 * L:3 * L]
    low_half = lax.broadcasted_iota(jnp.int32, (L, LANE), 1) < SSD_DH
    y_in = []
    for pair in range(SSD_HEADS // 2):
        g = (2 * pair) // SSD_HPG
        x_pair = xdt[:, pair * LANE:(pair + 1) * LANE]
        y_in.append(jnp.where(low_half, _dot(cb[g] * lm[2 * pair], x_pair), _dot(cb[g] * lm[2 * pair + 1], x_pair)))
    y_state = jnp.concatenate(cst, axis=1) * spread[L:2 * L]
    upd = [_dot_tn(xw[:, g * gp:(g + 1) * gp], bg[g]) for g in groups]
    for h in heads:
        g, hh = divmod(h, SSD_HPG)
        so_ref[0, h] = jnp.exp(c_last[h]) * st[g][hh * SSD_DH:(hh + 1) * SSD_DH] + upd[g][hh * SSD_DH:(hh + 1) * SSD_DH]
    yc = (jnp.concatenate(y_in, axis=1) + y_state + dskip_ref[...] * xs) * _silu(z_ref[...])
    y_ref[...] = (yc * lax.rsqrt(jnp.mean(yc * yc, axis=1, keepdims=True) + RMS_EPS) * sn_ref[...]).astype(y_ref.dtype)


def _ssd(proj, conv_w, conv_b, dt_bias, a_log, d_skip, snorm, st0, l0, acc, layer, n_seq, t_pad, L, valid, y_dtype):
    nc = t_pad // L
    rowblk = lambda s, c: s * nc + c
    vec = lambda n: pl.BlockSpec((1, n), lambda s, c: (0, 0))
    spread = (jnp.arange(MIX)[None, :] // SSD_DH == jnp.arange(LANE)[:, None]).astype(BF16)
    return _Part(
        functools.partial(_ssd_kernel, L=L, valid=valid),
        operands=[proj, proj, proj, conv_w, conv_b, dt_bias, a_log, d_skip, snorm, spread, *st0, *acc],
        in_specs=[pl.BlockSpec((L, SSD_CONV_DIM), lambda s, c, cb=_colblock("xbc"): (rowblk(s, c), cb)),
                  pl.BlockSpec((L, MIX), lambda s, c, cb=_colblock("zc"): (rowblk(s, c), cb)),
                  pl.BlockSpec((L, LANE), lambda s, c, cb=_colblock("dt"): (rowblk(s, c), cb)),
                  pl.BlockSpec((SSD_CONV, SSD_CONV_DIM), lambda s, c: (0, 0)),
                  vec(SSD_CONV_DIM), vec(LANE), vec(LANE), vec(MIX), vec(MIX),
                  pl.BlockSpec((LANE, MIX), lambda s, c: (0, 0))]
        + [_state_spec(a, l0) for a in st0] + [_ANY] * len(acc),
        out_specs=[pl.BlockSpec((L, MIX), lambda s, c: (rowblk(s, c), 0))] + [_state_spec(a, layer) for a in acc],
        out_shape=[jax.ShapeDtypeStruct((n_seq * t_pad, MIX), y_dtype)]
        + [jax.ShapeDtypeStruct(a.shape, F32) for a in acc],
        n_alias=len(acc),
        scratch=[pltpu.VMEM((_CONV_PAD + L, SSD_CONV_DIM), F32)])


_POOL_PAD = 2 * SUBLANE


def _pool_kernel(u_ref, z_ref, lin_ref, scale_ref, p0_ref, p_alias, y_ref, po_ref, full_ref, *, L, valid, n_hist,
                 first_chunk_init):
    del p_alias
    if first_chunk_init:
        full_ref[_POOL_PAD - POOL_HIST:_POOL_PAD, :] = p0_ref[0]
        return
    step = pl.program_id(1)

    full_ref[_POOL_PAD:_POOL_PAD + L, :] = u_ref[...]
    pos = n_hist + step * L + lax.broadcasted_iota(jnp.int32, (L, 1), 0)
    for g, w in enumerate(POOL_WINDOWS):
        sl = slice(g * POOL_GC, (g + 1) * POOL_GC)
        cur = full_ref[_POOL_PAD:_POOL_PAD + L, sl]
        tot = cur
        for j in range(1, w):
            tot = tot + full_ref[_POOL_PAD - j:_POOL_PAD - j + L, sl]
        cnt = jnp.minimum(pos + 1, w).astype(F32)
        d = tot / cnt - cur
        y = _dot(d, lin_ref[g]) * scale_ref[:, sl] * _silu(z_ref[:, sl])
        y_ref[:, sl] = y.astype(y_ref.dtype)
    new_hist = full_ref[_POOL_PAD + valid - POOL_HIST:_POOL_PAD + valid, :]
    po_ref[0] = new_hist
    full_ref[_POOL_PAD - POOL_HIST:_POOL_PAD, :] = new_hist


def _pool(proj, lin, scale, st0, l0, acc, layer, n_seq, t_pad, L, valid, n_hist, y_dtype):
    nt = t_pad // L
    rowblk = lambda s, c: s * nt + c
    return _Part(
        functools.partial(_pool_kernel, L=L, valid=valid, n_hist=n_hist),
        operands=[proj, proj, lin, scale, *st0, *acc],
        in_specs=[pl.BlockSpec((L, MIX), lambda s, c, cb=_colblock("ud"): (rowblk(s, c), cb)),
                  pl.BlockSpec((L, MIX), lambda s, c, cb=_colblock("zd"): (rowblk(s, c), cb)),
                  pl.BlockSpec((len(POOL_WINDOWS), POOL_GC, POOL_GC), lambda s, c: (0, 0, 0)),
                  pl.BlockSpec((1, MIX), lambda s, c: (0, 0))] + [_state_spec(a, l0) for a in st0] + [_ANY] * len(acc),
        out_specs=[pl.BlockSpec((L, MIX), lambda s, c: (rowblk(s, c), 0))] + [_state_spec(a, layer) for a in acc],
        out_shape=[jax.ShapeDtypeStruct((n_seq * t_pad, MIX), y_dtype)]
        + [jax.ShapeDtypeStruct(a.shape, F32) for a in acc],
        n_alias=len(acc),
        scratch=[pltpu.VMEM((_POOL_PAD + L, MIX), F32)])


_REF_SIZES = (MIX, MIX, MIX, MIX, MIX, MLSTM_HEADS, MLSTM_HEADS,
              SWA_HEADS * SWA_DH, SWA_KV_HEADS * SWA_DH, SWA_KV_HEADS * SWA_DH, MIX,
              MIX, SSD_CONV_DIM, SSD_HEADS, MIX, MIX)
_REF_NAMES = ("qa", "ka", "va", "oa", "za", "ia", "fa", "qb", "kb", "vb", "zb", "zc", "xbc", "dt", "ud", "zd")


def _regroup_plan():
    ref_off, o = {}, 0
    for name, size in zip(_REF_NAMES, _REF_SIZES):
        ref_off[name] = o
        o += size
    src = []
    for name, (off, width) in _COL.items():
        start = ref_off["ia"] if name == "gate" else ref_off[name]
        src += [start + b * LANE for b in range(width // LANE)]
    assert len(src) == NPROJ // LANE and all(s % SUBLANE == 0 and s + LANE <= o for s in src)
    return src


_REGROUP_BLOCKS = 4


def _regroup_kernel(src_tab, *refs):
    del src_tab
    o_ref = refs[-1]
    for i, wt_ref in enumerate(refs[:-1]):
        o_ref[:, i * LANE:(i + 1) * LANE] = wt_ref[...].T.astype(BF16)


def _regroup_w_in(w_in):
    depth, d_in, _ = w_in.shape
    w_t = jnp.transpose(w_in, (0, 2, 1))
    src = jnp.asarray([s // SUBLANE for s in _regroup_plan()], jnp.int32)
    return pl.pallas_call(
        _regroup_kernel,
        grid_spec=pltpu.PrefetchScalarGridSpec(
            num_scalar_prefetch=1,
            grid=(depth, NPROJ // (_REGROUP_BLOCKS * LANE)),
            in_specs=[pl.BlockSpec((None, pl.Element(LANE), pl.Element(d_in)),
                                   lambda l, t, src, i=i: (l, src[t * _REGROUP_BLOCKS + i] * SUBLANE, 0))
                      for i in range(_REGROUP_BLOCKS)],
            out_specs=pl.BlockSpec((None, d_in, _REGROUP_BLOCKS * LANE), lambda l, t, src: (l, 0, t))),
        out_shape=jax.ShapeDtypeStruct((depth, d_in, NPROJ), BF16),
        compiler_params=_params(("arbitrary", "arbitrary"), 48),
        name="regroup",
    )(src, *([w_t] * _REGROUP_BLOCKS))


def _pad_lanes(v, n=LANE):
    return jnp.pad(v, (0, n - v.shape[0])).reshape(1, n)


def _rope_tables(pos):
    half = SWA_DH // 2
    inv = ROPE_THETA ** (-jnp.arange(half, dtype=F32) / half)
    ang = pos.astype(F32)[:, None] * inv[None, :]
    cos, sin = jnp.cos(ang), jnp.sin(ang)
    reps = LANE // SWA_DH
    return jnp.tile(jnp.concatenate([cos, cos], axis=1), (1, reps)), jnp.tile(jnp.concatenate([-sin, sin], axis=1), (1, reps))


class _Path:
    def __init__(self, n_seq, t_pad, valid, chunk, has_cache, n_hist, tm_in, tn_in, tm_out, tn_out, y_dtype):
        self.n_seq, self.t_pad, self.valid, self.chunk = n_seq, t_pad, valid, chunk
        self.has_cache, self.n_hist, self.y_dtype = has_cache, n_hist, y_dtype
        self.tm_in, self.tn_in, self.tm_out, self.tn_out = tm_in, tn_in, tm_out, tn_out


def _layer(x, layer, prm, states, l0, acc, tables, path):
    (g_pre, g_post, w_in, w_out, gate_bias, mnorm, sinks, conv_w, conv_b, dt_bias, a_log, d_skip, snorm,
     pool_lin, pool_scale) = prm
    c0, n0, m0, kc, vc, s0, conv0, p0 = states
    ca, na, ma, ka, va, sa, conva, pa = acc
    cos, sin = tables
    p = path
    proj = _inproj(x, g_pre, w_in, layer, p.tm_in, p.tn_in)
    L, valid = p.chunk, min(p.valid, p.chunk)
    parts = [
        _mlstm(proj, gate_bias, mnorm, (c0, n0, m0), l0, (ca, na, ma), layer, p.n_seq, p.t_pad, L, valid, p.y_dtype),
        _swa(proj, cos, sin, sinks, (kc, vc), l0, (ka, va), layer, p.n_seq, p.t_pad, L, valid,
             p.has_cache, not p.has_cache, p.y_dtype),
        _ssd(proj, conv_w, conv_b, dt_bias, a_log, d_skip, snorm, (conv0, s0), l0, (conva, sa), layer,
             p.n_seq, p.t_pad, L, valid, p.y_dtype),
        _pool(proj, pool_lin, pool_scale, (p0,), l0, (pa,), layer, p.n_seq, p.t_pad, L, valid, p.n_hist, p.y_dtype)]
    (ya, ca, na, ma), (yb, ka, va), (yc, conva, sa), (yd, pa) = _fused_call(
        parts, (p.n_seq, p.t_pad // L), "mixers", 48)
    x = _outproj((ya, yb, yc, yd), w_out, layer, x, g_post, p.tm_out, p.tn_out)
    return x, (ca, na, ma, ka, va, sa, conva, pa)


def kernel(x_prompt, x_sample, state_mlstm_C, state_mlstm_n, state_mlstm_m, cache_swa_k, cache_swa_v, state_ssd,
           state_ssd_conv, state_pool, norm_pre, norm_post, w_in, w_out, mlstm_b_i, mlstm_b_f, mlstm_norm,
           swa_sinks, ssd_conv_w, ssd_conv_b, ssd_dt_bias, ssd_A_log, ssd_D, ssd_norm, pool_lin, pool_scale):
    bp, seq, _ = x_prompt.shape
    bs, dec_seq, _ = x_sample.shape
    t_s = SUBLANE * pl.cdiv(dec_seq, SUBLANE)

    assert WINDOW == SSD_CHUNK == 2 * MLSTM_CHUNK
    prompt = _Path(bp, seq, seq, WINDOW, False, 0, 512, 1280, 512, 1024, BF16)
    sample = _Path(bs, t_s, dec_seq, t_s, True, POOL_HIST, bs * t_s, 1280, bs * t_s, 512, F32)

    w_in_r = _regroup_w_in(w_in)
    w_out_b = w_out.astype(BF16)

    xp = x_prompt.reshape(bp * seq, D_MODEL)
    xs = jnp.pad(x_sample, ((0, 0), (0, t_s - dec_seq), (0, 0))).reshape(bs * t_s, D_MODEL)

    tab_p = _rope_tables(jnp.arange(seq))
    tab_s = _rope_tables(PAST_LEN + jnp.arange(t_s))

    def state_shapes(layers, b):
        return ((layers, b, MLSTM_HEADS, MLSTM_DH, MLSTM_DH), (layers, b, MLSTM_HEADS, LANE), (layers, b, MLSTM_HEADS, LANE),
                (layers, b, WINDOW, LANE), (layers, b, WINDOW, LANE),
                (layers, b, SSD_HEADS, SSD_DH, SSD_DSTATE), (layers, b, SSD_CONV - 1, SSD_CONV_DIM),
                (layers, b, POOL_HIST, MIX))

    p_states0 = tuple(jnp.zeros(s, F32) for s in state_shapes(1, bp))
    s_states0 = (state_mlstm_C, state_mlstm_n,
                 jnp.broadcast_to(state_mlstm_m[..., None], (DEPTH, bs, MLSTM_HEADS, LANE)),
                 cache_swa_k.reshape(DEPTH, bs, WINDOW, LANE), cache_swa_v.reshape(DEPTH, bs, WINDOW, LANE),
                 state_ssd, state_ssd_conv, state_pool)
    p_acc = tuple(jnp.zeros(s, F32) for s in state_shapes(DEPTH, bp))
    s_acc = tuple(jnp.zeros(s, F32) for s in state_shapes(DEPTH, bs))

    for l in range(DEPTH):
        gate_bias = _pad_lanes(jnp.concatenate([mlstm_b_i[l], mlstm_b_f[l]]))
        prm = (norm_pre[l].reshape(1, D_MODEL), norm_post[l].reshape(1, D_MODEL), w_in_r, w_out_b,
               gate_bias, mlstm_norm[l].reshape(1, MIX), _pad_lanes(swa_sinks[l]),
               ssd_conv_w[l], ssd_conv_b[l].reshape(1, SSD_CONV_DIM), _pad_lanes(ssd_dt_bias[l]),
               _pad_lanes(ssd_A_log[l]), jnp.repeat(ssd_D[l], SSD_DH).reshape(1, MIX), ssd_norm[l].reshape(1, MIX),
               pool_lin[l], pool_scale[l].reshape(1, MIX))
        xp, p_acc = _layer(xp, l, prm, p_states0, 0, p_acc, tab_p, prompt)
        xs, s_acc = _layer(xs, l, prm, s_states0, l, s_acc, tab_s, sample)

    def finish(acc, b):
        c, n, m, k, v, s, conv, pool = acc
        kv_shape = (DEPTH, b, WINDOW, SWA_KV_HEADS, SWA_DH)
        return c, n, m[..., 0], k.reshape(kv_shape), v.reshape(kv_shape), s, conv, pool

    y_prompt = xp.reshape(bp, seq, D_MODEL)
    y_sample = xs.reshape(bs, t_s, D_MODEL)[:, :dec_seq]
    return (y_prompt, y_sample) + finish(p_acc, bp) + finish(s_acc, bs)
```

```python
import functools
import math

import jax
import jax.numpy as jnp
from jax import lax
from jax.experimental import pallas as pl
from jax.experimental.pallas import tpu as pltpu

F32 = jnp.float32
BF16 = jnp.bfloat16

D_MODEL = 4096
DEPTH = 4
PAST_LEN = 8192
MIX = D_MODEL // 4
MLSTM_DH = 128
MLSTM_HEADS = MIX // MLSTM_DH
MLSTM_CHUNK = 64
GATE_CAP = 15.0
SWA_DH = 64
SWA_HEADS = MIX // SWA_DH
SWA_KV_HEADS = 2
SWA_GROUP = SWA_HEADS // SWA_KV_HEADS
WINDOW = 128
ROPE_THETA = 10000.0
SSD_DH = 64
SSD_HEADS = MIX // SSD_DH
SSD_GROUPS = 4
SSD_HPG = SSD_HEADS // SSD_GROUPS
SSD_DSTATE = 128
SSD_CONV = 4
SSD_CONV_DIM = MIX + 2 * SSD_GROUPS * SSD_DSTATE
SSD_CHUNK = 128
POOL_WINDOWS = (2, 4, 8, 16)
POOL_GC = MIX // len(POOL_WINDOWS)
POOL_HIST = max(POOL_WINDOWS) - 1
RMS_EPS = 1e-6

LANE = 128
SUBLANE = 8
NEG = -1e30

_COL = {}
_off = 0
for _name, _width in (("qa", MIX), ("ka", MIX), ("va", MIX), ("oa", MIX), ("za", MIX),
                      ("qb", MIX), ("zb", MIX), ("zc", MIX), ("xbc", SSD_CONV_DIM),
                      ("ud", MIX), ("zd", MIX),
                      ("gate", LANE), ("kb", LANE), ("vb", LANE), ("dt", LANE)):
    _COL[_name] = (_off, _width)
    _off += _width
NPROJ = _off


def _colblock(name):
    off, width = _COL[name]
    assert off % width == 0
    return off // width


def _silu(x):
    return x * jax.nn.sigmoid(x)


def _dot(a, b):
    return jnp.dot(a, b, preferred_element_type=F32)


def _dot_nt(a, b):
    return lax.dot_general(a, b, (((1,), (1,)), ((), ())), preferred_element_type=F32)


def _dot_tn(a, b):
    return lax.dot_general(a, b, (((0,), (0,)), ((), ())), preferred_element_type=F32)


def _cumsum_rows(x):
    n = x.shape[0]
    r = lax.broadcasted_iota(jnp.int32, (n, n), 0)
    c = lax.broadcasted_iota(jnp.int32, (n, n), 1)
    tri = (c <= r).astype(F32)
    return jnp.dot(tri, x, preferred_element_type=F32, precision=lax.Precision.HIGHEST)


def _params(sem, vmem_mb):
    return pltpu.CompilerParams(dimension_semantics=sem, vmem_limit_bytes=vmem_mb * 1024 * 1024)


_ANY = pl.BlockSpec(memory_space=pl.ANY)


class _Part:
    def __init__(self, body, operands, in_specs, out_specs, out_shape, n_alias, scratch=()):
        self.body, self.operands, self.in_specs, self.out_specs = body, operands, in_specs, out_specs
        self.out_shape, self.n_alias, self.scratch = out_shape, n_alias, list(scratch)


def _fused_call(parts, grid, name, vmem_mb):
    n_in = [len(p.operands) for p in parts]
    n_out = [len(p.out_shape) for p in parts]
    n_scr = [len(p.scratch) for p in parts]

    def body(*refs):
        ins, outs, scr = refs[:sum(n_in)], refs[sum(n_in):sum(n_in) + sum(n_out)], refs[sum(n_in) + sum(n_out):]

        def run(first_chunk_init):
            i = o = s = 0
            for k, p in enumerate(parts):
                p.body(*ins[i:i + n_in[k]], *outs[o:o + n_out[k]], *scr[s:s + n_scr[k]],
                       first_chunk_init=first_chunk_init)
                i, o, s = i + n_in[k], o + n_out[k], s + n_scr[k]

        pl.when(pl.program_id(len(grid) - 1) == 0)(functools.partial(run, True))
        run(False)

    aliases, i, o = {}, 0, 0
    for k, p in enumerate(parts):
        for a in range(p.n_alias):
            aliases[i + n_in[k] - p.n_alias + a] = o + 1 + a
        i, o = i + n_in[k], o + n_out[k]
    flat = pl.pallas_call(
        body,
        grid=grid,
        in_specs=[s for p in parts for s in p.in_specs],
        out_specs=[s for p in parts for s in p.out_specs],
        out_shape=[s for p in parts for s in p.out_shape],
        scratch_shapes=[s for p in parts for s in p.scratch],
        input_output_aliases=aliases,
        compiler_params=_params(("arbitrary",) * len(grid), vmem_mb),
        name=name,
    )(*[x for p in parts for x in p.operands])
    out, o = [], 0
    for k in range(len(parts)):
        out.append(list(flat[o:o + n_out[k]]))
        o += n_out[k]
    return out


def _state_spec(stacked, layer):
    tail = stacked.shape[2:]
    return pl.BlockSpec((None, 1) + tail, lambda s, c: (layer, s) + (0,) * len(tail))


_NORM_STEPS = 8


def _inproj_kernel(x_ref, g_ref, w_ref, o_ref, h_ref):
    p = pl.program_id(0)
    j = pl.program_id(1)
    rows = x_ref.shape[0] // _NORM_STEPS

    def normalise_slice():
        rs = pl.ds(pl.multiple_of(jnp.minimum(j, _NORM_STEPS - 1) * rows, rows), rows)
        x = x_ref[rs, :]
        ms = jnp.mean(x * x, axis=-1, keepdims=True)
        h_ref[p % 2, rs, :] = (x * lax.rsqrt(ms + RMS_EPS) * g_ref[...]).astype(BF16)

    pl.when(p == 0)(normalise_slice)

    @pl.when(p > 0)
    def _():
        normalise_slice()
        o_ref[...] = _dot(h_ref[(p - 1) % 2], w_ref[...])


def _inproj(x, g, w_all, layer, tm, tn):
    m = x.shape[0]
    n_row_tiles, n_col = m // tm, NPROJ // tn
    assert n_col >= _NORM_STEPS and tm % (_NORM_STEPS * 2 * SUBLANE) == 0
    col = lambda p, j: jnp.where(p > 0, j, 0)
    return pl.pallas_call(
        _inproj_kernel,
        grid=(n_row_tiles + 1, n_col),
        in_specs=[pl.BlockSpec((tm, D_MODEL), lambda p, j: (jnp.minimum(p, n_row_tiles - 1), 0)),
                  pl.BlockSpec((1, D_MODEL), lambda p, j: (0, 0)),
                  pl.BlockSpec((None, D_MODEL, tn), lambda p, j: (layer, 0, col(p, j)))],
        out_specs=pl.BlockSpec((tm, tn), lambda p, j: (jnp.maximum(p - 1, 0), col(p, j))),
        out_shape=jax.ShapeDtypeStruct((m, NPROJ), F32),
        scratch_shapes=[pltpu.VMEM((2, tm, D_MODEL), BF16)],
        compiler_params=_params(("arbitrary", "arbitrary"), 56),
        name="inproj",
    )(x, g, w_all)


def _outproj_kernel(ya_ref, yb_ref, yc_ref, yd_ref, w_ref, x_ref, g_ref, o_ref, acc_ref, ss_ref, *, tn, n_row_tiles):
    i = pl.program_id(0)
    j = pl.program_id(1)
    slot = i % 2
    col = pl.ds(pl.multiple_of(j * tn, tn), tn)

    @pl.when(i < n_row_tiles)
    def _():
        out = None
        for part, r in enumerate((ya_ref, yb_ref, yc_ref, yd_ref)):
            term = _dot(r[...].astype(BF16), w_ref[part * MIX:(part + 1) * MIX, :])
            out = term if out is None else out + term
        acc_ref[slot, :, col] = out
        sq = jnp.sum(out * out, axis=1, keepdims=True)
        ss_ref[slot] = jnp.where(j == 0, sq, ss_ref[slot] + sq)

    @pl.when(i > 0)
    def _():
        rs = lax.rsqrt(ss_ref[1 - slot] * (1.0 / D_MODEL) + RMS_EPS)
        o_ref[...] = x_ref[...] + acc_ref[1 - slot, :, col] * rs * g_ref[...]


def _outproj(ys, w_all, layer, x, g, tm, tn):
    m = x.shape[0]
    n_row_tiles = m // tm
    yspec = pl.BlockSpec((tm, MIX), lambda i, j: (jnp.minimum(i, n_row_tiles - 1), 0))
    lagged = pl.BlockSpec((tm, tn), lambda i, j: (jnp.maximum(i - 1, 0), jnp.where(i > 0, j, 0)))
    return pl.pallas_call(
        functools.partial(_outproj_kernel, tn=tn, n_row_tiles=n_row_tiles),
        grid=(n_row_tiles + 1, D_MODEL // tn),
        in_specs=[yspec, yspec, yspec, yspec,
                  pl.BlockSpec((None, D_MODEL, tn), lambda i, j: (layer, 0, j)),
                  lagged,
                  pl.BlockSpec((1, tn), lambda i, j: (0, j))],
        out_specs=lagged,
        out_shape=jax.ShapeDtypeStruct((m, D_MODEL), F32),
        scratch_shapes=[pltpu.VMEM((2, tm, D_MODEL), F32), pltpu.VMEM((2, tm, 1), F32)],
        compiler_params=_params(("arbitrary", "arbitrary"), 56),
        name="outproj",
    )(*ys, w_all, x, g)


def _mlstm_kernel(q_ref, k_ref, v_ref, o_ref, z_ref, gate_ref, bias_ref, mn_ref, c0_ref, n0_ref, m0_ref,
                  c_alias, n_alias, m_alias, y_ref, c_ref, n_ref, m_ref, *, L, valid, first_chunk_init):
    del c_alias, n_alias, m_alias
    if first_chunk_init:
        c_ref[...] = c0_ref[...]
        n_ref[...] = n0_ref[...]
        m_ref[...] = m0_ref[...]
        return

    row = lax.broadcasted_iota(jnp.int32, (L, LANE), 0)
    row_ok = row < valid
    gates = GATE_CAP * jnp.tanh((gate_ref[...] + bias_ref[...]) / GATE_CAP)
    logf = jnp.where(row_ok, jax.nn.log_sigmoid(gates), 0.0)
    b_all = _cumsum_rows(pltpu.roll(logf, LANE - MLSTM_HEADS, axis=1))
    r_all = jnp.where(row_ok, gates - b_all, NEG)
    r_all_t = r_all.T

    tr = lax.broadcasted_iota(jnp.int32, (L, L), 0)
    tc = lax.broadcasted_iota(jnp.int32, (L, L), 1)
    tril = tc <= tr

    heads = range(MLSTM_HEADS)
    sl = [slice(h * MLSTM_DH, (h + 1) * MLSTM_DH) for h in heads]
    q = [q_ref[:, sl[h]] * (MLSTM_DH ** -0.5) for h in heads]
    k = [k_ref[:, sl[h]] for h in heads]
    v = [v_ref[:, sl[h]] for h in heads]
    c_prev = [c_ref[0, h] for h in heads]
    n_prev = [n_ref[0, h:h + 1, :] for h in heads]
    m_prev = [m_ref[0, h:h + 1, 0:1] for h in heads]
    qk = [_dot_nt(q[h], k[h]) for h in heads]
    qc = [_dot(q[h], c_prev[h]) for h in heads]
    qn = [jnp.sum(q[h] * n_prev[h], axis=1, keepdims=True) for h in heads]

    b_col = [b_all[:, h:h + 1] for h in heads]
    dmat = [jnp.where(tril, b_col[h] + r_all_t[h:h + 1, :], NEG) for h in heads]
    inter = [b_col[h] + m_prev[h] for h in heads]
    mt = [jnp.maximum(inter[h], jnp.max(dmat[h], axis=1, keepdims=True)) for h in heads]
    s = [jnp.exp(dmat[h] - mt[h]) * qk[h] for h in heads]
    sv = [_dot(s[h], v[h]) for h in heads]
    w_prev_t = [jnp.exp(inter[h] - mt[h]) for h in heads]
    den = [w_prev_t[h] * qn[h] + jnp.sum(s[h], axis=1, keepdims=True) for h in heads]
    hh = [(w_prev_t[h] * qc[h] + sv[h]) / jnp.maximum(jnp.abs(den[h]), jnp.exp(-mt[h])) for h in heads]

    b_last = [b_all[L - 1:L, h:h + 1] for h in heads]
    dec = [b_last[h] + r_all[:, h:h + 1] for h in heads]
    m_new = [jnp.maximum(b_last[h] + m_prev[h], jnp.max(dec[h], axis=0, keepdims=True)) for h in heads]
    w_prev = [jnp.exp(b_last[h] + m_prev[h] - m_new[h]) for h in heads]
    kw = [k[h] * jnp.exp(dec[h] - m_new[h]) for h in heads]
    c_new = [w_prev[h] * c_prev[h] + _dot_tn(kw[h], v[h]) for h in heads]
    n_new = [w_prev[h] * n_prev[h] + jnp.sum(kw[h], axis=0, keepdims=True) for h in heads]

    hn = [hh[h] * lax.rsqrt(jnp.mean(hh[h] * hh[h], axis=1, keepdims=True) + RMS_EPS) * mn_ref[:, sl[h]] for h in heads]
    ys = [hn[h] * jax.nn.sigmoid(o_ref[:, sl[h]]) * _silu(z_ref[:, sl[h]]) for h in heads]

    y_ref[...] = jnp.concatenate(ys, axis=1).astype(y_ref.dtype)
    for h in heads:
        c_ref[0, h] = c_new[h]
    n_ref[0] = jnp.concatenate(n_new, axis=0)
    m_ref[0] = jnp.concatenate([jnp.broadcast_to(m_new[h], (1, LANE)) for h in heads], axis=0)


def _mlstm(proj, bias, mnorm, st0, l0, acc, layer, n_seq, t_pad, L, valid, y_dtype):
    nc = t_pad // L
    wide = lambda name: pl.BlockSpec((L, MIX), lambda s, c, cb=_colblock(name): (s * nc + c, cb))
    vec = lambda n: pl.BlockSpec((1, n), lambda s, c: (0, 0))
    return _Part(
        functools.partial(_mlstm_kernel, L=L, valid=valid),
        operands=[proj, proj, proj, proj, proj, proj, bias, mnorm, *st0, *acc],
        in_specs=[wide("qa"), wide("ka"), wide("va"), wide("oa"), wide("za"),
                  pl.BlockSpec((L, LANE), lambda s, c, cb=_colblock("gate"): (s * nc + c, cb)),
                  vec(LANE), vec(MIX)] + [_state_spec(a, l0) for a in st0] + [_ANY] * len(acc),
        out_specs=[pl.BlockSpec((L, MIX), lambda s, c: (s * nc + c, 0))] + [_state_spec(a, layer) for a in acc],
        out_shape=[jax.ShapeDtypeStruct((n_seq * t_pad, MIX), y_dtype)]
        + [jax.ShapeDtypeStruct(a.shape, F32) for a in acc],
        n_alias=len(acc))


def _swa_kernel(q_ref, k_ref, v_ref, z_ref, cos_ref, sin_ref, sink_ref, kc_ref, vc_ref, k_alias, v_alias,
                y_ref, ko_ref, vo_ref, *, Lq, valid, has_cache, first_chunk_init):
    del k_alias, v_alias
    if first_chunk_init:
        ko_ref[...] = kc_ref[...]
        vo_ref[...] = vc_ref[...]
        return
    blk = pl.program_id(1)

    cos = cos_ref[...]
    sin = sin_ref[...]
    lane = lax.broadcasted_iota(jnp.int32, (Lq, LANE), 1)
    first_half = (lane % SWA_DH) < (SWA_DH // 2)

    def rope(x):
        partner = jnp.where(first_half, pltpu.roll(x, LANE - SWA_DH // 2, axis=1), pltpu.roll(x, SWA_DH // 2, axis=1))
        return x * cos + partner * sin

    k_cur = rope(k_ref[...])
    v_cur = v_ref[...]
    k_prev = ko_ref[0]
    v_prev = vo_ref[0]
    kk = jnp.concatenate([k_prev, k_cur], axis=0)
    vv = jnp.concatenate([v_prev, v_cur], axis=0)

    rows, nk = SWA_GROUP * Lq, WINDOW + Lq
    t = lax.broadcasted_iota(jnp.int32, (rows, nk), 0) & (Lq - 1)
    c = lax.broadcasted_iota(jnp.int32, (rows, nk), 1)
    prev_ok = jnp.logical_or(has_cache, blk > 0)
    mask = ((c < WINDOW) & (c > t) & prev_ok) | ((c >= WINDOW) & (c - WINDOW <= t) & (c - WINDOW < valid))

    pairs_per_group = SWA_GROUP // 2
    y_chunks = []
    for g in range(SWA_KV_HEADS):
        gsl = slice(g * SWA_DH, (g + 1) * SWA_DH)
        q_pairs = [rope(q_ref[:, (g * pairs_per_group + p) * LANE:(g * pairs_per_group + p + 1) * LANE])
                   * (SWA_DH ** -0.5) for p in range(pairs_per_group)]
        q_st = jnp.concatenate([q_pairs[h // 2][:, (h % 2) * SWA_DH:(h % 2 + 1) * SWA_DH]
                                for h in range(SWA_GROUP)], axis=0)
        sink = jnp.concatenate([jnp.broadcast_to(sink_ref[0:1, g * SWA_GROUP + h:g * SWA_GROUP + h + 1], (Lq, 1))
                                for h in range(SWA_GROUP)], axis=0)
        s = jnp.where(mask, _dot_nt(q_st, kk[:, gsl]), NEG)
        mx = jnp.maximum(jnp.max(s, axis=1, keepdims=True), sink)
        p = jnp.exp(s - mx)
        den = jnp.sum(p, axis=1, keepdims=True) + jnp.exp(sink - mx)
        o = _dot(p, vv[:, gsl]) / den
        for pr in range(pairs_per_group):
            csl = slice((g * pairs_per_group + pr) * LANE, (g * pairs_per_group + pr + 1) * LANE)
            pair = jnp.concatenate([o[(2 * pr) * Lq:(2 * pr + 1) * Lq], o[(2 * pr + 1) * Lq:(2 * pr + 2) * Lq]], axis=1)
            y_chunks.append(pair * _silu(z_ref[:, csl]))
    y_ref[...] = jnp.concatenate(y_chunks, axis=1).astype(y_ref.dtype)

    if valid == WINDOW:
        ko_ref[0] = k_cur
        vo_ref[0] = v_cur
    else:
        ko_ref[0, 0:WINDOW - valid, :] = k_prev[valid:, :]
        ko_ref[0, WINDOW - valid:WINDOW, :] = k_cur[0:valid, :]
        vo_ref[0, 0:WINDOW - valid, :] = v_prev[valid:, :]
        vo_ref[0, WINDOW - valid:WINDOW, :] = v_cur[0:valid, :]


def _swa(proj, cos, sin, sinks, st0, l0, acc, layer, n_seq, t_pad, Lq, valid, has_cache, table_per_block, y_dtype):
    nb = t_pad // Lq
    rowblk = lambda s, b: s * nb + b
    wide = lambda name: pl.BlockSpec((Lq, MIX), lambda s, b, cb=_colblock(name): (rowblk(s, b), cb))
    narrow = lambda name: pl.BlockSpec((Lq, LANE), lambda s, b, cb=_colblock(name): (rowblk(s, b), cb))
    table = pl.BlockSpec((Lq, LANE), (lambda s, b: (b, 0)) if table_per_block else (lambda s, b: (0, 0)))
    return _Part(
        functools.partial(_swa_kernel, Lq=Lq, valid=valid, has_cache=has_cache),
        operands=[proj, proj, proj, proj, cos, sin, sinks, *st0, *acc],
        in_specs=[wide("qb"), narrow("kb"), narrow("vb"), wide("zb"), table, table,
                  pl.BlockSpec((1, LANE), lambda s, b: (0, 0))] + [_state_spec(a, l0) for a in st0] + [_ANY] * len(acc),
        out_specs=[pl.BlockSpec((Lq, MIX), lambda s, b: (rowblk(s, b), 0))] + [_state_spec(a, layer) for a in acc],
        out_shape=[jax.ShapeDtypeStruct((n_seq * t_pad, MIX), y_dtype)]
        + [jax.ShapeDtypeStruct(a.shape, F32) for a in acc],
        n_alias=len(acc))


_CONV_PAD = SUBLANE


def _spread_heads(x, e):
    lane = lax.broadcasted_iota(jnp.int32, x.shape, 1)
    x = jnp.where(lane < SSD_HEADS, x, 0.0)
    hi = x.astype(BF16)
    rest = x - hi.astype(F32)
    mid = rest.astype(BF16)
    lo = (rest - mid.astype(F32)).astype(BF16)
    return _dot(hi, e) + _dot(mid, e) + _dot(lo, e)


def _ssd_kernel(xbc_ref, z_ref, dt_ref, cw_ref, cb_ref, dtb_ref, alog_ref, dskip_ref, sn_ref, e_ref, conv0_ref, s0_ref,
                conv_alias, s_alias, y_ref, convo_ref, so_ref, full_ref, *, L, valid, first_chunk_init):
    del conv_alias, s_alias
    hist = SSD_CONV - 1
    if first_chunk_init:
        so_ref[...] = s0_ref[...]
        full_ref[_CONV_PAD - hist:_CONV_PAD, :] = conv0_ref[0]
        return

    full_ref[_CONV_PAD:_CONV_PAD + L, :] = xbc_ref[...]
    full = full_ref[...]
    acc = cb_ref[...] + full[_CONV_PAD:, :] * cw_ref[hist:hist + 1, :]
    for j in range(hist):
        acc = acc + pltpu.roll(full, hist - j, axis=0)[_CONV_PAD:, :] * cw_ref[j:j + 1, :]
    xc = _silu(acc)
    new_hist = full_ref[_CONV_PAD + valid - hist:_CONV_PAD + valid, :]
    convo_ref[0] = new_hist
    full_ref[_CONV_PAD - hist:_CONV_PAD, :] = new_hist

    nbc = SSD_GROUPS * SSD_DSTATE
    row_ok = lax.broadcasted_iota(jnp.int32, (L, LANE), 0) < valid
    dtv = jnp.where(row_ok, jax.nn.softplus(dt_ref[...] + dtb_ref[...]), 0.0)
    a = dtv * (-jnp.exp(alog_ref[...]))
    cum = _cumsum_rows(a)
    cum_t = cum.T
    tr = lax.broadcasted_iota(jnp.int32, (L, L), 0)
    tc = lax.broadcasted_iota(jnp.int32, (L, L), 1)
    tril = tc <= tr

    groups, heads = range(SSD_GROUPS), range(SSD_HEADS)
    gp = SSD_HPG * SSD_DH
    bg = [xc[:, MIX + g * SSD_DSTATE:MIX + (g + 1) * SSD_DSTATE] for g in groups]
    cg = [xc[:, MIX + nbc + g * SSD_DSTATE:MIX + nbc + (g + 1) * SSD_DSTATE] for g in groups]
    st = [so_ref[0, g * SSD_HPG:(g + 1) * SSD_HPG].reshape(gp, SSD_DSTATE) for g in groups]
    cb = [_dot_nt(cg[g], bg[g]) for g in groups]
    cst = [_dot_nt(cg[g], st[g]) for g in groups]
    c_col = [cum[:, h:h + 1] for h in heads]
    c_last = [cum[L - 1:L, h:h + 1] for h in heads]
    lm = [jnp.where(tril, jnp.exp(jnp.where(tril, c_col[h] - cum_t[h:h + 1, :], 0.0)), 0.0) for h in heads]

    spread = _spread_heads(jnp.concatenate([dtv, jnp.exp(cum), jnp.exp(cum[L - 1:L, :] - cum)], axis=0), e_ref[...])
    xs = xc[:, :MIX]
    xdt = xs * spread[0:L]
    xw = xdt * spread[2 * L:3 * L]
    low_half = lax.broadcasted_iota(jnp.int32, (L, LANE), 1) < SSD_DH
    y_in = []
    for pair in range(SSD_HEADS // 2):
        g = (2 * pair) // SSD_HPG
        x_pair = xdt[:, pair * LANE:(pair + 1) * LANE]
        y_in.append(jnp.where(low_half, _dot(cb[g] * lm[2 * pair], x_pair), _dot(cb[g] * lm[2 * pair + 1], x_pair)))
    y_state = jnp.concatenate(cst, axis=1) * spread[L:2 * L]
    upd = [_dot_tn(xw[:, g * gp:(g + 1) * gp], bg[g]) for g in groups]
    for h in heads:
        g, hh = divmod(h, SSD_HPG)
        so_ref[0, h] = jnp.exp(c_last[h]) * st[g][hh * SSD_DH:(hh + 1) * SSD_DH] + upd[g][hh * SSD_DH:(hh + 1) * SSD_DH]
    yc = (jnp.concatenate(y_in, axis=1) + y_state + dskip_ref[...] * xs) * _silu(z_ref[...])
    y_ref[...] = (yc * lax.rsqrt(jnp.mean(yc * yc, axis=1, keepdims=True) + RMS_EPS) * sn_ref[...]).astype(y_ref.dtype)


def _ssd(proj, conv_w, conv_b, dt_bias, a_log, d_skip, snorm, st0, l0, acc, layer, n_seq, t_pad, L, valid, y_dtype):
    nc = t_pad // L
    rowblk = lambda s, c: s * nc + c
    vec = lambda n: pl.BlockSpec((1, n), lambda s, c: (0, 0))
    spread = (jnp.arange(MIX)[None, :] // SSD_DH == jnp.arange(LANE)[:, None]).astype(BF16)
    return _Part(
        functools.partial(_ssd_kernel, L=L, valid=valid),
        operands=[proj, proj, proj, conv_w, conv_b, dt_bias, a_log, d_skip, snorm, spread, *st0, *acc],
        in_specs=[pl.BlockSpec((L, SSD_CONV_DIM), lambda s, c, cb=_colblock("xbc"): (rowblk(s, c), cb)),
                  pl.BlockSpec((L, MIX), lambda s, c, cb=_colblock("zc"): (rowblk(s, c), cb)),
                  pl.BlockSpec((L, LANE), lambda s, c, cb=_colblock("dt"): (rowblk(s, c), cb)),
                  pl.BlockSpec((SSD_CONV, SSD_CONV_DIM), lambda s, c: (0, 0)),
                  vec(SSD_CONV_DIM), vec(LANE), vec(LANE), vec(MIX), vec(MIX),
                  pl.BlockSpec((LANE, MIX), lambda s, c: (0, 0))]
        + [_state_spec(a, l0) for a in st0] + [_ANY] * len(acc),
        out_specs=[pl.BlockSpec((L, MIX), lambda s, c: (rowblk(s, c), 0))] + [_state_spec(a, layer) for a in acc],
        out_shape=[jax.ShapeDtypeStruct((n_seq * t_pad, MIX), y_dtype)]
        + [jax.ShapeDtypeStruct(a.shape, F32) for a in acc],
        n_alias=len(acc),
        scratch=[pltpu.VMEM((_CONV_PAD + L, SSD_CONV_DIM), F32)])


_POOL_PAD = 2 * SUBLANE
assert all(w & (w - 1) == 0 for w in POOL_WINDOWS) and POOL_HIST < _POOL_PAD


def _pool_kernel(u_ref, z_ref, lin_ref, scale_ref, p0_ref, p_alias, y_ref, po_ref, full_ref, *, L, valid, n_hist,
                 first_chunk_init):
    del p_alias
    if first_chunk_init:
        full_ref[_POOL_PAD - POOL_HIST:_POOL_PAD, :] = p0_ref[0]
        return
    step = pl.program_id(1)

    full_ref[_POOL_PAD:_POOL_PAD + L, :] = u_ref[...]
    pos = n_hist + step * L + lax.broadcasted_iota(jnp.int32, (L, 1), 0)
    ys = []
    for g, w in enumerate(POOL_WINDOWS):
        sl = slice(g * POOL_GC, (g + 1) * POOL_GC)
        rows = full_ref[:, sl]
        tot, span = rows, 1
        while span < w:
            tot = tot + pltpu.roll(tot, span, axis=0)
            span *= 2
        cur = rows[_POOL_PAD:]
        cnt = jnp.minimum(pos + 1, w).astype(F32)
        d = tot[_POOL_PAD:] / cnt - cur
        ys.append(_dot(d, lin_ref[g]) * scale_ref[:, sl] * _silu(z_ref[:, sl]))
    y_ref[...] = jnp.concatenate(ys, axis=1).astype(y_ref.dtype)
    new_hist = full_ref[_POOL_PAD + valid - POOL_HIST:_POOL_PAD + valid, :]
    po_ref[0] = new_hist
    full_ref[_POOL_PAD - POOL_HIST:_POOL_PAD, :] = new_hist


def _pool(proj, lin, scale, st0, l0, acc, layer, n_seq, t_pad, L, valid, n_hist, y_dtype):
    nt = t_pad // L
    rowblk = lambda s, c: s * nt + c
    return _Part(
        functools.partial(_pool_kernel, L=L, valid=valid, n_hist=n_hist),
        operands=[proj, proj, lin, scale, *st0, *acc],
        in_specs=[pl.BlockSpec((L, MIX), lambda s, c, cb=_colblock("ud"): (rowblk(s, c), cb)),
                  pl.BlockSpec((L, MIX), lambda s, c, cb=_colblock("zd"): (rowblk(s, c), cb)),
                  pl.BlockSpec((len(POOL_WINDOWS), POOL_GC, POOL_GC), lambda s, c: (0, 0, 0)),
                  pl.BlockSpec((1, MIX), lambda s, c: (0, 0))] + [_state_spec(a, l0) for a in st0] + [_ANY] * len(acc),
        out_specs=[pl.BlockSpec((L, MIX), lambda s, c: (rowblk(s, c), 0))] + [_state_spec(a, layer) for a in acc],
        out_shape=[jax.ShapeDtypeStruct((n_seq * t_pad, MIX), y_dtype)]
        + [jax.ShapeDtypeStruct(a.shape, F32) for a in acc],
        n_alias=len(acc),
        scratch=[pltpu.VMEM((_POOL_PAD + L, MIX), F32)])


_REF_SIZES = (MIX, MIX, MIX, MIX, MIX, MLSTM_HEADS, MLSTM_HEADS,
              SWA_HEADS * SWA_DH, SWA_KV_HEADS * SWA_DH, SWA_KV_HEADS * SWA_DH, MIX,
              MIX, SSD_CONV_DIM, SSD_HEADS, MIX, MIX)
_REF_NAMES = ("qa", "ka", "va", "oa", "za", "ia", "fa", "qb", "kb", "vb", "zb", "zc", "xbc", "dt", "ud", "zd")


def _regroup_plan():
    ref_off, o = {}, 0
    for name, size in zip(_REF_NAMES, _REF_SIZES):
        ref_off[name] = o
        o += size
    src = []
    for name, (off, width) in _COL.items():
        start = ref_off["ia"] if name == "gate" else ref_off[name]
        src += [start + b * LANE for b in range(width // LANE)]
    assert len(src) == NPROJ // LANE and all(s % SUBLANE == 0 and s + LANE <= o for s in src)
    return src


_REGROUP_BLOCKS = 4


def _regroup_kernel(src_tab, *refs):
    del src_tab
    o_ref = refs[-1]
    for i, wt_ref in enumerate(refs[:-1]):
        o_ref[:, i * LANE:(i + 1) * LANE] = wt_ref[...].T.astype(BF16)


def _regroup_w_in(w_in):
    depth, d_in, _ = w_in.shape
    w_t = jnp.transpose(w_in, (0, 2, 1))
    src = jnp.asarray([s // SUBLANE for s in _regroup_plan()], jnp.int32)
    return pl.pallas_call(
        _regroup_kernel,
        grid_spec=pltpu.PrefetchScalarGridSpec(
            num_scalar_prefetch=1,
            grid=(depth, NPROJ // (_REGROUP_BLOCKS * LANE)),
            in_specs=[pl.BlockSpec((None, pl.Element(LANE), pl.Element(d_in)),
                                   lambda l, t, src, i=i: (l, src[t * _REGROUP_BLOCKS + i] * SUBLANE, 0))
                      for i in range(_REGROUP_BLOCKS)],
            out_specs=pl.BlockSpec((None, d_in, _REGROUP_BLOCKS * LANE), lambda l, t, src: (l, 0, t))),
        out_shape=jax.ShapeDtypeStruct((depth, d_in, NPROJ), BF16),
        compiler_params=_params(("arbitrary", "arbitrary"), 48),
        name="regroup",
    )(src, *([w_t] * _REGROUP_BLOCKS))


def _pad_lanes(v, n=LANE):
    return jnp.pad(v, (0, n - v.shape[0])).reshape(1, n)


def _rope_tables(pos):
    half = SWA_DH // 2
    inv = ROPE_THETA ** (-jnp.arange(half, dtype=F32) / half)
    ang = pos.astype(F32)[:, None] * inv[None, :]
    cos, sin = jnp.cos(ang), jnp.sin(ang)
    reps = LANE // SWA_DH
    return jnp.tile(jnp.concatenate([cos, cos], axis=1), (1, reps)), jnp.tile(jnp.concatenate([-sin, sin], axis=1), (1, reps))


class _Path:
    def __init__(self, n_seq, t_pad, valid, chunk, has_cache, n_hist, tm_in, tn_in, tm_out, tn_out, y_dtype):
        self.n_seq, self.t_pad, self.valid, self.chunk = n_seq, t_pad, valid, chunk
        self.has_cache, self.n_hist, self.y_dtype = has_cache, n_hist, y_dtype
        self.tm_in, self.tn_in, self.tm_out, self.tn_out = tm_in, tn_in, tm_out, tn_out


def _layer(x, layer, prm, states, l0, acc, tables, path):
    (g_pre, g_post, w_in, w_out, gate_bias, mnorm, sinks, conv_w, conv_b, dt_bias, a_log, d_skip, snorm,
     pool_lin, pool_scale) = prm
    c0, n0, m0, kc, vc, s0, conv0, p0 = states
    ca, na, ma, ka, va, sa, conva, pa = acc
    cos, sin = tables
    p = path
    proj = _inproj(x, g_pre, w_in, layer, p.tm_in, p.tn_in)
    L, valid = p.chunk, min(p.valid, p.chunk)
    parts = [
        _mlstm(proj, gate_bias, mnorm, (c0, n0, m0), l0, (ca, na, ma), layer, p.n_seq, p.t_pad, L, valid, p.y_dtype),
        _swa(proj, cos, sin, sinks, (kc, vc), l0, (ka, va), layer, p.n_seq, p.t_pad, L, valid,
             p.has_cache, not p.has_cache, p.y_dtype),
        _ssd(proj, conv_w, conv_b, dt_bias, a_log, d_skip, snorm, (conv0, s0), l0, (conva, sa), layer,
             p.n_seq, p.t_pad, L, valid, p.y_dtype),
        _pool(proj, pool_lin, pool_scale, (p0,), l0, (pa,), layer, p.n_seq, p.t_pad, L, valid, p.n_hist, p.y_dtype)]
    (ya, ca, na, ma), (yb, ka, va), (yc, conva, sa), (yd, pa) = _fused_call(
        parts, (p.n_seq, p.t_pad // L), "mixers", 48)
    x = _outproj((ya, yb, yc, yd), w_out, layer, x, g_post, p.tm_out, p.tn_out)
    return x, (ca, na, ma, ka, va, sa, conva, pa)


def kernel(x_prompt, x_sample, state_mlstm_C, state_mlstm_n, state_mlstm_m, cache_swa_k, cache_swa_v, state_ssd,
           state_ssd_conv, state_pool, norm_pre, norm_post, w_in, w_out, mlstm_b_i, mlstm_b_f, mlstm_norm,
           swa_sinks, ssd_conv_w, ssd_conv_b, ssd_dt_bias, ssd_A_log, ssd_D, ssd_norm, pool_lin, pool_scale):
    bp, seq, _ = x_prompt.shape
    bs, dec_seq, _ = x_sample.shape
    t_s = SUBLANE * pl.cdiv(dec_seq, SUBLANE)

    assert WINDOW == SSD_CHUNK == 2 * MLSTM_CHUNK
    prompt = _Path(bp, seq, seq, WINDOW, False, 0, 512, 1280, 512, 1024, BF16)
    sample = _Path(bs, t_s, dec_seq, t_s, True, POOL_HIST, bs * t_s, 1280, bs * t_s, 512, F32)

    w_in_r = _regroup_w_in(w_in)
    w_out_b = w_out.astype(BF16)

    xp = x_prompt.reshape(bp * seq, D_MODEL)
    xs = jnp.pad(x_sample, ((0, 0), (0, t_s - dec_seq), (0, 0))).reshape(bs * t_s, D_MODEL)

    tab_p = _rope_tables(jnp.arange(seq))
    tab_s = _rope_tables(PAST_LEN + jnp.arange(t_s))

    def state_shapes(layers, b):
        return ((layers, b, MLSTM_HEADS, MLSTM_DH, MLSTM_DH), (layers, b, MLSTM_HEADS, LANE), (layers, b, MLSTM_HEADS, LANE),
                (layers, b, WINDOW, LANE), (layers, b, WINDOW, LANE),
                (layers, b, SSD_HEADS, SSD_DH, SSD_DSTATE), (layers, b, SSD_CONV - 1, SSD_CONV_DIM),
                (layers, b, POOL_HIST, MIX))

    p_states0 = tuple(jnp.zeros(s, F32) for s in state_shapes(1, bp))
    s_states0 = (state_mlstm_C, state_mlstm_n,
                 jnp.broadcast_to(state_mlstm_m[..., None], (DEPTH, bs, MLSTM_HEADS, LANE)),
                 cache_swa_k.reshape(DEPTH, bs, WINDOW, LANE), cache_swa_v.reshape(DEPTH, bs, WINDOW, LANE),
                 state_ssd, state_ssd_conv, state_pool)
    p_acc = tuple(jnp.zeros(s, F32) for s in state_shapes(DEPTH, bp))
    s_acc = tuple(jnp.zeros(s, F32) for s in state_shapes(DEPTH, bs))

    for l in range(DEPTH):
        gate_bias = _pad_lanes(jnp.concatenate([mlstm_b_i[l], mlstm_b_f[l]]))
        prm = (norm_pre[l].reshape(1, D_MODEL), norm_post[l].reshape(1, D_MODEL), w_in_r, w_out_b,
               gate_bias, mlstm_norm[l].reshape(1, MIX), _pad_lanes(swa_sinks[l]),
               ssd_conv_w[l], ssd_conv_b[l].reshape(1, SSD_CONV_DIM), _pad_lanes(ssd_dt_bias[l]),
               _pad_lanes(ssd_A_log[l]), jnp.repeat(ssd_D[l], SSD_DH).reshape(1, MIX), ssd_norm[l].reshape(1, MIX),
               pool_lin[l], pool_scale[l].reshape(1, MIX))
        xp, p_acc = _layer(xp, l, prm, p_states0, 0, p_acc, tab_p, prompt)
        xs, s_acc = _layer(xs, l, prm, s_states0, l, s_acc, tab_s, sample)

    def finish(acc, b):
        c, n, m, k, v, s, conv, pool = acc
        kv_shape = (DEPTH, b, WINDOW, SWA_KV_HEADS, SWA_DH)
        return c, n, m[..., 0], k.reshape(kv_shape), v.reshape(kv_shape), s, conv, pool

    y_prompt = xp.reshape(bp, seq, D_MODEL)
    y_sample = xs.reshape(bs, t_s, D_MODEL)[:, :dec_seq]
    return (y_prompt, y_sample) + finish(p_acc, bp) + finish(s_acc, bs)
```

```python
import functools
import math

import jax
import jax.numpy as jnp
from jax import lax
from jax.experimental import pallas as pl
from jax.experimental.pallas import tpu as pltpu

F32 = jnp.float32
BF16 = jnp.bfloat16

D_MODEL = 4096
DEPTH = 4
PAST_LEN = 8192
MIX = D_MODEL // 4
MLSTM_DH = 128
MLSTM_HEADS = MIX // MLSTM_DH
MLSTM_CHUNK = 64
GATE_CAP = 15.0
SWA_DH = 64
SWA_HEADS = MIX // SWA_DH
SWA_KV_HEADS = 2
SWA_GROUP = SWA_HEADS // SWA_KV_HEADS
WINDOW = 128
ROPE_THETA = 10000.0
SSD_DH = 64
SSD_HEADS = MIX // SSD_DH
SSD_GROUPS = 4
SSD_HPG = SSD_HEADS // SSD_GROUPS
SSD_DSTATE = 128
SSD_CONV = 4
SSD_CONV_DIM = MIX + 2 * SSD_GROUPS * SSD_DSTATE
SSD_CHUNK = 128
POOL_WINDOWS = (2, 4, 8, 16)
POOL_GC = MIX // len(POOL_WINDOWS)
POOL_HIST = max(POOL_WINDOWS) - 1
RMS_EPS = 1e-6

LANE = 128
SUBLANE = 8
NEG = -1e30

_COL = {}
_off = 0
for _name, _width in (("qa", MIX), ("ka", MIX), ("va", MIX), ("oa", MIX), ("za", MIX),
                      ("qb", MIX), ("zb", MIX), ("zc", MIX), ("xbc", SSD_CONV_DIM),
                      ("ud", MIX), ("zd", MIX),
                      ("gate", LANE), ("kb", LANE), ("vb", LANE), ("dt", LANE)):
    _COL[_name] = (_off, _width)
    _off += _width
NPROJ = _off


def _colblock(name):
    off, width = _COL[name]
    assert off % width == 0
    return off // width


def _silu(x):
    return x * jax.nn.sigmoid(x)


def _dot(a, b):
    return jnp.dot(a, b, preferred_element_type=F32)


def _dot_nt(a, b):
    return lax.dot_general(a, b, (((1,), (1,)), ((), ())), preferred_element_type=F32)


def _dot_tn(a, b):
    return lax.dot_general(a, b, (((0,), (0,)), ((), ())), preferred_element_type=F32)


def _cumsum_rows(x):
    n = x.shape[0]
    r = lax.broadcasted_iota(jnp.int32, (n, n), 0)
    c = lax.broadcasted_iota(jnp.int32, (n, n), 1)
    tri = (c <= r).astype(F32)
    return jnp.dot(tri, x, preferred_element_type=F32, precision=lax.Precision.HIGHEST)


def _params(sem, vmem_mb):
    return pltpu.CompilerParams(dimension_semantics=sem, vmem_limit_bytes=vmem_mb * 1024 * 1024)


_ANY = pl.BlockSpec(memory_space=pl.ANY)


class _Part:
    def __init__(self, body, operands, in_specs, out_specs, out_shape, n_alias, scratch=()):
        self.body, self.operands, self.in_specs, self.out_specs = body, operands, in_specs, out_specs
        self.out_shape, self.n_alias, self.scratch = out_shape, n_alias, list(scratch)


def _fused_call(parts, grid, name, vmem_mb):
    n_in = [len(p.operands) for p in parts]
    n_out = [len(p.out_shape) for p in parts]
    n_scr = [len(p.scratch) for p in parts]

    def body(*refs):
        ins, outs, scr = refs[:sum(n_in)], refs[sum(n_in):sum(n_in) + sum(n_out)], refs[sum(n_in) + sum(n_out):]

        def run(first_chunk_init):
            i = o = s = 0
            for k, p in enumerate(parts):
                p.body(*ins[i:i + n_in[k]], *outs[o:o + n_out[k]], *scr[s:s + n_scr[k]],
                       first_chunk_init=first_chunk_init)
                i, o, s = i + n_in[k], o + n_out[k], s + n_scr[k]

        pl.when(pl.program_id(len(grid) - 1) == 0)(functools.partial(run, True))
        run(False)

    aliases, i, o = {}, 0, 0
    for k, p in enumerate(parts):
        for a in range(p.n_alias):
            aliases[i + n_in[k] - p.n_alias + a] = o + 1 + a
        i, o = i + n_in[k], o + n_out[k]
    flat = pl.pallas_call(
        body,
        grid=grid,
        in_specs=[s for p in parts for s in p.in_specs],
        out_specs=[s for p in parts for s in p.out_specs],
        out_shape=[s for p in parts for s in p.out_shape],
        scratch_shapes=[s for p in parts for s in p.scratch],
        input_output_aliases=aliases,
        compiler_params=_params(("arbitrary",) * len(grid), vmem_mb),
        name=name,
    )(*[x for p in parts for x in p.operands])
    out, o = [], 0
    for k in range(len(parts)):
        out.append(list(flat[o:o + n_out[k]]))
        o += n_out[k]
    return out


def _state_spec(stacked, layer):
    tail = stacked.shape[2:]
    return pl.BlockSpec((None, 1) + tail, lambda s, c: (layer, s) + (0,) * len(tail))


_NORM_STEPS = 8


def _inproj_kernel(x_ref, g_ref, w_ref, o_ref, h_ref):
    p = pl.program_id(0)
    j = pl.program_id(1)
    rows = x_ref.shape[0]

    def normalise_slice():
        rs = pl.ds(pl.multiple_of(jnp.minimum(j, _NORM_STEPS - 1) * rows, rows), rows)
        x = x_ref[...]
        ms = jnp.mean(x * x, axis=-1, keepdims=True)
        h_ref[p % 2, rs, :] = (x * lax.rsqrt(ms + RMS_EPS) * g_ref[...]).astype(BF16)

    pl.when(p == 0)(normalise_slice)

    @pl.when(p > 0)
    def _():
        normalise_slice()
        o_ref[...] = _dot(h_ref[(p - 1) % 2], w_ref[...])


def _inproj(x, g, w_all, layer, tm, tn):
    m = x.shape[0]
    n_row_tiles, n_col = m // tm, NPROJ // tn
    assert n_col >= _NORM_STEPS and tm % (_NORM_STEPS * 2 * SUBLANE) == 0
    col = lambda p, j: jnp.where(p > 0, j, 0)
    x_slice = lambda p, j: jnp.minimum(p, n_row_tiles - 1) * _NORM_STEPS + jnp.minimum(j, _NORM_STEPS - 1)
    return pl.pallas_call(
        _inproj_kernel,
        grid=(n_row_tiles + 1, n_col),
        in_specs=[pl.BlockSpec((tm // _NORM_STEPS, D_MODEL), lambda p, j: (x_slice(p, j), 0)),
                  pl.BlockSpec((1, D_MODEL), lambda p, j: (0, 0)),
                  pl.BlockSpec((None, D_MODEL, tn), lambda p, j: (layer, 0, col(p, j)))],
        out_specs=pl.BlockSpec((tm, tn), lambda p, j: (jnp.maximum(p - 1, 0), col(p, j))),
        out_shape=jax.ShapeDtypeStruct((m, NPROJ), F32),
        scratch_shapes=[pltpu.VMEM((2, tm, D_MODEL), BF16)],
        compiler_params=_params(("arbitrary", "arbitrary"), 58),
        name="inproj",
    )(x, g, w_all)


def _outproj_kernel(ya_ref, yb_ref, yc_ref, yd_ref, w_ref, x_ref, g_ref, o_ref, acc_ref, ss_ref, *, tn, n_row_tiles):
    i = pl.program_id(0)
    j = pl.program_id(1)
    slot = i % 2
    col = pl.ds(pl.multiple_of(j * tn, tn), tn)

    @pl.when(i < n_row_tiles)
    def _():
        out = None
        for part, r in enumerate((ya_ref, yb_ref, yc_ref, yd_ref)):
            term = _dot(r[...].astype(BF16), w_ref[part * MIX:(part + 1) * MIX, :])
            out = term if out is None else out + term
        acc_ref[slot, :, col] = out
        sq = jnp.sum(out * out, axis=1, keepdims=True)
        ss_ref[slot] = jnp.where(j == 0, sq, ss_ref[slot] + sq)

    @pl.when(i > 0)
    def _():
        rs = lax.rsqrt(ss_ref[1 - slot] * (1.0 / D_MODEL) + RMS_EPS)
        o_ref[...] = x_ref[...] + acc_ref[1 - slot, :, col] * rs * g_ref[...]


def _outproj(ys, w_all, layer, x, g, tm, tn):
    m = x.shape[0]
    n_row_tiles = m // tm
    yspec = pl.BlockSpec((tm, MIX), lambda i, j: (jnp.minimum(i, n_row_tiles - 1), 0))
    lagged = pl.BlockSpec((tm, tn), lambda i, j: (jnp.maximum(i - 1, 0), jnp.where(i > 0, j, 0)))
    return pl.pallas_call(
        functools.partial(_outproj_kernel, tn=tn, n_row_tiles=n_row_tiles),
        grid=(n_row_tiles + 1, D_MODEL // tn),
        in_specs=[yspec, yspec, yspec, yspec,
                  pl.BlockSpec((None, D_MODEL, tn), lambda i, j: (layer, 0, j)),
                  lagged,
                  pl.BlockSpec((1, tn), lambda i, j: (0, j))],
        out_specs=lagged,
        out_shape=jax.ShapeDtypeStruct((m, D_MODEL), F32),
        scratch_shapes=[pltpu.VMEM((2, tm, D_MODEL), F32), pltpu.VMEM((2, tm, 1), F32)],
        compiler_params=_params(("arbitrary", "arbitrary"), 56),
        name="outproj",
    )(*ys, w_all, x, g)


def _mlstm_kernel(q_ref, k_ref, v_ref, o_ref, z_ref, gate_ref, bias_ref, mn_ref, c0_ref, n0_ref, m0_ref,
                  c_alias, n_alias, m_alias, y_ref, c_ref, n_ref, m_ref, *, L, valid, first_chunk_init):
    del c_alias, n_alias, m_alias
    if first_chunk_init:
        c_ref[...] = c0_ref[...]
        n_ref[...] = n0_ref[...]
        m_ref[...] = m0_ref[...]
        return

    row = lax.broadcasted_iota(jnp.int32, (L, LANE), 0)
    row_ok = row < valid
    gates = GATE_CAP * jnp.tanh((gate_ref[...] + bias_ref[...]) / GATE_CAP)
    logf = jnp.where(row_ok, jax.nn.log_sigmoid(gates), 0.0)
    b_all = _cumsum_rows(pltpu.roll(logf, LANE - MLSTM_HEADS, axis=1))
    r_all = jnp.where(row_ok, gates - b_all, NEG)
    r_all_t = r_all.T

    tr = lax.broadcasted_iota(jnp.int32, (L, L), 0)
    tc = lax.broadcasted_iota(jnp.int32, (L, L), 1)
    tril = tc <= tr

    heads = range(MLSTM_HEADS)
    sl = [slice(h * MLSTM_DH, (h + 1) * MLSTM_DH) for h in heads]
    q = [q_ref[:, sl[h]] * (MLSTM_DH ** -0.5) for h in heads]
    k = [k_ref[:, sl[h]] for h in heads]
    v = [v_ref[:, sl[h]] for h in heads]
    c_prev = [c_ref[0, h] for h in heads]
    n_prev = [n_ref[0, h:h + 1, :] for h in heads]
    m_prev = [m_ref[0, h:h + 1, 0:1] for h in heads]
    qk = [_dot_nt(q[h], k[h]) for h in heads]
    qc = [_dot(q[h], c_prev[h]) for h in heads]
    qn = [jnp.sum(q[h] * n_prev[h], axis=1, keepdims=True) for h in heads]

    b_col = [b_all[:, h:h + 1] for h in heads]
    dmat = [jnp.where(tril, b_col[h] + r_all_t[h:h + 1, :], NEG) for h in heads]
    inter = [b_col[h] + m_prev[h] for h in heads]
    mt = [jnp.maximum(inter[h], jnp.max(dmat[h], axis=1, keepdims=True)) for h in heads]
    s = [jnp.exp(dmat[h] - mt[h]) * qk[h] for h in heads]
    sv = [_dot(s[h], v[h]) for h in heads]
    w_prev_t = [jnp.exp(inter[h] - mt[h]) for h in heads]
    den = [w_prev_t[h] * qn[h] + jnp.sum(s[h], axis=1, keepdims=True) for h in heads]
    hh = [(w_prev_t[h] * qc[h] + sv[h]) / jnp.maximum(jnp.abs(den[h]), jnp.exp(-mt[h])) for h in heads]

    b_last = [b_all[L - 1:L, h:h + 1] for h in heads]
    dec = [b_last[h] + r_all[:, h:h + 1] for h in heads]
    m_new = [jnp.maximum(b_last[h] + m_prev[h], jnp.max(dec[h], axis=0, keepdims=True)) for h in heads]
    w_prev = [jnp.exp(b_last[h] + m_prev[h] - m_new[h]) for h in heads]
    kw = [k[h] * jnp.exp(dec[h] - m_new[h]) for h in heads]
    c_new = [w_prev[h] * c_prev[h] + _dot_tn(kw[h], v[h]) for h in heads]
    n_new = [w_prev[h] * n_prev[h] + jnp.sum(kw[h], axis=0, keepdims=True) for h in heads]

    hn = [hh[h] * lax.rsqrt(jnp.mean(hh[h] * hh[h], axis=1, keepdims=True) + RMS_EPS) * mn_ref[:, sl[h]] for h in heads]
    ys = [hn[h] * jax.nn.sigmoid(o_ref[:, sl[h]]) * _silu(z_ref[:, sl[h]]) for h in heads]

    y_ref[...] = jnp.concatenate(ys, axis=1).astype(y_ref.dtype)
    for h in heads:
        c_ref[0, h] = c_new[h]
    n_ref[0] = jnp.concatenate(n_new, axis=0)
    m_ref[0] = jnp.concatenate([jnp.broadcast_to(m_new[h], (1, LANE)) for h in heads], axis=0)


def _mlstm(proj, bias, mnorm, st0, l0, acc, layer, n_seq, t_pad, L, valid, y_dtype):
    nc = t_pad // L
    wide = lambda name: pl.BlockSpec((L, MIX), lambda s, c, cb=_colblock(name): (s * nc + c, cb))
    vec = lambda n: pl.BlockSpec((1, n), lambda s, c: (0, 0))
    return _Part(
        functools.partial(_mlstm_kernel, L=L, valid=valid),
        operands=[proj, proj, proj, proj, proj, proj, bias, mnorm, *st0, *acc],
        in_specs=[wide("qa"), wide("ka"), wide("va"), wide("oa"), wide("za"),
                  pl.BlockSpec((L, LANE), lambda s, c, cb=_colblock("gate"): (s * nc + c, cb)),
                  vec(LANE), vec(MIX)] + [_state_spec(a, l0) for a in st0] + [_ANY] * len(acc),
        out_specs=[pl.BlockSpec((L, MIX), lambda s, c: (s * nc + c, 0))] + [_state_spec(a, layer) for a in acc],
        out_shape=[jax.ShapeDtypeStruct((n_seq * t_pad, MIX), y_dtype)]
        + [jax.ShapeDtypeStruct(a.shape, F32) for a in acc],
        n_alias=len(acc))


def _swa_kernel(q_ref, k_ref, v_ref, z_ref, cos_ref, sin_ref, sink_ref, kc_ref, vc_ref, k_alias, v_alias,
                y_ref, ko_ref, vo_ref, *, Lq, valid, has_cache, first_chunk_init):
    del k_alias, v_alias
    if first_chunk_init:
        ko_ref[...] = kc_ref[...]
        vo_ref[...] = vc_ref[...]
        return
    blk = pl.program_id(1)

    cos = cos_ref[...]
    sin = sin_ref[...]
    lane = lax.broadcasted_iota(jnp.int32, (Lq, LANE), 1)
    first_half = (lane % SWA_DH) < (SWA_DH // 2)

    def rope(x):
        partner = jnp.where(first_half, pltpu.roll(x, LANE - SWA_DH // 2, axis=1), pltpu.roll(x, SWA_DH // 2, axis=1))
        return x * cos + partner * sin

    k_cur = rope(k_ref[...])
    v_cur = v_ref[...]
    k_prev = ko_ref[0]
    v_prev = vo_ref[0]
    kk = jnp.concatenate([k_prev, k_cur], axis=0)
    vv = jnp.concatenate([v_prev, v_cur], axis=0)

    rows, nk = SWA_GROUP * Lq, WINDOW + Lq
    t = lax.broadcasted_iota(jnp.int32, (rows, nk), 0) & (Lq - 1)
    c = lax.broadcasted_iota(jnp.int32, (rows, nk), 1)
    prev_ok = jnp.logical_or(has_cache, blk > 0)
    mask = ((c < WINDOW) & (c > t) & prev_ok) | ((c >= WINDOW) & (c - WINDOW <= t) & (c - WINDOW < valid))

    pairs_per_group = SWA_GROUP // 2
    y_chunks = []
    for g in range(SWA_KV_HEADS):
        gsl = slice(g * SWA_DH, (g + 1) * SWA_DH)
        q_pairs = [rope(q_ref[:, (g * pairs_per_group + p) * LANE:(g * pairs_per_group + p + 1) * LANE])
                   * (SWA_DH ** -0.5) for p in range(pairs_per_group)]
        q_st = jnp.concatenate([q_pairs[h // 2][:, (h % 2) * SWA_DH:(h % 2 + 1) * SWA_DH]
                                for h in range(SWA_GROUP)], axis=0)
        sink = jnp.concatenate([jnp.broadcast_to(sink_ref[0:1, g * SWA_GROUP + h:g * SWA_GROUP + h + 1], (Lq, 1))
                                for h in range(SWA_GROUP)], axis=0)
        s = jnp.where(mask, _dot_nt(q_st, kk[:, gsl]), NEG)
        mx = jnp.maximum(jnp.max(s, axis=1, keepdims=True), sink)
        p = jnp.exp(s - mx)
        den = jnp.sum(p, axis=1, keepdims=True) + jnp.exp(sink - mx)
        o = _dot(p, vv[:, gsl]) / den
        for pr in range(pairs_per_group):
            csl = slice((g * pairs_per_group + pr) * LANE, (g * pairs_per_group + pr + 1) * LANE)
            pair = jnp.concatenate([o[(2 * pr) * Lq:(2 * pr + 1) * Lq], o[(2 * pr + 1) * Lq:(2 * pr + 2) * Lq]], axis=1)
            y_chunks.append(pair * _silu(z_ref[:, csl]))
    y_ref[...] = jnp.concatenate(y_chunks, axis=1).astype(y_ref.dtype)

    if valid == WINDOW:
        ko_ref[0] = k_cur
        vo_ref[0] = v_cur
    else:
        ko_ref[0, 0:WINDOW - valid, :] = k_prev[valid:, :]
        ko_ref[0, WINDOW - valid:WINDOW, :] = k_cur[0:valid, :]
        vo_ref[0, 0:WINDOW - valid, :] = v_prev[valid:, :]
        vo_ref[0, WINDOW - valid:WINDOW, :] = v_cur[0:valid, :]


def _swa(proj, cos, sin, sinks, st0, l0, acc, layer, n_seq, t_pad, Lq, valid, has_cache, table_per_block, y_dtype):
    nb = t_pad // Lq
    rowblk = lambda s, b: s * nb + b
    wide = lambda name: pl.BlockSpec((Lq, MIX), lambda s, b, cb=_colblock(name): (rowblk(s, b), cb))
    narrow = lambda name: pl.BlockSpec((Lq, LANE), lambda s, b, cb=_colblock(name): (rowblk(s, b), cb))
    table = pl.BlockSpec((Lq, LANE), (lambda s, b: (b, 0)) if table_per_block else (lambda s, b: (0, 0)))
    return _Part(
        functools.partial(_swa_kernel, Lq=Lq, valid=valid, has_cache=has_cache),
        operands=[proj, proj, proj, proj, cos, sin, sinks, *st0, *acc],
        in_specs=[wide("qb"), narrow("kb"), narrow("vb"), wide("zb"), table, table,
                  pl.BlockSpec((1, LANE), lambda s, b: (0, 0))] + [_state_spec(a, l0) for a in st0] + [_ANY] * len(acc),
        out_specs=[pl.BlockSpec((Lq, MIX), lambda s, b: (rowblk(s, b), 0))] + [_state_spec(a, layer) for a in acc],
        out_shape=[jax.ShapeDtypeStruct((n_seq * t_pad, MIX), y_dtype)]
        + [jax.ShapeDtypeStruct(a.shape, F32) for a in acc],
        n_alias=len(acc))


_CONV_PAD = SUBLANE


def _spread_heads(x, e):
    lane = lax.broadcasted_iota(jnp.int32, x.shape, 1)
    x = jnp.where(lane < SSD_HEADS, x, 0.0)
    hi = x.astype(BF16)
    rest = x - hi.astype(F32)
    mid = rest.astype(BF16)
    lo = (rest - mid.astype(F32)).astype(BF16)
    return _dot(hi, e) + _dot(mid, e) + _dot(lo, e)


def _ssd_kernel(xbc_ref, z_ref, dt_ref, cw_ref, cb_ref, dtb_ref, alog_ref, dskip_ref, sn_ref, e_ref, conv0_ref, s0_ref,
                conv_alias, s_alias, y_ref, convo_ref, so_ref, full_ref, *, L, valid, first_chunk_init):
    del conv_alias, s_alias
    hist = SSD_CONV - 1
    if first_chunk_init:
        so_ref[...] = s0_ref[...]
        full_ref[_CONV_PAD - hist:_CONV_PAD, :] = conv0_ref[0]
        return

    full_ref[_CONV_PAD:_CONV_PAD + L, :] = xbc_ref[...]
    full = full_ref[...]
    acc = cb_ref[...] + full[_CONV_PAD:, :] * cw_ref[hist:hist + 1, :]
    for j in range(hist):
        acc = acc + pltpu.roll(full, hist - j, axis=0)[_CONV_PAD:, :] * cw_ref[j:j + 1, :]
    xc = _silu(acc)
    new_hist = full_ref[_CONV_PAD + valid - hist:_CONV_PAD + valid, :]
    convo_ref[0] = new_hist
    full_ref[_CONV_PAD - hist:_CONV_PAD, :] = new_hist

    nbc = SSD_GROUPS * SSD_DSTATE
    row_ok = lax.broadcasted_iota(jnp.int32, (L, LANE), 0) < valid
    dtv = jnp.where(row_ok, jax.nn.softplus(dt_ref[...] + dtb_ref[...]), 0.0)
    a = dtv * (-jnp.exp(alog_ref[...]))
    cum = _cumsum_rows(a)
    cum_t = cum.T
    tr = lax.broadcasted_iota(jnp.int32, (L, L), 0)
    tc = lax.broadcasted_iota(jnp.int32, (L, L), 1)
    tril = tc <= tr

    groups, heads = range(SSD_GROUPS), range(SSD_HEADS)
    gp = SSD_HPG * SSD_DH
    bg = [xc[:, MIX + g * SSD_DSTATE:MIX + (g + 1) * SSD_DSTATE] for g in groups]
    cg = [xc[:, MIX + nbc + g * SSD_DSTATE:MIX + nbc + (g + 1) * SSD_DSTATE] for g in groups]
    st = [so_ref[0, g * SSD_HPG:(g + 1) * SSD_HPG].reshape(gp, SSD_DSTATE) for g in groups]
    cb = [_dot_nt(cg[g], bg[g]) for g in groups]
    cst = [_dot_nt(cg[g], st[g]) for g in groups]
    c_col = [cum[:, h:h + 1] for h in heads]
    c_last = [cum[L - 1:L, h:h + 1] for h in heads]
    lm = [jnp.where(tril, jnp.exp(jnp.where(tril, c_col[h] - cum_t[h:h + 1, :], 0.0)), 0.0) for h in heads]

    spread = _spread_heads(jnp.concatenate([dtv, jnp.exp(cum), jnp.exp(cum[L - 1:L, :] - cum)], axis=0), e_ref[...])
    xs = xc[:, :MIX]
    xdt = xs * spread[0:L]
    xw = xdt * spread[2 * L:3 * L]
    low_half = lax.broadcasted_iota(jnp.int32, (L, LANE), 1) < SSD_DH
    y_in = []
    for pair in range(SSD_HEADS // 2):
        g = (2 * pair) // SSD_HPG
        x_pair = xdt[:, pair * LANE:(pair + 1) * LANE]
        y_in.append(jnp.where(low_half, _dot(cb[g] * lm[2 * pair], x_pair), _dot(cb[g] * lm[2 * pair + 1], x_pair)))
    y_state = jnp.concatenate(cst, axis=1) * spread[L:2 * L]
    upd = [_dot_tn(xw[:, g * gp:(g + 1) * gp], bg[g]) for g in groups]
    for h in heads:
        g, hh = divmod(h, SSD_HPG)
        so_ref[0, h] = jnp.exp(c_last[h]) * st[g][hh * SSD_DH:(hh + 1) * SSD_DH] + upd[g][hh * SSD_DH:(hh + 1) * SSD_DH]
    yc = (jnp.concatenate(y_in, axis=1) + y_state + dskip_ref[...] * xs) * _silu(z_ref[...])
    y_ref[...] = (yc * lax.rsqrt(jnp.mean(yc * yc, axis=1, keepdims=True) + RMS_EPS) * sn_ref[...]).astype(y_ref.dtype)


def _ssd(proj, conv_w, conv_b, dt_bias, a_log, d_skip, snorm, st0, l0, acc, layer, n_seq, t_pad, L, valid, y_dtype):
    nc = t_pad // L
    rowblk = lambda s, c: s * nc + c
    vec = lambda n: pl.BlockSpec((1, n), lambda s, c: (0, 0))
    spread = (jnp.arange(MIX)[None, :] // SSD_DH == jnp.arange(LANE)[:, None]).astype(BF16)
    return _Part(
        functools.partial(_ssd_kernel, L=L, valid=valid),
        operands=[proj, proj, proj, conv_w, conv_b, dt_bias, a_log, d_skip, snorm, spread, *st0, *acc],
        in_specs=[pl.BlockSpec((L, SSD_CONV_DIM), lambda s, c, cb=_colblock("xbc"): (rowblk(s, c), cb)),
                  pl.BlockSpec((L, MIX), lambda s, c, cb=_colblock("zc"): (rowblk(s, c), cb)),
                  pl.BlockSpec((L, LANE), lambda s, c, cb=_colblock("dt"): (rowblk(s, c), cb)),
                  pl.BlockSpec((SSD_CONV, SSD_CONV_DIM), lambda s, c: (0, 0)),
                  vec(SSD_CONV_DIM), vec(LANE), vec(LANE), vec(MIX), vec(MIX),
                  pl.BlockSpec((LANE, MIX), lambda s, c: (0, 0))]
        + [_state_spec(a, l0) for a in st0] + [_ANY] * len(acc),
        out_specs=[pl.BlockSpec((L, MIX), lambda s, c: (rowblk(s, c), 0))] + [_state_spec(a, layer) for a in acc],
        out_shape=[jax.ShapeDtypeStruct((n_seq * t_pad, MIX), y_dtype)]
        + [jax.ShapeDtypeStruct(a.shape, F32) for a in acc],
        n_alias=len(acc),
        scratch=[pltpu.VMEM((_CONV_PAD + L, SSD_CONV_DIM), F32)])


_POOL_PAD = 2 * SUBLANE
assert all(w & (w - 1) == 0 for w in POOL_WINDOWS) and POOL_HIST < _POOL_PAD


def _pool_kernel(u_ref, z_ref, lin_ref, scale_ref, p0_ref, p_alias, y_ref, po_ref, full_ref, *, L, valid, n_hist,
                 first_chunk_init):
    del p_alias
    if first_chunk_init:
        full_ref[_POOL_PAD - POOL_HIST:_POOL_PAD, :] = p0_ref[0]
        return
    step = pl.program_id(1)

    full_ref[_POOL_PAD:_POOL_PAD + L, :] = u_ref[...]
    pos = n_hist + step * L + lax.broadcasted_iota(jnp.int32, (L, 1), 0)
    ys = []
    for g, w in enumerate(POOL_WINDOWS):
        sl = slice(g * POOL_GC, (g + 1) * POOL_GC)
        rows = full_ref[:, sl]
        tot, span = rows, 1
        while span < w:
            tot = tot + pltpu.roll(tot, span, axis=0)
            span *= 2
        cur = rows[_POOL_PAD:]
        cnt = jnp.minimum(pos + 1, w).astype(F32)
        d = tot[_POOL_PAD:] / cnt - cur
        ys.append(_dot(d, lin_ref[g]) * scale_ref[:, sl] * _silu(z_ref[:, sl]))
    y_ref[...] = jnp.concatenate(ys, axis=1).astype(y_ref.dtype)
    new_hist = full_ref[_POOL_PAD + valid - POOL_HIST:_POOL_PAD + valid, :]
    po_ref[0] = new_hist
    full_ref[_POOL_PAD - POOL_HIST:_POOL_PAD, :] = new_hist


def _pool(proj, lin, scale, st0, l0, acc, layer, n_seq, t_pad, L, valid, n_hist, y_dtype):
    nt = t_pad // L
    rowblk = lambda s, c: s * nt + c
    return _Part(
        functools.partial(_pool_kernel, L=L, valid=valid, n_hist=n_hist),
        operands=[proj, proj, lin, scale, *st0, *acc],
        in_specs=[pl.BlockSpec((L, MIX), lambda s, c, cb=_colblock("ud"): (rowblk(s, c), cb)),
                  pl.BlockSpec((L, MIX), lambda s, c, cb=_colblock("zd"): (rowblk(s, c), cb)),
                  pl.BlockSpec((len(POOL_WINDOWS), POOL_GC, POOL_GC), lambda s, c: (0, 0, 0)),
                  pl.BlockSpec((1, MIX), lambda s, c: (0, 0))] + [_state_spec(a, l0) for a in st0] + [_ANY] * len(acc),
        out_specs=[pl.BlockSpec((L, MIX), lambda s, c: (rowblk(s, c), 0))] + [_state_spec(a, layer) for a in acc],
        out_shape=[jax.ShapeDtypeStruct((n_seq * t_pad, MIX), y_dtype)]
        + [jax.ShapeDtypeStruct(a.shape, F32) for a in acc],
        n_alias=len(acc),
        scratch=[pltpu.VMEM((_POOL_PAD + L, MIX), F32)])


_REF_SIZES = (MIX, MIX, MIX, MIX, MIX, MLSTM_HEADS, MLSTM_HEADS,
              SWA_HEADS * SWA_DH, SWA_KV_HEADS * SWA_DH, SWA_KV_HEADS * SWA_DH, MIX,
              MIX, SSD_CONV_DIM, SSD_HEADS, MIX, MIX)
_REF_NAMES = ("qa", "ka", "va", "oa", "za", "ia", "fa", "qb", "kb", "vb", "zb", "zc", "xbc", "dt", "ud", "zd")


def _regroup_plan():
    ref_off, o = {}, 0
    for name, size in zip(_REF_NAMES, _REF_SIZES):
        ref_off[name] = o
        o += size
    src = []
    for name, (off, width) in _COL.items():
        start = ref_off["ia"] if name == "gate" else ref_off[name]
        src += [start + b * LANE for b in range(width // LANE)]
    assert len(src) == NPROJ // LANE and all(s % SUBLANE == 0 and s + LANE <= o for s in src)
    return src


_REGROUP_BLOCKS = 4


def _regroup_kernel(src_tab, *refs):
    del src_tab
    o_ref = refs[-1]
    for i, wt_ref in enumerate(refs[:-1]):
        o_ref[:, i * LANE:(i + 1) * LANE] = wt_ref[...].T.astype(BF16)


def _regroup_w_in(w_in):
    depth, d_in, _ = w_in.shape
    w_t = jnp.transpose(w_in, (0, 2, 1))
    src = jnp.asarray([s // SUBLANE for s in _regroup_plan()], jnp.int32)
    return pl.pallas_call(
        _regroup_kernel,
        grid_spec=pltpu.PrefetchScalarGridSpec(
            num_scalar_prefetch=1,
            grid=(depth, NPROJ // (_REGROUP_BLOCKS * LANE)),
            in_specs=[pl.BlockSpec((None, pl.Element(LANE), pl.Element(d_in)),
                                   lambda l, t, src, i=i: (l, src[t * _REGROUP_BLOCKS + i] * SUBLANE, 0))
                      for i in range(_REGROUP_BLOCKS)],
            out_specs=pl.BlockSpec((None, d_in, _REGROUP_BLOCKS * LANE), lambda l, t, src: (l, 0, t))),
        out_shape=jax.ShapeDtypeStruct((depth, d_in, NPROJ), BF16),
        compiler_params=_params(("arbitrary", "arbitrary"), 48),
        name="regroup",
    )(src, *([w_t] * _REGROUP_BLOCKS))


def _pad_lanes(v, n=LANE):
    return jnp.pad(v, (0, n - v.shape[0])).reshape(1, n)


def _rope_tables(pos):
    half = SWA_DH // 2
    inv = ROPE_THETA ** (-jnp.arange(half, dtype=F32) / half)
    ang = pos.astype(F32)[:, None] * inv[None, :]
    cos, sin = jnp.cos(ang), jnp.sin(ang)
    reps = LANE // SWA_DH
    return jnp.tile(jnp.concatenate([cos, cos], axis=1), (1, reps)), jnp.tile(jnp.concatenate([-sin, sin], axis=1), (1, reps))


class _Path:
    def __init__(self, n_seq, t_pad, valid, chunk, has_cache, n_hist, tm_in, tn_in, tm_out, tn_out, y_dtype):
        self.n_seq, self.t_pad, self.valid, self.chunk = n_seq, t_pad, valid, chunk
        self.has_cache, self.n_hist, self.y_dtype = has_cache, n_hist, y_dtype
        self.tm_in, self.tn_in, self.tm_out, self.tn_out = tm_in, tn_in, tm_out, tn_out


def _layer(x, layer, prm, states, l0, acc, tables, path):
    (g_pre, g_post, w_in, w_out, gate_bias, mnorm, sinks, conv_w, conv_b, dt_bias, a_log, d_skip, snorm,
     pool_lin, pool_scale) = prm
    c0, n0, m0, kc, vc, s0, conv0, p0 = states
    ca, na, ma, ka, va, sa, conva, pa = acc
    cos, sin = tables
    p = path
    proj = _inproj(x, g_pre, w_in, layer, p.tm_in, p.tn_in)
    L, valid = p.chunk, min(p.valid, p.chunk)
    parts = [
        _mlstm(proj, gate_bias, mnorm, (c0, n0, m0), l0, (ca, na, ma), layer, p.n_seq, p.t_pad, L, valid, p.y_dtype),
        _swa(proj, cos, sin, sinks, (kc, vc), l0, (ka, va), layer, p.n_seq, p.t_pad, L, valid,
             p.has_cache, not p.has_cache, p.y_dtype),
        _ssd(proj, conv_w, conv_b, dt_bias, a_log, d_skip, snorm, (conv0, s0), l0, (conva, sa), layer,
             p.n_seq, p.t_pad, L, valid, p.y_dtype),
        _pool(proj, pool_lin, pool_scale, (p0,), l0, (pa,), layer, p.n_seq, p.t_pad, L, valid, p.n_hist, p.y_dtype)]
    (ya, ca, na, ma), (yb, ka, va), (yc, conva, sa), (yd, pa) = _fused_call(
        parts, (p.n_seq, p.t_pad // L), "mixers", 48)
    x = _outproj((ya, yb, yc, yd), w_out, layer, x, g_post, p.tm_out, p.tn_out)
    return x, (ca, na, ma, ka, va, sa, conva, pa)


def kernel(x_prompt, x_sample, state_mlstm_C, state_mlstm_n, state_mlstm_m, cache_swa_k, cache_swa_v, state_ssd,
           state_ssd_conv, state_pool, norm_pre, norm_post, w_in, w_out, mlstm_b_i, mlstm_b_f, mlstm_norm,
           swa_sinks, ssd_conv_w, ssd_conv_b, ssd_dt_bias, ssd_A_log, ssd_D, ssd_norm, pool_lin, pool_scale):
    bp, seq, _ = x_prompt.shape
    bs, dec_seq, _ = x_sample.shape
    t_s = SUBLANE * pl.cdiv(dec_seq, SUBLANE)

    assert WINDOW == SSD_CHUNK == 2 * MLSTM_CHUNK
    prompt = _Path(bp, seq, seq, WINDOW, False, 0, 1024, 1280, 512, 1024, BF16)
    sample = _Path(bs, t_s, dec_seq, t_s, True, POOL_HIST, bs * t_s, 1280, bs * t_s, 512, F32)

    w_in_r = _regroup_w_in(w_in)
    w_out_b = w_out.astype(BF16)

    xp = x_prompt.reshape(bp * seq, D_MODEL)
    xs = jnp.pad(x_sample, ((0, 0), (0, t_s - dec_seq), (0, 0))).reshape(bs * t_s, D_MODEL)

    tab_p = _rope_tables(jnp.arange(seq))
    tab_s = _rope_tables(PAST_LEN + jnp.arange(t_s))

    def state_shapes(layers, b):
        return ((layers, b, MLSTM_HEADS, MLSTM_DH, MLSTM_DH), (layers, b, MLSTM_HEADS, LANE), (layers, b, MLSTM_HEADS, LANE),
                (layers, b, WINDOW, LANE), (layers, b, WINDOW, LANE),
                (layers, b, SSD_HEADS, SSD_DH, SSD_DSTATE), (layers, b, SSD_CONV - 1, SSD_CONV_DIM),
                (layers, b, POOL_HIST, MIX))

    p_states0 = tuple(jnp.zeros(s, F32) for s in state_shapes(1, bp))
    s_states0 = (state_mlstm_C, state_mlstm_n,
                 jnp.broadcast_to(state_mlstm_m[..., None], (DEPTH, bs, MLSTM_HEADS, LANE)),
                 cache_swa_k.reshape(DEPTH, bs, WINDOW, LANE), cache_swa_v.reshape(DEPTH, bs, WINDOW, LANE),
                 state_ssd, state_ssd_conv, state_pool)
    p_acc = tuple(jnp.zeros(s, F32) for s in state_shapes(DEPTH, bp))
    s_acc = tuple(jnp.zeros(s, F32) for s in state_shapes(DEPTH, bs))

    for l in range(DEPTH):
        gate_bias = _pad_lanes(jnp.concatenate([mlstm_b_i[l], mlstm_b_f[l]]))
        prm = (norm_pre[l].reshape(1, D_MODEL), norm_post[l].reshape(1, D_MODEL), w_in_r, w_out_b,
               gate_bias, mlstm_norm[l].reshape(1, MIX), _pad_lanes(swa_sinks[l]),
               ssd_conv_w[l], ssd_conv_b[l].reshape(1, SSD_CONV_DIM), _pad_lanes(ssd_dt_bias[l]),
               _pad_lanes(ssd_A_log[l]), jnp.repeat(ssd_D[l], SSD_DH).reshape(1, MIX), ssd_norm[l].reshape(1, MIX),
               pool_lin[l], pool_scale[l].reshape(1, MIX))
        xp, p_acc = _layer(xp, l, prm, p_states0, 0, p_acc, tab_p, prompt)
        xs, s_acc = _layer(xs, l, prm, s_states0, l, s_acc, tab_s, sample)

    def finish(acc, b):
        c, n, m, k, v, s, conv, pool = acc
        kv_shape = (DEPTH, b, WINDOW, SWA_KV_HEADS, SWA_DH)
        return c, n, m[..., 0], k.reshape(kv_shape), v.reshape(kv_shape), s, conv, pool

    y_prompt = xp.reshape(bp, seq, D_MODEL)
    y_sample = xs.reshape(bs, t_s, D_MODEL)[:, :dec_seq]
    return (y_prompt, y_sample) + finish(p_acc, bp) + finish(s_acc, bs)
```

```python
import functools
import math

import jax
import jax.numpy as jnp
from jax import lax
from jax.experimental import pallas as pl
from jax.experimental.pallas import tpu as pltpu

F32 = jnp.float32
BF16 = jnp.bfloat16

D_MODEL = 4096
DEPTH = 4
PAST_LEN = 8192
MIX = D_MODEL // 4
MLSTM_DH = 128
MLSTM_HEADS = MIX // MLSTM_DH
MLSTM_CHUNK = 64
GATE_CAP = 15.0
SWA_DH = 64
SWA_HEADS = MIX // SWA_DH
SWA_KV_HEADS = 2
SWA_GROUP = SWA_HEADS // SWA_KV_HEADS
WINDOW = 128
ROPE_THETA = 10000.0
SSD_DH = 64
SSD_HEADS = MIX // SSD_DH
SSD_GROUPS = 4
SSD_HPG = SSD_HEADS // SSD_GROUPS
SSD_DSTATE = 128
SSD_CONV = 4
SSD_CONV_DIM = MIX + 2 * SSD_GROUPS * SSD_DSTATE
SSD_CHUNK = 128
POOL_WINDOWS = (2, 4, 8, 16)
POOL_GC = MIX // len(POOL_WINDOWS)
POOL_HIST = max(POOL_WINDOWS) - 1
RMS_EPS = 1e-6

LANE = 128
SUBLANE = 8
NEG = -1e30

_COL = {}
_off = 0
for _name, _width in (("qa", MIX), ("ka", MIX), ("va", MIX), ("oa", MIX), ("za", MIX),
                      ("qb", MIX), ("zb", MIX), ("zc", MIX), ("xbc", SSD_CONV_DIM),
                      ("ud", MIX), ("zd", MIX),
                      ("gate", LANE), ("kb", LANE), ("vb", LANE), ("dt", LANE)):
    _COL[_name] = (_off, _width)
    _off += _width
NPROJ = _off


def _colblock(name):
    off, width = _COL[name]
    assert off % width == 0
    return off // width


def _silu(x):
    return x * jax.nn.sigmoid(x)


def _dot(a, b):
    return jnp.dot(a, b, preferred_element_type=F32)


def _dot_nt(a, b):
    return lax.dot_general(a, b, (((1,), (1,)), ((), ())), preferred_element_type=F32)


def _dot_tn(a, b):
    return lax.dot_general(a, b, (((0,), (0,)), ((), ())), preferred_element_type=F32)


def _cumsum_rows(x):
    n = x.shape[0]
    r = lax.broadcasted_iota(jnp.int32, (n, n), 0)
    c = lax.broadcasted_iota(jnp.int32, (n, n), 1)
    tri = (c <= r).astype(F32)
    return jnp.dot(tri, x, preferred_element_type=F32, precision=lax.Precision.HIGHEST)


def _params(sem, vmem_mb):
    return pltpu.CompilerParams(dimension_semantics=sem, vmem_limit_bytes=vmem_mb * 1024 * 1024)


_ANY = pl.BlockSpec(memory_space=pl.ANY)


class _Part:
    def __init__(self, body, operands, in_specs, out_specs, out_shape, n_alias, scratch=()):
        self.body, self.operands, self.in_specs, self.out_specs = body, operands, in_specs, out_specs
        self.out_shape, self.n_alias, self.scratch = out_shape, n_alias, list(scratch)


def _fused_call(parts, grid, name, vmem_mb):
    n_in = [len(p.operands) for p in parts]
    n_out = [len(p.out_shape) for p in parts]
    n_scr = [len(p.scratch) for p in parts]

    def body(*refs):
        ins, outs, scr = refs[:sum(n_in)], refs[sum(n_in):sum(n_in) + sum(n_out)], refs[sum(n_in) + sum(n_out):]

        def run(first_chunk_init):
            i = o = s = 0
            for k, p in enumerate(parts):
                p.body(*ins[i:i + n_in[k]], *outs[o:o + n_out[k]], *scr[s:s + n_scr[k]],
                       first_chunk_init=first_chunk_init)
                i, o, s = i + n_in[k], o + n_out[k], s + n_scr[k]

        pl.when(pl.program_id(len(grid) - 1) == 0)(functools.partial(run, True))
        run(False)

    aliases, i, o = {}, 0, 0
    for k, p in enumerate(parts):
        for a in range(p.n_alias):
            aliases[i + n_in[k] - p.n_alias + a] = o + 1 + a
        i, o = i + n_in[k], o + n_out[k]
    flat = pl.pallas_call(
        body,
        grid=grid,
        in_specs=[s for p in parts for s in p.in_specs],
        out_specs=[s for p in parts for s in p.out_specs],
        out_shape=[s for p in parts for s in p.out_shape],
        scratch_shapes=[s for p in parts for s in p.scratch],
        input_output_aliases=aliases,
        compiler_params=_params(("arbitrary",) * len(grid), vmem_mb),
        name=name,
    )(*[x for p in parts for x in p.operands])
    out, o = [], 0
    for k in range(len(parts)):
        out.append(list(flat[o:o + n_out[k]]))
        o += n_out[k]
    return out


def _state_spec(stacked, layer):
    tail = stacked.shape[2:]
    return pl.BlockSpec((None, 1) + tail, lambda s, c: (layer, s) + (0,) * len(tail))


_NORM_STEPS = 8


def _inproj_kernel(x_ref, g_ref, w_ref, o_ref, h_ref):
    p = pl.program_id(0)
    j = pl.program_id(1)
    rows = x_ref.shape[0]

    def normalise_slice():
        rs = pl.ds(pl.multiple_of(jnp.minimum(j, _NORM_STEPS - 1) * rows, rows), rows)
        x = x_ref[...]
        ms = jnp.mean(x * x, axis=-1, keepdims=True)
        h_ref[p % 2, rs, :] = (x * lax.rsqrt(ms + RMS_EPS) * g_ref[...]).astype(BF16)

    pl.when(p == 0)(normalise_slice)

    @pl.when(p > 0)
    def _():
        normalise_slice()
        o_ref[...] = _dot(h_ref[(p - 1) % 2], w_ref[...])


def _inproj(x, g, w_all, layer, tm, tn):
    m = x.shape[0]
    n_row_tiles, n_col = m // tm, NPROJ // tn
    assert n_col >= _NORM_STEPS and tm % (_NORM_STEPS * 2 * SUBLANE) == 0
    col = lambda p, j: jnp.where(p > 0, j, 0)
    x_slice = lambda p, j: jnp.minimum(p, n_row_tiles - 1) * _NORM_STEPS + jnp.minimum(j, _NORM_STEPS - 1)
    return pl.pallas_call(
        _inproj_kernel,
        grid=(n_row_tiles + 1, n_col),
        in_specs=[pl.BlockSpec((tm // _NORM_STEPS, D_MODEL), lambda p, j: (x_slice(p, j), 0)),
                  pl.BlockSpec((1, D_MODEL), lambda p, j: (0, 0)),
                  pl.BlockSpec((None, D_MODEL, tn), lambda p, j: (layer, 0, col(p, j)))],
        out_specs=pl.BlockSpec((tm, tn), lambda p, j: (jnp.maximum(p - 1, 0), col(p, j))),
        out_shape=jax.ShapeDtypeStruct((m, NPROJ), F32),
        scratch_shapes=[pltpu.VMEM((2, tm, D_MODEL), BF16)],
        compiler_params=_params(("arbitrary", "arbitrary"), 58),
        name="inproj",
    )(x, g, w_all)


def _outproj_kernel(ya_ref, yb_ref, yc_ref, yd_ref, w_ref, x_ref, g_ref, o_ref, acc_ref, ss_ref, *, tn, n_row_tiles):
    i = pl.program_id(0)
    j = pl.program_id(1)
    slot = i % 2
    col = pl.ds(pl.multiple_of(j * tn, tn), tn)

    def finalize():
        rs = lax.rsqrt(ss_ref[1 - slot] * (1.0 / D_MODEL) + RMS_EPS)
        o_ref[...] = x_ref[...] + acc_ref[:, col] * rs * g_ref[...]

    def multiply():
        out = None
        for part, r in enumerate((ya_ref, yb_ref, yc_ref, yd_ref)):
            term = _dot(r[...].astype(BF16), w_ref[part * MIX:(part + 1) * MIX, :])
            out = term if out is None else out + term
        acc_ref[:, col] = out
        sq = jnp.sum(out * out, axis=1, keepdims=True)
        ss_ref[slot] = jnp.where(j == 0, sq, ss_ref[slot] + sq)

    pl.when(i == 0)(multiply)
    pl.when(i == n_row_tiles)(finalize)

    @pl.when(jnp.logical_and(i > 0, i < n_row_tiles))
    def _():
        finalize()
        multiply()


def _outproj(ys, w_all, layer, x, g, tm, tn):
    m = x.shape[0]
    n_row_tiles = m // tm
    yspec = pl.BlockSpec((tm, MIX), lambda i, j: (jnp.minimum(i, n_row_tiles - 1), 0))
    lagged = pl.BlockSpec((tm, tn), lambda i, j: (jnp.maximum(i - 1, 0), jnp.where(i > 0, j, 0)))
    return pl.pallas_call(
        functools.partial(_outproj_kernel, tn=tn, n_row_tiles=n_row_tiles),
        grid=(n_row_tiles + 1, D_MODEL // tn),
        in_specs=[yspec, yspec, yspec, yspec,
                  pl.BlockSpec((None, D_MODEL, tn), lambda i, j: (layer, 0, j)),
                  lagged,
                  pl.BlockSpec((1, tn), lambda i, j: (0, j))],
        out_specs=lagged,
        out_shape=jax.ShapeDtypeStruct((m, D_MODEL), F32),
        scratch_shapes=[pltpu.VMEM((tm, D_MODEL), F32), pltpu.VMEM((2, tm, 1), F32)],
        compiler_params=_params(("arbitrary", "arbitrary"), 56),
        name="outproj",
    )(*ys, w_all, x, g)


def _mlstm_kernel(q_ref, k_ref, v_ref, o_ref, z_ref, gate_ref, bias_ref, mn_ref, c0_ref, n0_ref, m0_ref,
                  c_alias, n_alias, m_alias, y_ref, c_ref, n_ref, m_ref, *, L, valid, first_chunk_init):
    del c_alias, n_alias, m_alias
    if first_chunk_init:
        c_ref[...] = c0_ref[...]
        n_ref[...] = n0_ref[...]
        m_ref[...] = m0_ref[...]
        return

    row = lax.broadcasted_iota(jnp.int32, (L, LANE), 0)
    row_ok = row < valid
    gates = GATE_CAP * jnp.tanh((gate_ref[...] + bias_ref[...]) / GATE_CAP)
    logf = jnp.where(row_ok, jax.nn.log_sigmoid(gates), 0.0)
    b_all = _cumsum_rows(pltpu.roll(logf, LANE - MLSTM_HEADS, axis=1))
    r_all = jnp.where(row_ok, gates - b_all, NEG)
    r_all_t = r_all.T

    tr = lax.broadcasted_iota(jnp.int32, (L, L), 0)
    tc = lax.broadcasted_iota(jnp.int32, (L, L), 1)
    tril = tc <= tr

    heads = range(MLSTM_HEADS)
    sl = [slice(h * MLSTM_DH, (h + 1) * MLSTM_DH) for h in heads]
    q = [q_ref[:, sl[h]] * (MLSTM_DH ** -0.5) for h in heads]
    k = [k_ref[:, sl[h]] for h in heads]
    v = [v_ref[:, sl[h]] for h in heads]
    c_prev = [c_ref[0, h] for h in heads]
    n_prev = [n_ref[0, h:h + 1, :] for h in heads]
    m_prev = [m_ref[0, h:h + 1, 0:1] for h in heads]
    qk = [_dot_nt(q[h], k[h]) for h in heads]
    qc = [_dot(q[h], c_prev[h]) for h in heads]
    qn = [jnp.sum(q[h] * n_prev[h], axis=1, keepdims=True) for h in heads]

    b_col = [b_all[:, h:h + 1] for h in heads]
    dmat = [jnp.where(tril, b_col[h] + r_all_t[h:h + 1, :], NEG) for h in heads]
    inter = [b_col[h] + m_prev[h] for h in heads]
    mt = [jnp.maximum(inter[h], jnp.max(dmat[h], axis=1, keepdims=True)) for h in heads]
    s = [jnp.exp(dmat[h] - mt[h]) * qk[h] for h in heads]
    sv = [_dot(s[h], v[h]) for h in heads]
    w_prev_t = [jnp.exp(inter[h] - mt[h]) for h in heads]
    den = [w_prev_t[h] * qn[h] + jnp.sum(s[h], axis=1, keepdims=True) for h in heads]
    hh = [(w_prev_t[h] * qc[h] + sv[h]) / jnp.maximum(jnp.abs(den[h]), jnp.exp(-mt[h])) for h in heads]

    b_last = [b_all[L - 1:L, h:h + 1] for h in heads]
    dec = [b_last[h] + r_all[:, h:h + 1] for h in heads]
    m_new = [jnp.maximum(b_last[h] + m_prev[h], jnp.max(dec[h], axis=0, keepdims=True)) for h in heads]
    w_prev = [jnp.exp(b_last[h] + m_prev[h] - m_new[h]) for h in heads]
    kw = [k[h] * jnp.exp(dec[h] - m_new[h]) for h in heads]
    c_new = [w_prev[h] * c_prev[h] + _dot_tn(kw[h], v[h]) for h in heads]
    n_new = [w_prev[h] * n_prev[h] + jnp.sum(kw[h], axis=0, keepdims=True) for h in heads]

    hn = [hh[h] * lax.rsqrt(jnp.mean(hh[h] * hh[h], axis=1, keepdims=True) + RMS_EPS) * mn_ref[:, sl[h]] for h in heads]
    ys = [hn[h] * jax.nn.sigmoid(o_ref[:, sl[h]]) * _silu(z_ref[:, sl[h]]) for h in heads]

    y_ref[...] = jnp.concatenate(ys, axis=1).astype(y_ref.dtype)
    for h in heads:
        c_ref[0, h] = c_new[h]
    n_ref[0] = jnp.concatenate(n_new, axis=0)
    m_ref[0] = jnp.concatenate([jnp.broadcast_to(m_new[h], (1, LANE)) for h in heads], axis=0)


def _mlstm(proj, bias, mnorm, st0, l0, acc, layer, n_seq, t_pad, L, valid, y_dtype):
    nc = t_pad // L
    wide = lambda name: pl.BlockSpec((L, MIX), lambda s, c, cb=_colblock(name): (s * nc + c, cb))
    vec = lambda n: pl.BlockSpec((1, n), lambda s, c: (0, 0))
    return _Part(
        functools.partial(_mlstm_kernel, L=L, valid=valid),
        operands=[proj, proj, proj, proj, proj, proj, bias, mnorm, *st0, *acc],
        in_specs=[wide("qa"), wide("ka"), wide("va"), wide("oa"), wide("za"),
                  pl.BlockSpec((L, LANE), lambda s, c, cb=_colblock("gate"): (s * nc + c, cb)),
                  vec(LANE), vec(MIX)] + [_state_spec(a, l0) for a in st0] + [_ANY] * len(acc),
        out_specs=[pl.BlockSpec((L, MIX), lambda s, c: (s * nc + c, 0))] + [_state_spec(a, layer) for a in acc],
        out_shape=[jax.ShapeDtypeStruct((n_seq * t_pad, MIX), y_dtype)]
        + [jax.ShapeDtypeStruct(a.shape, F32) for a in acc],
        n_alias=len(acc))


def _swa_kernel(q_ref, k_ref, v_ref, z_ref, cos_ref, sin_ref, sink_ref, kc_ref, vc_ref, k_alias, v_alias,
                y_ref, ko_ref, vo_ref, *, Lq, valid, has_cache, first_chunk_init):
    del k_alias, v_alias
    if first_chunk_init:
        ko_ref[...] = kc_ref[...]
        vo_ref[...] = vc_ref[...]
        return
    blk = pl.program_id(1)

    cos = cos_ref[...]
    sin = sin_ref[...]
    lane = lax.broadcasted_iota(jnp.int32, (Lq, LANE), 1)
    first_half = (lane % SWA_DH) < (SWA_DH // 2)

    def rope(x):
        partner = jnp.where(first_half, pltpu.roll(x, LANE - SWA_DH // 2, axis=1), pltpu.roll(x, SWA_DH // 2, axis=1))
        return x * cos + partner * sin

    k_cur = rope(k_ref[...])
    v_cur = v_ref[...]
    k_prev = ko_ref[0]
    v_prev = vo_ref[0]
    kk = jnp.concatenate([k_prev, k_cur], axis=0)
    vv = jnp.concatenate([v_prev, v_cur], axis=0)

    rows, nk = SWA_GROUP * Lq, WINDOW + Lq
    t = lax.broadcasted_iota(jnp.int32, (rows, nk), 0) & (Lq - 1)
    c = lax.broadcasted_iota(jnp.int32, (rows, nk), 1)
    prev_ok = jnp.logical_or(has_cache, blk > 0)
    mask = ((c < WINDOW) & (c > t) & prev_ok) | ((c >= WINDOW) & (c - WINDOW <= t) & (c - WINDOW < valid))

    pairs_per_group = SWA_GROUP // 2
    y_chunks = []
    for g in range(SWA_KV_HEADS):
        gsl = slice(g * SWA_DH, (g + 1) * SWA_DH)
        q_pairs = [rope(q_ref[:, (g * pairs_per_group + p) * LANE:(g * pairs_per_group + p + 1) * LANE])
                   * (SWA_DH ** -0.5) for p in range(pairs_per_group)]
        q_st = jnp.concatenate([q_pairs[h // 2][:, (h % 2) * SWA_DH:(h % 2 + 1) * SWA_DH]
                                for h in range(SWA_GROUP)], axis=0)
        sink = jnp.concatenate([jnp.broadcast_to(sink_ref[0:1, g * SWA_GROUP + h:g * SWA_GROUP + h + 1], (Lq, 1))
                                for h in range(SWA_GROUP)], axis=0)
        s = jnp.where(mask, _dot_nt(q_st, kk[:, gsl]), NEG)
        mx = jnp.maximum(jnp.max(s, axis=1, keepdims=True), sink)
        p = jnp.exp(s - mx)
        den = jnp.sum(p, axis=1, keepdims=True) + jnp.exp(sink - mx)
        o = _dot(p, vv[:, gsl]) / den
        for pr in range(pairs_per_group):
            csl = slice((g * pairs_per_group + pr) * LANE, (g * pairs_per_group + pr + 1) * LANE)
            pair = jnp.concatenate([o[(2 * pr) * Lq:(2 * pr + 1) * Lq], o[(2 * pr + 1) * Lq:(2 * pr + 2) * Lq]], axis=1)
            y_chunks.append(pair * _silu(z_ref[:, csl]))
    y_ref[...] = jnp.concatenate(y_chunks, axis=1).astype(y_ref.dtype)

    if valid == WINDOW:
        ko_ref[0] = k_cur
        vo_ref[0] = v_cur
    else:
        ko_ref[0, 0:WINDOW - valid, :] = k_prev[valid:, :]
        ko_ref[0, WINDOW - valid:WINDOW, :] = k_cur[0:valid, :]
        vo_ref[0, 0:WINDOW - valid, :] = v_prev[valid:, :]
        vo_ref[0, WINDOW - valid:WINDOW, :] = v_cur[0:valid, :]


def _swa(proj, cos, sin, sinks, st0, l0, acc, layer, n_seq, t_pad, Lq, valid, has_cache, table_per_block, y_dtype):
    nb = t_pad // Lq
    rowblk = lambda s, b: s * nb + b
    wide = lambda name: pl.BlockSpec((Lq, MIX), lambda s, b, cb=_colblock(name): (rowblk(s, b), cb))
    narrow = lambda name: pl.BlockSpec((Lq, LANE), lambda s, b, cb=_colblock(name): (rowblk(s, b), cb))
    table = pl.BlockSpec((Lq, LANE), (lambda s, b: (b, 0)) if table_per_block else (lambda s, b: (0, 0)))
    return _Part(
        functools.partial(_swa_kernel, Lq=Lq, valid=valid, has_cache=has_cache),
        operands=[proj, proj, proj, proj, cos, sin, sinks, *st0, *acc],
        in_specs=[wide("qb"), narrow("kb"), narrow("vb"), wide("zb"), table, table,
                  pl.BlockSpec((1, LANE), lambda s, b: (0, 0))] + [_state_spec(a, l0) for a in st0] + [_ANY] * len(acc),
        out_specs=[pl.BlockSpec((Lq, MIX), lambda s, b: (rowblk(s, b), 0))] + [_state_spec(a, layer) for a in acc],
        out_shape=[jax.ShapeDtypeStruct((n_seq * t_pad, MIX), y_dtype)]
        + [jax.ShapeDtypeStruct(a.shape, F32) for a in acc],
        n_alias=len(acc))


_CONV_PAD = SUBLANE


def _spread_heads(x, e):
    lane = lax.broadcasted_iota(jnp.int32, x.shape, 1)
    x = jnp.where(lane < SSD_HEADS, x, 0.0)
    hi = x.astype(BF16)
    rest = x - hi.astype(F32)
    mid = rest.astype(BF16)
    lo = (rest - mid.astype(F32)).astype(BF16)
    return _dot(hi, e) + _dot(mid, e) + _dot(lo, e)


def _ssd_kernel(xbc_ref, z_ref, dt_ref, cw_ref, cb_ref, dtb_ref, alog_ref, dskip_ref, sn_ref, e_ref, conv0_ref, s0_ref,
                conv_alias, s_alias, y_ref, convo_ref, so_ref, full_ref, *, L, valid, first_chunk_init):
    del conv_alias, s_alias
    hist = SSD_CONV - 1
    if first_chunk_init:
        so_ref[...] = s0_ref[...]
        full_ref[_CONV_PAD - hist:_CONV_PAD, :] = conv0_ref[0]
        return

    full_ref[_CONV_PAD:_CONV_PAD + L, :] = xbc_ref[...]
    full = full_ref[...]
    acc = cb_ref[...] + full[_CONV_PAD:, :] * cw_ref[hist:hist + 1, :]
    for j in range(hist):
        acc = acc + pltpu.roll(full, hist - j, axis=0)[_CONV_PAD:, :] * cw_ref[j:j + 1, :]
    xc = _silu(acc)
    new_hist = full_ref[_CONV_PAD + valid - hist:_CONV_PAD + valid, :]
    convo_ref[0] = new_hist
    full_ref[_CONV_PAD - hist:_CONV_PAD, :] = new_hist

    nbc = SSD_GROUPS * SSD_DSTATE
    row_ok = lax.broadcasted_iota(jnp.int32, (L, LANE), 0) < valid
    dtv = jnp.where(row_ok, jax.nn.softplus(dt_ref[...] + dtb_ref[...]), 0.0)
    a = dtv * (-jnp.exp(alog_ref[...]))
    cum = _cumsum_rows(a)
    cum_t = cum.T
    tr = lax.broadcasted_iota(jnp.int32, (L, L), 0)
    tc = lax.broadcasted_iota(jnp.int32, (L, L), 1)
    tril = tc <= tr

    groups, heads = range(SSD_GROUPS), range(SSD_HEADS)
    gp = SSD_HPG * SSD_DH
    bg = [xc[:, MIX + g * SSD_DSTATE:MIX + (g + 1) * SSD_DSTATE] for g in groups]
    cg = [xc[:, MIX + nbc + g * SSD_DSTATE:MIX + nbc + (g + 1) * SSD_DSTATE] for g in groups]
    st = [so_ref[0, g * SSD_HPG:(g + 1) * SSD_HPG].reshape(gp, SSD_DSTATE) for g in groups]
    cb = [_dot_nt(cg[g], bg[g]) for g in groups]
    cst = [_dot_nt(cg[g], st[g]) for g in groups]
    c_col = [cum[:, h:h + 1] for h in heads]
    c_last = [cum[L - 1:L, h:h + 1] for h in heads]
    lm = [jnp.where(tril, jnp.exp(jnp.where(tril, c_col[h] - cum_t[h:h + 1, :], 0.0)), 0.0) for h in heads]

    spread = _spread_heads(jnp.concatenate([dtv, jnp.exp(cum), jnp.exp(cum[L - 1:L, :] - cum)], axis=0), e_ref[...])
    xs = xc[:, :MIX]
    xdt = xs * spread[0:L]
    xw = xdt * spread[2 * L:3 * L]
    low_half = lax.broadcasted_iota(jnp.int32, (L, LANE), 1) < SSD_DH
    y_in = []
    for pair in range(SSD_HEADS // 2):
        g = (2 * pair) // SSD_HPG
        x_pair = xdt[:, pair * LANE:(pair + 1) * LANE]
        y_in.append(jnp.where(low_half, _dot(cb[g] * lm[2 * pair], x_pair), _dot(cb[g] * lm[2 * pair + 1], x_pair)))
    y_state = jnp.concatenate(cst, axis=1) * spread[L:2 * L]
    upd = [_dot_tn(xw[:, g * gp:(g + 1) * gp], bg[g]) for g in groups]
    for h in heads:
        g, hh = divmod(h, SSD_HPG)
        so_ref[0, h] = jnp.exp(c_last[h]) * st[g][hh * SSD_DH:(hh + 1) * SSD_DH] + upd[g][hh * SSD_DH:(hh + 1) * SSD_DH]
    yc = (jnp.concatenate(y_in, axis=1) + y_state + dskip_ref[...] * xs) * _silu(z_ref[...])
    y_ref[...] = (yc * lax.rsqrt(jnp.mean(yc * yc, axis=1, keepdims=True) + RMS_EPS) * sn_ref[...]).astype(y_ref.dtype)


def _ssd(proj, conv_w, conv_b, dt_bias, a_log, d_skip, snorm, st0, l0, acc, layer, n_seq, t_pad, L, valid, y_dtype):
    nc = t_pad // L
    rowblk = lambda s, c: s * nc + c
    vec = lambda n: pl.BlockSpec((1, n), lambda s, c: (0, 0))
    spread = (jnp.arange(MIX)[None, :] // SSD_DH == jnp.arange(LANE)[:, None]).astype(BF16)
    return _Part(
        functools.partial(_ssd_kernel, L=L, valid=valid),
        operands=[proj, proj, proj, conv_w, conv_b, dt_bias, a_log, d_skip, snorm, spread, *st0, *acc],
        in_specs=[pl.BlockSpec((L, SSD_CONV_DIM), lambda s, c, cb=_colblock("xbc"): (rowblk(s, c), cb)),
                  pl.BlockSpec((L, MIX), lambda s, c, cb=_colblock("zc"): (rowblk(s, c), cb)),
                  pl.BlockSpec((L, LANE), lambda s, c, cb=_colblock("dt"): (rowblk(s, c), cb)),
                  pl.BlockSpec((SSD_CONV, SSD_CONV_DIM), lambda s, c: (0, 0)),
                  vec(SSD_CONV_DIM), vec(LANE), vec(LANE), vec(MIX), vec(MIX),
                  pl.BlockSpec((LANE, MIX), lambda s, c: (0, 0))]
        + [_state_spec(a, l0) for a in st0] + [_ANY] * len(acc),
        out_specs=[pl.BlockSpec((L, MIX), lambda s, c: (rowblk(s, c), 0))] + [_state_spec(a, layer) for a in acc],
        out_shape=[jax.ShapeDtypeStruct((n_seq * t_pad, MIX), y_dtype)]
        + [jax.ShapeDtypeStruct(a.shape, F32) for a in acc],
        n_alias=len(acc),
        scratch=[pltpu.VMEM((_CONV_PAD + L, SSD_CONV_DIM), F32)])


_POOL_PAD = 2 * SUBLANE
assert all(w & (w - 1) == 0 for w in POOL_WINDOWS) and POOL_HIST < _POOL_PAD


def _pool_kernel(u_ref, z_ref, lin_ref, scale_ref, p0_ref, p_alias, y_ref, po_ref, full_ref, *, L, valid, n_hist,
                 first_chunk_init):
    del p_alias
    if first_chunk_init:
        full_ref[_POOL_PAD - POOL_HIST:_POOL_PAD, :] = p0_ref[0]
        return
    step = pl.program_id(1)

    full_ref[_POOL_PAD:_POOL_PAD + L, :] = u_ref[...]
    pos = n_hist + step * L + lax.broadcasted_iota(jnp.int32, (L, 1), 0)
    ys = []
    for g, w in enumerate(POOL_WINDOWS):
        sl = slice(g * POOL_GC, (g + 1) * POOL_GC)
        rows = full_ref[:, sl]
        tot, span = rows, 1
        while span < w:
            tot = tot + pltpu.roll(tot, span, axis=0)
            span *= 2
        cur = rows[_POOL_PAD:]
        cnt = jnp.minimum(pos + 1, w).astype(F32)
        d = tot[_POOL_PAD:] / cnt - cur
        ys.append(_dot(d, lin_ref[g]) * scale_ref[:, sl] * _silu(z_ref[:, sl]))
    y_ref[...] = jnp.concatenate(ys, axis=1).astype(y_ref.dtype)
    new_hist = full_ref[_POOL_PAD + valid - POOL_HIST:_POOL_PAD + valid, :]
    po_ref[0] = new_hist
    full_ref[_POOL_PAD - POOL_HIST:_POOL_PAD, :] = new_hist


def _pool(proj, lin, scale, st0, l0, acc, layer, n_seq, t_pad, L, valid, n_hist, y_dtype):
    nt = t_pad // L
    rowblk = lambda s, c: s * nt + c
    return _Part(
        functools.partial(_pool_kernel, L=L, valid=valid, n_hist=n_hist),
        operands=[proj, proj, lin, scale, *st0, *acc],
        in_specs=[pl.BlockSpec((L, MIX), lambda s, c, cb=_colblock("ud"): (rowblk(s, c), cb)),
                  pl.BlockSpec((L, MIX), lambda s, c, cb=_colblock("zd"): (rowblk(s, c), cb)),
                  pl.BlockSpec((len(POOL_WINDOWS), POOL_GC, POOL_GC), lambda s, c: (0, 0, 0)),
                  pl.BlockSpec((1, MIX), lambda s, c: (0, 0))] + [_state_spec(a, l0) for a in st0] + [_ANY] * len(acc),
        out_specs=[pl.BlockSpec((L, MIX), lambda s, c: (rowblk(s, c), 0))] + [_state_spec(a, layer) for a in acc],
        out_shape=[jax.ShapeDtypeStruct((n_seq * t_pad, MIX), y_dtype)]
        + [jax.ShapeDtypeStruct(a.shape, F32) for a in acc],
        n_alias=len(acc),
        scratch=[pltpu.VMEM((_POOL_PAD + L, MIX), F32)])


_REF_SIZES = (MIX, MIX, MIX, MIX, MIX, MLSTM_HEADS, MLSTM_HEADS,
              SWA_HEADS * SWA_DH, SWA_KV_HEADS * SWA_DH, SWA_KV_HEADS * SWA_DH, MIX,
              MIX, SSD_CONV_DIM, SSD_HEADS, MIX, MIX)
_REF_NAMES = ("qa", "ka", "va", "oa", "za", "ia", "fa", "qb", "kb", "vb", "zb", "zc", "xbc", "dt", "ud", "zd")


def _regroup_plan():
    ref_off, o = {}, 0
    for name, size in zip(_REF_NAMES, _REF_SIZES):
        ref_off[name] = o
        o += size
    src = []
    for name, (off, width) in _COL.items():
        start = ref_off["ia"] if name == "gate" else ref_off[name]
        src += [start + b * LANE for b in range(width // LANE)]
    assert len(src) == NPROJ // LANE and all(s % SUBLANE == 0 and s + LANE <= o for s in src)
    return src


_REGROUP_BLOCKS = 4


def _regroup_kernel(src_tab, *refs):
    del src_tab
    o_ref = refs[-1]
    for i, wt_ref in enumerate(refs[:-1]):
        o_ref[:, i * LANE:(i + 1) * LANE] = wt_ref[...].T.astype(BF16)


def _regroup_w_in(w_in):
    depth, d_in, _ = w_in.shape
    w_t = jnp.transpose(w_in, (0, 2, 1))
    src = jnp.asarray([s // SUBLANE for s in _regroup_plan()], jnp.int32)
    return pl.pallas_call(
        _regroup_kernel,
        grid_spec=pltpu.PrefetchScalarGridSpec(
            num_scalar_prefetch=1,
            grid=(depth, NPROJ // (_REGROUP_BLOCKS * LANE)),
            in_specs=[pl.BlockSpec((None, pl.Element(LANE), pl.Element(d_in)),
                                   lambda l, t, src, i=i: (l, src[t * _REGROUP_BLOCKS + i] * SUBLANE, 0))
                      for i in range(_REGROUP_BLOCKS)],
            out_specs=pl.BlockSpec((None, d_in, _REGROUP_BLOCKS * LANE), lambda l, t, src: (l, 0, t))),
        out_shape=jax.ShapeDtypeStruct((depth, d_in, NPROJ), BF16),
        compiler_params=_params(("arbitrary", "arbitrary"), 48),
        name="regroup",
    )(src, *([w_t] * _REGROUP_BLOCKS))


def _pad_lanes(v, n=LANE):
    return jnp.pad(v, (0, n - v.shape[0])).reshape(1, n)


def _rope_tables(pos):
    half = SWA_DH // 2
    inv = ROPE_THETA ** (-jnp.arange(half, dtype=F32) / half)
    ang = pos.astype(F32)[:, None] * inv[None, :]
    cos, sin = jnp.cos(ang), jnp.sin(ang)
    reps = LANE // SWA_DH
    return jnp.tile(jnp.concatenate([cos, cos], axis=1), (1, reps)), jnp.tile(jnp.concatenate([-sin, sin], axis=1), (1, reps))


class _Path:
    def __init__(self, n_seq, t_pad, valid, chunk, has_cache, n_hist, tm_in, tn_in, tm_out, tn_out, y_dtype):
        self.n_seq, self.t_pad, self.valid, self.chunk = n_seq, t_pad, valid, chunk
        self.has_cache, self.n_hist, self.y_dtype = has_cache, n_hist, y_dtype
        self.tm_in, self.tn_in, self.tm_out, self.tn_out = tm_in, tn_in, tm_out, tn_out


def _layer(x, layer, prm, states, l0, acc, tables, path):
    (g_pre, g_post, w_in, w_out, gate_bias, mnorm, sinks, conv_w, conv_b, dt_bias, a_log, d_skip, snorm,
     pool_lin, pool_scale) = prm
    c0, n0, m0, kc, vc, s0, conv0, p0 = states
    ca, na, ma, ka, va, sa, conva, pa = acc
    cos, sin = tables
    p = path
    proj = _inproj(x, g_pre, w_in, layer, p.tm_in, p.tn_in)
    L, valid = p.chunk, min(p.valid, p.chunk)
    parts = [
        _mlstm(proj, gate_bias, mnorm, (c0, n0, m0), l0, (ca, na, ma), layer, p.n_seq, p.t_pad, L, valid, p.y_dtype),
        _swa(proj, cos, sin, sinks, (kc, vc), l0, (ka, va), layer, p.n_seq, p.t_pad, L, valid,
             p.has_cache, not p.has_cache, p.y_dtype),
        _ssd(proj, conv_w, conv_b, dt_bias, a_log, d_skip, snorm, (conv0, s0), l0, (conva, sa), layer,
             p.n_seq, p.t_pad, L, valid, p.y_dtype),
        _pool(proj, pool_lin, pool_scale, (p0,), l0, (pa,), layer, p.n_seq, p.t_pad, L, valid, p.n_hist, p.y_dtype)]
    (ya, ca, na, ma), (yb, ka, va), (yc, conva, sa), (yd, pa) = _fused_call(
        parts, (p.n_seq, p.t_pad // L), "mixers", 48)
    x = _outproj((ya, yb, yc, yd), w_out, layer, x, g_post, p.tm_out, p.tn_out)
    return x, (ca, na, ma, ka, va, sa, conva, pa)


def kernel(x_prompt, x_sample, state_mlstm_C, state_mlstm_n, state_mlstm_m, cache_swa_k, cache_swa_v, state_ssd,
           state_ssd_conv, state_pool, norm_pre, norm_post, w_in, w_out, mlstm_b_i, mlstm_b_f, mlstm_norm,
           swa_sinks, ssd_conv_w, ssd_conv_b, ssd_dt_bias, ssd_A_log, ssd_D, ssd_norm, pool_lin, pool_scale):
    bp, seq, _ = x_prompt.shape
    bs, dec_seq, _ = x_sample.shape
    t_s = SUBLANE * pl.cdiv(dec_seq, SUBLANE)

    assert WINDOW == SSD_CHUNK == 2 * MLSTM_CHUNK
    prompt = _Path(bp, seq, seq, WINDOW, False, 0, 1024, 1280, 1024, 512, BF16)
    sample = _Path(bs, t_s, dec_seq, t_s, True, POOL_HIST, bs * t_s, 1280, bs * t_s, 512, F32)

    w_in_r = _regroup_w_in(w_in)
    w_out_b = w_out.astype(BF16)

    xp = x_prompt.reshape(bp * seq, D_MODEL)
    xs = jnp.pad(x_sample, ((0, 0), (0, t_s - dec_seq), (0, 0))).reshape(bs * t_s, D_MODEL)

    tab_p = _rope_tables(jnp.arange(seq))
    tab_s = _rope_tables(PAST_LEN + jnp.arange(t_s))

    def state_shapes(layers, b):
        return ((layers, b, MLSTM_HEADS, MLSTM_DH, MLSTM_DH), (layers, b, MLSTM_HEADS, LANE), (layers, b, MLSTM_HEADS, LANE),
                (layers, b, WINDOW, LANE), (layers, b, WINDOW, LANE),
                (layers, b, SSD_HEADS, SSD_DH, SSD_DSTATE), (layers, b, SSD_CONV - 1, SSD_CONV_DIM),
                (layers, b, POOL_HIST, MIX))

    p_states0 = tuple(jnp.zeros(s, F32) for s in state_shapes(1, bp))
    s_states0 = (state_mlstm_C, state_mlstm_n,
                 jnp.broadcast_to(state_mlstm_m[..., None], (DEPTH, bs, MLSTM_HEADS, LANE)),
                 cache_swa_k.reshape(DEPTH, bs, WINDOW, LANE), cache_swa_v.reshape(DEPTH, bs, WINDOW, LANE),
                 state_ssd, state_ssd_conv, state_pool)
    p_acc = tuple(jnp.zeros(s, F32) for s in state_shapes(DEPTH, bp))
    s_acc = tuple(jnp.zeros(s, F32) for s in state_shapes(DEPTH, bs))

    for l in range(DEPTH):
        gate_bias = _pad_lanes(jnp.concatenate([mlstm_b_i[l], mlstm_b_f[l]]))
        prm = (norm_pre[l].reshape(1, D_MODEL), norm_post[l].reshape(1, D_MODEL), w_in_r, w_out_b,
               gate_bias, mlstm_norm[l].reshape(1, MIX), _pad_lanes(swa_sinks[l]),
               ssd_conv_w[l], ssd_conv_b[l].reshape(1, SSD_CONV_DIM), _pad_lanes(ssd_dt_bias[l]),
               _pad_lanes(ssd_A_log[l]), jnp.repeat(ssd_D[l], SSD_DH).reshape(1, MIX), ssd_norm[l].reshape(1, MIX),
               pool_lin[l], pool_scale[l].reshape(1, MIX))
        xp, p_acc = _layer(xp, l, prm, p_states0, 0, p_acc, tab_p, prompt)
        xs, s_acc = _layer(xs, l, prm, s_states0, l, s_acc, tab_s, sample)

    def finish(acc, b):
        c, n, m, k, v, s, conv, pool = acc
        kv_shape = (DEPTH, b, WINDOW, SWA_KV_HEADS, SWA_DH)
        return c, n, m[..., 0], k.reshape(kv_shape), v.reshape(kv_shape), s, conv, pool

    y_prompt = xp.reshape(bp, seq, D_MODEL)
    y_sample = xs.reshape(bs, t_s, D_MODEL)[:, :dec_seq]
    return (y_prompt, y_sample) + finish(p_acc, bp) + finish(s_acc, bs)
```

```python
import functools
import math

import jax
import jax.numpy as jnp
from jax import lax
from jax.experimental import pallas as pl
from jax.experimental.pallas import tpu as pltpu

F32 = jnp.float32
BF16 = jnp.bfloat16

D_MODEL = 4096
DEPTH = 4
PAST_LEN = 8192
MIX = D_MODEL // 4
MLSTM_DH = 128
MLSTM_HEADS = MIX // MLSTM_DH
MLSTM_CHUNK = 64
GATE_CAP = 15.0
SWA_DH = 64
SWA_HEADS = MIX // SWA_DH
SWA_KV_HEADS = 2
SWA_GROUP = SWA_HEADS // SWA_KV_HEADS
WINDOW = 128
ROPE_THETA = 10000.0
SSD_DH = 64
SSD_HEADS = MIX // SSD_DH
SSD_GROUPS = 4
SSD_HPG = SSD_HEADS // SSD_GROUPS
SSD_DSTATE = 128
SSD_CONV = 4
SSD_CONV_DIM = MIX + 2 * SSD_GROUPS * SSD_DSTATE
SSD_CHUNK = 128
POOL_WINDOWS = (2, 4, 8, 16)
POOL_GC = MIX // len(POOL_WINDOWS)
POOL_HIST = max(POOL_WINDOWS) - 1
RMS_EPS = 1e-6

LANE = 128
SUBLANE = 8
NEG = -1e30

_COL = {}
_off = 0
for _name, _width in (("qa", MIX), ("ka", MIX), ("va", MIX), ("oa", MIX), ("za", MIX),
                      ("qb", MIX), ("zb", MIX), ("zc", MIX), ("xbc", SSD_CONV_DIM),
                      ("ud", MIX), ("zd", MIX),
                      ("gate", LANE), ("kb", LANE), ("vb", LANE), ("dt", LANE)):
    _COL[_name] = (_off, _width)
    _off += _width
NPROJ = _off


def _colblock(name):
    off, width = _COL[name]
    assert off % width == 0
    return off // width


LOG2E = math.log2(math.e)


def _sigmoid(x):
    return 0.5 * jnp.tanh(0.5 * x) + 0.5


def _silu(x):
    half = 0.5 * x
    return half + half * jnp.tanh(half)


def _dot(a, b):
    return jnp.dot(a, b, preferred_element_type=F32)


def _dot_nt(a, b):
    return lax.dot_general(a, b, (((1,), (1,)), ((), ())), preferred_element_type=F32)


def _dot_tn(a, b):
    return lax.dot_general(a, b, (((0,), (0,)), ((), ())), preferred_element_type=F32)


def _cumsum_rows(x):
    n = x.shape[0]
    r = lax.broadcasted_iota(jnp.int32, (n, n), 0)
    c = lax.broadcasted_iota(jnp.int32, (n, n), 1)
    tri = (c <= r).astype(F32)
    return jnp.dot(tri, x, preferred_element_type=F32, precision=lax.Precision.HIGHEST)


def _params(sem, vmem_mb):
    return pltpu.CompilerParams(dimension_semantics=sem, vmem_limit_bytes=vmem_mb * 1024 * 1024)


_ANY = pl.BlockSpec(memory_space=pl.ANY)


class _Part:
    def __init__(self, body, operands, in_specs, out_specs, out_shape, n_alias, scratch=()):
        self.body, self.operands, self.in_specs, self.out_specs = body, operands, in_specs, out_specs
        self.out_shape, self.n_alias, self.scratch = out_shape, n_alias, list(scratch)


def _fused_call(parts, grid, name, vmem_mb):
    n_in = [len(p.operands) for p in parts]
    n_out = [len(p.out_shape) for p in parts]
    n_scr = [len(p.scratch) for p in parts]

    def body(*refs):
        ins, outs, scr = refs[:sum(n_in)], refs[sum(n_in):sum(n_in) + sum(n_out)], refs[sum(n_in) + sum(n_out):]

        def run(first_chunk_init):
            i = o = s = 0
            for k, p in enumerate(parts):
                p.body(*ins[i:i + n_in[k]], *outs[o:o + n_out[k]], *scr[s:s + n_scr[k]],
                       first_chunk_init=first_chunk_init)
                i, o, s = i + n_in[k], o + n_out[k], s + n_scr[k]

        pl.when(pl.program_id(len(grid) - 1) == 0)(functools.partial(run, True))
        run(False)

    aliases, i, o = {}, 0, 0
    for k, p in enumerate(parts):
        for a in range(p.n_alias):
            aliases[i + n_in[k] - p.n_alias + a] = o + 1 + a
        i, o = i + n_in[k], o + n_out[k]
    flat = pl.pallas_call(
        body,
        grid=grid,
        in_specs=[s for p in parts for s in p.in_specs],
        out_specs=[s for p in parts for s in p.out_specs],
        out_shape=[s for p in parts for s in p.out_shape],
        scratch_shapes=[s for p in parts for s in p.scratch],
        input_output_aliases=aliases,
        compiler_params=_params(("arbitrary",) * len(grid), vmem_mb),
        name=name,
    )(*[x for p in parts for x in p.operands])
    out, o = [], 0
    for k in range(len(parts)):
        out.append(list(flat[o:o + n_out[k]]))
        o += n_out[k]
    return out


def _state_spec(stacked, layer):
    tail = stacked.shape[2:]
    return pl.BlockSpec((None, 1) + tail, lambda s, c: (layer, s) + (0,) * len(tail))


_NORM_STEPS = 8


def _inproj_kernel(x_ref, g_ref, w_ref, o_ref, h_ref):
    p = pl.program_id(0)
    j = pl.program_id(1)
    rows = x_ref.shape[0]

    def normalise_slice():
        rs = pl.ds(pl.multiple_of(jnp.minimum(j, _NORM_STEPS - 1) * rows, rows), rows)
        x = x_ref[...]
        ms = jnp.mean(x * x, axis=-1, keepdims=True)
        h_ref[p % 2, rs, :] = (x * lax.rsqrt(ms + RMS_EPS) * g_ref[...]).astype(BF16)

    pl.when(p == 0)(normalise_slice)

    @pl.when(p > 0)
    def _():
        normalise_slice()
        o_ref[...] = _dot(h_ref[(p - 1) % 2], w_ref[...])


def _inproj(x, g, w_all, layer, tm, tn):
    m = x.shape[0]
    n_row_tiles, n_col = m // tm, NPROJ // tn
    assert n_col >= _NORM_STEPS and tm % (_NORM_STEPS * 2 * SUBLANE) == 0
    col = lambda p, j: jnp.where(p > 0, j, 0)
    x_slice = lambda p, j: jnp.minimum(p, n_row_tiles - 1) * _NORM_STEPS + jnp.minimum(j, _NORM_STEPS - 1)
    return pl.pallas_call(
        _inproj_kernel,
        grid=(n_row_tiles + 1, n_col),
        in_specs=[pl.BlockSpec((tm // _NORM_STEPS, D_MODEL), lambda p, j: (x_slice(p, j), 0)),
                  pl.BlockSpec((1, D_MODEL), lambda p, j: (0, 0)),
                  pl.BlockSpec((None, D_MODEL, tn), lambda p, j: (layer, 0, col(p, j)))],
        out_specs=pl.BlockSpec((tm, tn), lambda p, j: (jnp.maximum(p - 1, 0), col(p, j))),
        out_shape=jax.ShapeDtypeStruct((m, NPROJ), F32),
        scratch_shapes=[pltpu.VMEM((2, tm, D_MODEL), BF16)],
        compiler_params=_params(("arbitrary", "arbitrary"), 58),
        name="inproj",
    )(x, g, w_all)


def _outproj_kernel(ya_ref, yb_ref, yc_ref, yd_ref, w_ref, x_ref, g_ref, o_ref, acc_ref, ss_ref, *, tn, n_row_tiles):
    i = pl.program_id(0)
    j = pl.program_id(1)
    slot = i % 2
    col = pl.ds(pl.multiple_of(j * tn, tn), tn)

    def finalize():
        rs = lax.rsqrt(ss_ref[1 - slot] * (1.0 / D_MODEL) + RMS_EPS)
        o_ref[...] = x_ref[...] + acc_ref[:, col] * rs * g_ref[...]

    def multiply():
        out = None
        for part, r in enumerate((ya_ref, yb_ref, yc_ref, yd_ref)):
            term = _dot(r[...].astype(BF16), w_ref[part * MIX:(part + 1) * MIX, :])
            out = term if out is None else out + term
        acc_ref[:, col] = out
        sq = jnp.sum(out * out, axis=1, keepdims=True)
        ss_ref[slot] = jnp.where(j == 0, sq, ss_ref[slot] + sq)

    pl.when(i == 0)(multiply)
    pl.when(i == n_row_tiles)(finalize)

    @pl.when(jnp.logical_and(i > 0, i < n_row_tiles))
    def _():
        finalize()
        multiply()


def _outproj(ys, w_all, layer, x, g, tm, tn):
    m = x.shape[0]
    n_row_tiles = m // tm
    yspec = pl.BlockSpec((tm, MIX), lambda i, j: (jnp.minimum(i, n_row_tiles - 1), 0))
    lagged = pl.BlockSpec((tm, tn), lambda i, j: (jnp.maximum(i - 1, 0), jnp.where(i > 0, j, 0)))
    return pl.pallas_call(
        functools.partial(_outproj_kernel, tn=tn, n_row_tiles=n_row_tiles),
        grid=(n_row_tiles + 1, D_MODEL // tn),
        in_specs=[yspec, yspec, yspec, yspec,
                  pl.BlockSpec((None, D_MODEL, tn), lambda i, j: (layer, 0, j)),
                  lagged,
                  pl.BlockSpec((1, tn), lambda i, j: (0, j))],
        out_specs=lagged,
        out_shape=jax.ShapeDtypeStruct((m, D_MODEL), F32),
        scratch_shapes=[pltpu.VMEM((tm, D_MODEL), F32), pltpu.VMEM((2, tm, 1), F32)],
        compiler_params=_params(("arbitrary", "arbitrary"), 56),
        name="outproj",
    )(*ys, w_all, x, g)


def _mlstm_kernel(q_ref, k_ref, v_ref, o_ref, z_ref, gate_ref, bias_ref, mn_ref, c0_ref, n0_ref, m0_ref,
                  c_alias, n_alias, m_alias, y_ref, c_ref, n_ref, m_ref, *, L, valid, first_chunk_init):
    del c_alias, n_alias, m_alias
    if first_chunk_init:
        c_ref[...] = c0_ref[...]
        n_ref[...] = n0_ref[...]
        m_ref[...] = m0_ref[...]
        return

    row = lax.broadcasted_iota(jnp.int32, (L, LANE), 0)
    row_ok = row < valid
    gates = GATE_CAP * jnp.tanh((gate_ref[...] + bias_ref[...]) / GATE_CAP)
    logf = jnp.where(row_ok, jax.nn.log_sigmoid(gates), 0.0)
    b_all = _cumsum_rows(pltpu.roll(logf, LANE - MLSTM_HEADS, axis=1))
    r_all = jnp.where(row_ok, gates - b_all, NEG)
    r_all_t = r_all.T

    tr = lax.broadcasted_iota(jnp.int32, (L, L), 0)
    tc = lax.broadcasted_iota(jnp.int32, (L, L), 1)
    tril = tc <= tr

    heads = range(MLSTM_HEADS)
    sl = [slice(h * MLSTM_DH, (h + 1) * MLSTM_DH) for h in heads]
    q = [q_ref[:, sl[h]] * (MLSTM_DH ** -0.5) for h in heads]
    k = [k_ref[:, sl[h]] for h in heads]
    v = [v_ref[:, sl[h]] for h in heads]
    c_prev = [c_ref[0, h] for h in heads]
    n_prev = [n_ref[0, h:h + 1, :] for h in heads]
    m_prev = [m_ref[0, h:h + 1, 0:1] for h in heads]
    qk = [_dot_nt(q[h], k[h]) for h in heads]
    qc = [_dot(q[h], c_prev[h]) for h in heads]
    qn = [jnp.sum(q[h] * n_prev[h], axis=1, keepdims=True) for h in heads]

    b_col = [b_all[:, h:h + 1] for h in heads]
    dmat = [jnp.where(tril, b_col[h] + r_all_t[h:h + 1, :], NEG) for h in heads]
    inter = [b_col[h] + m_prev[h] for h in heads]
    mt = [jnp.maximum(inter[h], jnp.max(dmat[h], axis=1, keepdims=True)) for h in heads]
    s = [jnp.exp(dmat[h] - mt[h]) * qk[h] for h in heads]
    sv = [_dot(s[h], v[h]) for h in heads]
    w_prev_t = [jnp.exp(inter[h] - mt[h]) for h in heads]
    den = [w_prev_t[h] * qn[h] + jnp.sum(s[h], axis=1, keepdims=True) for h in heads]
    hh = [(w_prev_t[h] * qc[h] + sv[h]) / jnp.maximum(jnp.abs(den[h]), jnp.exp(-mt[h])) for h in heads]

    b_last = [b_all[L - 1:L, h:h + 1] for h in heads]
    dec = [b_last[h] + r_all[:, h:h + 1] for h in heads]
    m_new = [jnp.maximum(b_last[h] + m_prev[h], jnp.max(dec[h], axis=0, keepdims=True)) for h in heads]
    w_prev = [jnp.exp(b_last[h] + m_prev[h] - m_new[h]) for h in heads]
    kw = [k[h] * jnp.exp(dec[h] - m_new[h]) for h in heads]
    c_new = [w_prev[h] * c_prev[h] + _dot_tn(kw[h], v[h]) for h in heads]
    n_new = [w_prev[h] * n_prev[h] + jnp.sum(kw[h], axis=0, keepdims=True) for h in heads]

    hn = [hh[h] * lax.rsqrt(jnp.mean(hh[h] * hh[h], axis=1, keepdims=True) + RMS_EPS) * mn_ref[:, sl[h]] for h in heads]
    ys = [hn[h] * _sigmoid(o_ref[:, sl[h]]) * _silu(z_ref[:, sl[h]]) for h in heads]

    y_ref[...] = jnp.concatenate(ys, axis=1).astype(y_ref.dtype)
    for h in heads:
        c_ref[0, h] = c_new[h]
    n_ref[0] = jnp.concatenate(n_new, axis=0)
    m_ref[0] = jnp.concatenate([jnp.broadcast_to(m_new[h], (1, LANE)) for h in heads], axis=0)


def _mlstm(proj, bias, mnorm, st0, l0, acc, layer, n_seq, t_pad, L, valid, y_dtype):
    nc = t_pad // L
    wide = lambda name: pl.BlockSpec((L, MIX), lambda s, c, cb=_colblock(name): (s * nc + c, cb))
    vec = lambda n: pl.BlockSpec((1, n), lambda s, c: (0, 0))
    return _Part(
        functools.partial(_mlstm_kernel, L=L, valid=valid),
        operands=[proj, proj, proj, proj, proj, proj, bias, mnorm, *st0, *acc],
        in_specs=[wide("qa"), wide("ka"), wide("va"), wide("oa"), wide("za"),
                  pl.BlockSpec((L, LANE), lambda s, c, cb=_colblock("gate"): (s * nc + c, cb)),
                  vec(LANE), vec(MIX)] + [_state_spec(a, l0) for a in st0] + [_ANY] * len(acc),
        out_specs=[pl.BlockSpec((L, MIX), lambda s, c: (s * nc + c, 0))] + [_state_spec(a, layer) for a in acc],
        out_shape=[jax.ShapeDtypeStruct((n_seq * t_pad, MIX), y_dtype)]
        + [jax.ShapeDtypeStruct(a.shape, F32) for a in acc],
        n_alias=len(acc))


def _swa_kernel(q_ref, k_ref, v_ref, z_ref, cos_ref, sin_ref, sink_ref, kc_ref, vc_ref, k_alias, v_alias,
                y_ref, ko_ref, vo_ref, *, Lq, valid, has_cache, first_chunk_init):
    del k_alias, v_alias
    if first_chunk_init:
        ko_ref[...] = kc_ref[...]
        vo_ref[...] = vc_ref[...]
        return
    blk = pl.program_id(1)

    cos = cos_ref[...]
    sin = sin_ref[...]
    lane = lax.broadcasted_iota(jnp.int32, (Lq, LANE), 1)
    first_half = (lane % SWA_DH) < (SWA_DH // 2)

    def rope(x):
        partner = jnp.where(first_half, pltpu.roll(x, LANE - SWA_DH // 2, axis=1), pltpu.roll(x, SWA_DH // 2, axis=1))
        return x * cos + partner * sin

    k_cur = rope(k_ref[...])
    v_cur = v_ref[...]
    k_prev = ko_ref[0]
    v_prev = vo_ref[0]
    kk = jnp.concatenate([k_prev, k_cur], axis=0)
    vv = jnp.concatenate([v_prev, v_cur], axis=0)

    rows, nk = SWA_GROUP * Lq, WINDOW + Lq
    t = lax.broadcasted_iota(jnp.int32, (rows, nk), 0) & (Lq - 1)
    c = lax.broadcasted_iota(jnp.int32, (rows, nk), 1)
    prev_ok = jnp.logical_or(has_cache, blk > 0)
    mask = ((c < WINDOW) & (c > t) & prev_ok) | ((c >= WINDOW) & (c - WINDOW <= t) & (c - WINDOW < valid))

    pairs_per_group = SWA_GROUP // 2
    y_chunks = []
    for g in range(SWA_KV_HEADS):
        gsl = slice(g * SWA_DH, (g + 1) * SWA_DH)
        q_pairs = [rope(q_ref[:, (g * pairs_per_group + p) * LANE:(g * pairs_per_group + p + 1) * LANE])
                   * (SWA_DH ** -0.5 * LOG2E) for p in range(pairs_per_group)]
        q_st = jnp.concatenate([q_pairs[h // 2][:, (h % 2) * SWA_DH:(h % 2 + 1) * SWA_DH]
                                for h in range(SWA_GROUP)], axis=0)
        sink = jnp.concatenate([jnp.broadcast_to(sink_ref[0:1, g * SWA_GROUP + h:g * SWA_GROUP + h + 1], (Lq, 1))
                                for h in range(SWA_GROUP)], axis=0) * LOG2E
        s = jnp.where(mask, _dot_nt(q_st, kk[:, gsl]), NEG)
        mx = jnp.maximum(jnp.max(s, axis=1, keepdims=True), sink)
        p = jnp.exp2(s - mx)
        den = jnp.sum(p, axis=1, keepdims=True) + jnp.exp2(sink - mx)
        o = _dot(p, vv[:, gsl]) / den
        for pr in range(pairs_per_group):
            csl = slice((g * pairs_per_group + pr) * LANE, (g * pairs_per_group + pr + 1) * LANE)
            pair = jnp.concatenate([o[(2 * pr) * Lq:(2 * pr + 1) * Lq], o[(2 * pr + 1) * Lq:(2 * pr + 2) * Lq]], axis=1)
            y_chunks.append(pair * _silu(z_ref[:, csl]))
    y_ref[...] = jnp.concatenate(y_chunks, axis=1).astype(y_ref.dtype)

    if valid == WINDOW:
        ko_ref[0] = k_cur
        vo_ref[0] = v_cur
    else:
        ko_ref[0, 0:WINDOW - valid, :] = k_prev[valid:, :]
        ko_ref[0, WINDOW - valid:WINDOW, :] = k_cur[0:valid, :]
        vo_ref[0, 0:WINDOW - valid, :] = v_prev[valid:, :]
        vo_ref[0, WINDOW - valid:WINDOW, :] = v_cur[0:valid, :]


def _swa(proj, cos, sin, sinks, st0, l0, acc, layer, n_seq, t_pad, Lq, valid, has_cache, table_per_block, y_dtype):
    nb = t_pad // Lq
    rowblk = lambda s, b: s * nb + b
    wide = lambda name: pl.BlockSpec((Lq, MIX), lambda s, b, cb=_colblock(name): (rowblk(s, b), cb))
    narrow = lambda name: pl.BlockSpec((Lq, LANE), lambda s, b, cb=_colblock(name): (rowblk(s, b), cb))
    table = pl.BlockSpec((Lq, LANE), (lambda s, b: (b, 0)) if table_per_block else (lambda s, b: (0, 0)))
    return _Part(
        functools.partial(_swa_kernel, Lq=Lq, valid=valid, has_cache=has_cache),
        operands=[proj, proj, proj, proj, cos, sin, sinks, *st0, *acc],
        in_specs=[wide("qb"), narrow("kb"), narrow("vb"), wide("zb"), table, table,
                  pl.BlockSpec((1, LANE), lambda s, b: (0, 0))] + [_state_spec(a, l0) for a in st0] + [_ANY] * len(acc),
        out_specs=[pl.BlockSpec((Lq, MIX), lambda s, b: (rowblk(s, b), 0))] + [_state_spec(a, layer) for a in acc],
        out_shape=[jax.ShapeDtypeStruct((n_seq * t_pad, MIX), y_dtype)]
        + [jax.ShapeDtypeStruct(a.shape, F32) for a in acc],
        n_alias=len(acc))


_CONV_PAD = SUBLANE


def _spread_heads(x, e):
    lane = lax.broadcasted_iota(jnp.int32, x.shape, 1)
    x = jnp.where(lane < SSD_HEADS, x, 0.0)
    hi = x.astype(BF16)
    rest = x - hi.astype(F32)
    mid = rest.astype(BF16)
    lo = (rest - mid.astype(F32)).astype(BF16)
    return _dot(hi, e) + _dot(mid, e) + _dot(lo, e)


def _ssd_kernel(xbc_ref, z_ref, dt_ref, cw_ref, cb_ref, dtb_ref, alog_ref, dskip_ref, sn_ref, e_ref, conv0_ref, s0_ref,
                conv_alias, s_alias, y_ref, convo_ref, so_ref, full_ref, *, L, valid, first_chunk_init):
    del conv_alias, s_alias
    hist = SSD_CONV - 1
    if first_chunk_init:
        so_ref[...] = s0_ref[...]
        full_ref[_CONV_PAD - hist:_CONV_PAD, :] = conv0_ref[0]
        return

    full_ref[_CONV_PAD:_CONV_PAD + L, :] = xbc_ref[...]
    full = full_ref[...]
    acc = cb_ref[...] + full[_CONV_PAD:, :] * cw_ref[hist:hist + 1, :]
    for j in range(hist):
        acc = acc + pltpu.roll(full, hist - j, axis=0)[_CONV_PAD:, :] * cw_ref[j:j + 1, :]
    xc = _silu(acc)
    new_hist = full_ref[_CONV_PAD + valid - hist:_CONV_PAD + valid, :]
    convo_ref[0] = new_hist
    full_ref[_CONV_PAD - hist:_CONV_PAD, :] = new_hist

    nbc = SSD_GROUPS * SSD_DSTATE
    row_ok = lax.broadcasted_iota(jnp.int32, (L, LANE), 0) < valid
    dtv = jnp.where(row_ok, jax.nn.softplus(dt_ref[...] + dtb_ref[...]), 0.0)
    a = dtv * (-jnp.exp(alog_ref[...]) * LOG2E)
    cum = _cumsum_rows(a)
    cum_t = cum.T
    tr = lax.broadcasted_iota(jnp.int32, (L, L), 0)
    tc = lax.broadcasted_iota(jnp.int32, (L, L), 1)
    tril = tc <= tr

    groups, heads = range(SSD_GROUPS), range(SSD_HEADS)
    gp = SSD_HPG * SSD_DH
    bg = [xc[:, MIX + g * SSD_DSTATE:MIX + (g + 1) * SSD_DSTATE] for g in groups]
    cg = [xc[:, MIX + nbc + g * SSD_DSTATE:MIX + nbc + (g + 1) * SSD_DSTATE] for g in groups]
    st = [so_ref[0, g * SSD_HPG:(g + 1) * SSD_HPG].reshape(gp, SSD_DSTATE) for g in groups]
    cb = [_dot_nt(cg[g], bg[g]) for g in groups]
    cst = [_dot_nt(cg[g], st[g]) for g in groups]
    c_col = [cum[:, h:h + 1] for h in heads]
    c_last = [cum[L - 1:L, h:h + 1] for h in heads]
    lm = [jnp.where(tril, jnp.exp2(jnp.where(tril, c_col[h] - cum_t[h:h + 1, :], 0.0)), 0.0) for h in heads]

    spread = _spread_heads(jnp.concatenate([dtv, jnp.exp2(cum), jnp.exp2(cum[L - 1:L, :] - cum)], axis=0), e_ref[...])
    xs = xc[:, :MIX]
    xdt = xs * spread[0:L]
    xw = xdt * spread[2 * L:3 * L]
    low_half = lax.broadcasted_iota(jnp.int32, (L, LANE), 1) < SSD_DH
    y_in = []
    for pair in range(SSD_HEADS // 2):
        g = (2 * pair) // SSD_HPG
        x_pair = xdt[:, pair * LANE:(pair + 1) * LANE]
        y_in.append(jnp.where(low_half, _dot(cb[g] * lm[2 * pair], x_pair), _dot(cb[g] * lm[2 * pair + 1], x_pair)))
    y_state = jnp.concatenate(cst, axis=1) * spread[L:2 * L]
    upd = [_dot_tn(xw[:, g * gp:(g + 1) * gp], bg[g]) for g in groups]
    for h in heads:
        g, hh = divmod(h, SSD_HPG)
        so_ref[0, h] = jnp.exp2(c_last[h]) * st[g][hh * SSD_DH:(hh + 1) * SSD_DH] + upd[g][hh * SSD_DH:(hh + 1) * SSD_DH]
    yc = (jnp.concatenate(y_in, axis=1) + y_state + dskip_ref[...] * xs) * _silu(z_ref[...])
    y_ref[...] = (yc * lax.rsqrt(jnp.mean(yc * yc, axis=1, keepdims=True) + RMS_EPS) * sn_ref[...]).astype(y_ref.dtype)


def _ssd(proj, conv_w, conv_b, dt_bias, a_log, d_skip, snorm, st0, l0, acc, layer, n_seq, t_pad, L, valid, y_dtype):
    nc = t_pad // L
    rowblk = lambda s, c: s * nc + c
    vec = lambda n: pl.BlockSpec((1, n), lambda s, c: (0, 0))
    spread = (jnp.arange(MIX)[None, :] // SSD_DH == jnp.arange(LANE)[:, None]).astype(BF16)
    return _Part(
        functools.partial(_ssd_kernel, L=L, valid=valid),
        operands=[proj, proj, proj, conv_w, conv_b, dt_bias, a_log, d_skip, snorm, spread, *st0, *acc],
        in_specs=[pl.BlockSpec((L, SSD_CONV_DIM), lambda s, c, cb=_colblock("xbc"): (rowblk(s, c), cb)),
                  pl.BlockSpec((L, MIX), lambda s, c, cb=_colblock("zc"): (rowblk(s, c), cb)),
                  pl.BlockSpec((L, LANE), lambda s, c, cb=_colblock("dt"): (rowblk(s, c), cb)),
                  pl.BlockSpec((SSD_CONV, SSD_CONV_DIM), lambda s, c: (0, 0)),
                  vec(SSD_CONV_DIM), vec(LANE), vec(LANE), vec(MIX), vec(MIX),
                  pl.BlockSpec((LANE, MIX), lambda s, c: (0, 0))]
        + [_state_spec(a, l0) for a in st0] + [_ANY] * len(acc),
        out_specs=[pl.BlockSpec((L, MIX), lambda s, c: (rowblk(s, c), 0))] + [_state_spec(a, layer) for a in acc],
        out_shape=[jax.ShapeDtypeStruct((n_seq * t_pad, MIX), y_dtype)]
        + [jax.ShapeDtypeStruct(a.shape, F32) for a in acc],
        n_alias=len(acc),
        scratch=[pltpu.VMEM((_CONV_PAD + L, SSD_CONV_DIM), F32)])


_POOL_PAD = 2 * SUBLANE
assert all(w & (w - 1) == 0 for w in POOL_WINDOWS) and POOL_HIST < _POOL_PAD


def _pool_kernel(u_ref, z_ref, lin_ref, scale_ref, p0_ref, p_alias, y_ref, po_ref, full_ref, *, L, valid, n_hist,
                 first_chunk_init):
    del p_alias
    if first_chunk_init:
        full_ref[_POOL_PAD - POOL_HIST:_POOL_PAD, :] = p0_ref[0]
        return
    step = pl.program_id(1)

    full_ref[_POOL_PAD:_POOL_PAD + L, :] = u_ref[...]
    pos = n_hist + step * L + lax.broadcasted_iota(jnp.int32, (L, 1), 0)
    ys = []
    for g, w in enumerate(POOL_WINDOWS):
        sl = slice(g * POOL_GC, (g + 1) * POOL_GC)
        rows = full_ref[:, sl]
        tot, span = rows, 1
        while span < w:
            tot = tot + pltpu.roll(tot, span, axis=0)
            span *= 2
        cur = rows[_POOL_PAD:]
        cnt = jnp.minimum(pos + 1, w).astype(F32)
        d = tot[_POOL_PAD:] / cnt - cur
        ys.append(_dot(d, lin_ref[g]) * scale_ref[:, sl] * _silu(z_ref[:, sl]))
    y_ref[...] = jnp.concatenate(ys, axis=1).astype(y_ref.dtype)
    new_hist = full_ref[_POOL_PAD + valid - POOL_HIST:_POOL_PAD + valid, :]
    po_ref[0] = new_hist
    full_ref[_POOL_PAD - POOL_HIST:_POOL_PAD, :] = new_hist


def _pool(proj, lin, scale, st0, l0, acc, layer, n_seq, t_pad, L, valid, n_hist, y_dtype):
    nt = t_pad // L
    rowblk = lambda s, c: s * nt + c
    return _Part(
        functools.partial(_pool_kernel, L=L, valid=valid, n_hist=n_hist),
        operands=[proj, proj, lin, scale, *st0, *acc],
        in_specs=[pl.BlockSpec((L, MIX), lambda s, c, cb=_colblock("ud"): (rowblk(s, c), cb)),
                  pl.BlockSpec((L, MIX), lambda s, c, cb=_colblock("zd"): (rowblk(s, c), cb)),
                  pl.BlockSpec((len(POOL_WINDOWS), POOL_GC, POOL_GC), lambda s, c: (0, 0, 0)),
                  pl.BlockSpec((1, MIX), lambda s, c: (0, 0))] + [_state_spec(a, l0) for a in st0] + [_ANY] * len(acc),
        out_specs=[pl.BlockSpec((L, MIX), lambda s, c: (rowblk(s, c), 0))] + [_state_spec(a, layer) for a in acc],
        out_shape=[jax.ShapeDtypeStruct((n_seq * t_pad, MIX), y_dtype)]
        + [jax.ShapeDtypeStruct(a.shape, F32) for a in acc],
        n_alias=len(acc),
        scratch=[pltpu.VMEM((_POOL_PAD + L, MIX), F32)])


_REF_SIZES = (MIX, MIX, MIX, MIX, MIX, MLSTM_HEADS, MLSTM_HEADS,
              SWA_HEADS * SWA_DH, SWA_KV_HEADS * SWA_DH, SWA_KV_HEADS * SWA_DH, MIX,
              MIX, SSD_CONV_DIM, SSD_HEADS, MIX, MIX)
_REF_NAMES = ("qa", "ka", "va", "oa", "za", "ia", "fa", "qb", "kb", "vb", "zb", "zc", "xbc", "dt", "ud", "zd")


def _regroup_plan():
    ref_off, o = {}, 0
    for name, size in zip(_REF_NAMES, _REF_SIZES):
        ref_off[name] = o
        o += size
    src = []
    for name, (off, width) in _COL.items():
        start = ref_off["ia"] if name == "gate" else ref_off[name]
        src += [start + b * LANE for b in range(width // LANE)]
    assert len(src) == NPROJ // LANE and all(s % SUBLANE == 0 and s + LANE <= o for s in src)
    return src


_REGROUP_BLOCKS = 4


def _regroup_kernel(src_tab, *refs):
    del src_tab
    o_ref = refs[-1]
    for i, wt_ref in enumerate(refs[:-1]):
        o_ref[:, i * LANE:(i + 1) * LANE] = wt_ref[...].T.astype(BF16)


def _regroup_w_in(w_in):
    depth, d_in, _ = w_in.shape
    w_t = jnp.transpose(w_in, (0, 2, 1))
    src = jnp.asarray([s // SUBLANE for s in _regroup_plan()], jnp.int32)
    return pl.pallas_call(
        _regroup_kernel,
        grid_spec=pltpu.PrefetchScalarGridSpec(
            num_scalar_prefetch=1,
            grid=(depth, NPROJ // (_REGROUP_BLOCKS * LANE)),
            in_specs=[pl.BlockSpec((None, pl.Element(LANE), pl.Element(d_in)),
                                   lambda l, t, src, i=i: (l, src[t * _REGROUP_BLOCKS + i] * SUBLANE, 0))
                      for i in range(_REGROUP_BLOCKS)],
            out_specs=pl.BlockSpec((None, d_in, _REGROUP_BLOCKS * LANE), lambda l, t, src: (l, 0, t))),
        out_shape=jax.ShapeDtypeStruct((depth, d_in, NPROJ), BF16),
        compiler_params=_params(("arbitrary", "arbitrary"), 48),
        name="regroup",
    )(src, *([w_t] * _REGROUP_BLOCKS))


def _pad_lanes(v, n=LANE):
    return jnp.pad(v, (0, n - v.shape[0])).reshape(1, n)


def _rope_tables(pos):
    half = SWA_DH // 2
    inv = ROPE_THETA ** (-jnp.arange(half, dtype=F32) / half)
    ang = pos.astype(F32)[:, None] * inv[None, :]
    cos, sin = jnp.cos(ang), jnp.sin(ang)
    reps = LANE // SWA_DH
    return jnp.tile(jnp.concatenate([cos, cos], axis=1), (1, reps)), jnp.tile(jnp.concatenate([-sin, sin], axis=1), (1, reps))


class _Path:
    def __init__(self, n_seq, t_pad, valid, chunk, has_cache, n_hist, tm_in, tn_in, tm_out, tn_out, y_dtype):
        self.n_seq, self.t_pad, self.valid, self.chunk = n_seq, t_pad, valid, chunk
        self.has_cache, self.n_hist, self.y_dtype = has_cache, n_hist, y_dtype
        self.tm_in, self.tn_in, self.tm_out, self.tn_out = tm_in, tn_in, tm_out, tn_out


def _layer(x, layer, prm, states, l0, acc, tables, path):
    (g_pre, g_post, w_in, w_out, gate_bias, mnorm, sinks, conv_w, conv_b, dt_bias, a_log, d_skip, snorm,
     pool_lin, pool_scale) = prm
    c0, n0, m0, kc, vc, s0, conv0, p0 = states
    ca, na, ma, ka, va, sa, conva, pa = acc
    cos, sin = tables
    p = path
    proj = _inproj(x, g_pre, w_in, layer, p.tm_in, p.tn_in)
    L, valid = p.chunk, min(p.valid, p.chunk)
    parts = [
        _mlstm(proj, gate_bias, mnorm, (c0, n0, m0), l0, (ca, na, ma), layer, p.n_seq, p.t_pad, L, valid, p.y_dtype),
        _swa(proj, cos, sin, sinks, (kc, vc), l0, (ka, va), layer, p.n_seq, p.t_pad, L, valid,
             p.has_cache, not p.has_cache, p.y_dtype),
        _ssd(proj, conv_w, conv_b, dt_bias, a_log, d_skip, snorm, (conv0, s0), l0, (conva, sa), layer,
             p.n_seq, p.t_pad, L, valid, p.y_dtype),
        _pool(proj, pool_lin, pool_scale, (p0,), l0, (pa,), layer, p.n_seq, p.t_pad, L, valid, p.n_hist, p.y_dtype)]
    (ya, ca, na, ma), (yb, ka, va), (yc, conva, sa), (yd, pa) = _fused_call(
        parts, (p.n_seq, p.t_pad // L), "mixers", 48)
    x = _outproj((ya, yb, yc, yd), w_out, layer, x, g_post, p.tm_out, p.tn_out)
    return x, (ca, na, ma, ka, va, sa, conva, pa)


def kernel(x_prompt, x_sample, state_mlstm_C, state_mlstm_n, state_mlstm_m, cache_swa_k, cache_swa_v, state_ssd,
           state_ssd_conv, state_pool, norm_pre, norm_post, w_in, w_out, mlstm_b_i, mlstm_b_f, mlstm_norm,
           swa_sinks, ssd_conv_w, ssd_conv_b, ssd_dt_bias, ssd_A_log, ssd_D, ssd_norm, pool_lin, pool_scale):
    bp, seq, _ = x_prompt.shape
    bs, dec_seq, _ = x_sample.shape
    t_s = SUBLANE * pl.cdiv(dec_seq, SUBLANE)

    assert WINDOW == SSD_CHUNK == 2 * MLSTM_CHUNK
    prompt = _Path(bp, seq, seq, WINDOW, False, 0, 1024, 1280, 1024, 512, BF16)
    sample = _Path(bs, t_s, dec_seq, t_s, True, POOL_HIST, bs * t_s, 1280, bs * t_s, 512, F32)

    w_in_r = _regroup_w_in(w_in)
    w_out_b = w_out.astype(BF16)

    xp = x_prompt.reshape(bp * seq, D_MODEL)
    xs = jnp.pad(x_sample, ((0, 0), (0, t_s - dec_seq), (0, 0))).reshape(bs * t_s, D_MODEL)

    tab_p = _rope_tables(jnp.arange(seq))
    tab_s = _rope_tables(PAST_LEN + jnp.arange(t_s))

    def state_shapes(layers, b):
        return ((layers, b, MLSTM_HEADS, MLSTM_DH, MLSTM_DH), (layers, b, MLSTM_HEADS, LANE), (layers, b, MLSTM_HEADS, LANE),
                (layers, b, WINDOW, LANE), (layers, b, WINDOW, LANE),
                (layers, b, SSD_HEADS, SSD_DH, SSD_DSTATE), (layers, b, SSD_CONV - 1, SSD_CONV_DIM),
                (layers, b, POOL_HIST, MIX))

    p_states0 = tuple(jnp.zeros(s, F32) for s in state_shapes(1, bp))
    s_states0 = (state_mlstm_C, state_mlstm_n,
                 jnp.broadcast_to(state_mlstm_m[..., None], (DEPTH, bs, MLSTM_HEADS, LANE)),
                 cache_swa_k.reshape(DEPTH, bs, WINDOW, LANE), cache_swa_v.reshape(DEPTH, bs, WINDOW, LANE),
                 state_ssd, state_ssd_conv, state_pool)
    p_acc = tuple(jnp.zeros(s, F32) for s in state_shapes(DEPTH, bp))
    s_acc = tuple(jnp.zeros(s, F32) for s in state_shapes(DEPTH, bs))

    for l in range(DEPTH):
        gate_bias = _pad_lanes(jnp.concatenate([mlstm_b_i[l], mlstm_b_f[l]]))
        prm = (norm_pre[l].reshape(1, D_MODEL), norm_post[l].reshape(1, D_MODEL), w_in_r, w_out_b,
               gate_bias, mlstm_norm[l].reshape(1, MIX), _pad_lanes(swa_sinks[l]),
               ssd_conv_w[l], ssd_conv_b[l].reshape(1, SSD_CONV_DIM), _pad_lanes(ssd_dt_bias[l]),
               _pad_lanes(ssd_A_log[l]), jnp.repeat(ssd_D[l], SSD_DH).reshape(1, MIX), ssd_norm[l].reshape(1, MIX),
               pool_lin[l], pool_scale[l].reshape(1, MIX))
        xp, p_acc = _layer(xp, l, prm, p_states0, 0, p_acc, tab_p, prompt)
        xs, s_acc = _layer(xs, l, prm, s_states0, l, s_acc, tab_s, sample)

    def finish(acc, b):
        c, n, m, k, v, s, conv, pool = acc
        kv_shape = (DEPTH, b, WINDOW, SWA_KV_HEADS, SWA_DH)
        return c, n, m[..., 0], k.reshape(kv_shape), v.reshape(kv_shape), s, conv, pool

    y_prompt = xp.reshape(bp, seq, D_MODEL)
    y_sample = xs.reshape(bs, t_s, D_MODEL)[:, :dec_seq]
    return (y_prompt, y_sample) + finish(p_acc, bp) + finish(s_acc, bs)
```

```python
import functools
import math

import jax
import jax.numpy as jnp
from jax import lax
from jax.experimental import pallas as pl
from jax.experimental.pallas import tpu as pltpu

F32 = jnp.float32
BF16 = jnp.bfloat16

D_MODEL = 4096
DEPTH = 4
PAST_LEN = 8192
MIX = D_MODEL // 4
MLSTM_DH = 128
MLSTM_HEADS = MIX // MLSTM_DH
MLSTM_CHUNK = 64
GATE_CAP = 15.0
SWA_DH = 64
SWA_HEADS = MIX // SWA_DH
SWA_KV_HEADS = 2
SWA_GROUP = SWA_HEADS // SWA_KV_HEADS
WINDOW = 128
ROPE_THETA = 10000.0
SSD_DH = 64
SSD_HEADS = MIX // SSD_DH
SSD_GROUPS = 4
SSD_HPG = SSD_HEADS // SSD_GROUPS
SSD_DSTATE = 128
SSD_CONV = 4
SSD_CONV_DIM = MIX + 2 * SSD_GROUPS * SSD_DSTATE
SSD_CHUNK = 128
POOL_WINDOWS = (2, 4, 8, 16)
POOL_GC = MIX // len(POOL_WINDOWS)
POOL_HIST = max(POOL_WINDOWS) - 1
RMS_EPS = 1e-6

LANE = 128
SUBLANE = 8
NEG = -1e30

_COL = {}
_off = 0
for _name, _width in (("qa", MIX), ("ka", MIX), ("va", MIX), ("oa", MIX), ("za", MIX),
                      ("qb", MIX), ("zb", MIX), ("zc", MIX), ("xbc", SSD_CONV_DIM),
                      ("ud", MIX), ("zd", MIX),
                      ("gate", LANE), ("kb", LANE), ("vb", LANE), ("dt", LANE)):
    _COL[_name] = (_off, _width)
    _off += _width
NPROJ = _off


def _colblock(name):
    off, width = _COL[name]
    assert off % width == 0
    return off // width


LOG2E = math.log2(math.e)


def _sigmoid(x):
    return 0.5 * jnp.tanh(0.5 * x) + 0.5


def _silu(x):
    half = 0.5 * x
    return half + half * jnp.tanh(half)


def _dot(a, b):
    return jnp.dot(a, b, preferred_element_type=F32)


def _dot_nt(a, b):
    return lax.dot_general(a, b, (((1,), (1,)), ((), ())), preferred_element_type=F32)


def _dot_tn(a, b):
    return lax.dot_general(a, b, (((0,), (0,)), ((), ())), preferred_element_type=F32)


def _cumsum_rows(x):
    n = x.shape[0]
    r = lax.broadcasted_iota(jnp.int32, (n, n), 0)
    c = lax.broadcasted_iota(jnp.int32, (n, n), 1)
    tri = (c <= r).astype(F32)
    return jnp.dot(tri, x, preferred_element_type=F32, precision=lax.Precision.HIGHEST)


def _params(sem, vmem_mb):
    return pltpu.CompilerParams(dimension_semantics=sem, vmem_limit_bytes=vmem_mb * 1024 * 1024)


_ANY = pl.BlockSpec(memory_space=pl.ANY)


class _Part:
    def __init__(self, body, operands, in_specs, out_specs, out_shape, n_alias, kinds, rows, scratch=()):
        self.body, self.operands, self.in_specs, self.out_specs = body, operands, in_specs, out_specs
        self.out_shape, self.n_alias, self.scratch = out_shape, n_alias, list(scratch)
        self.kinds, self.rows = kinds, rows
        assert len(kinds) == len(operands) + len(out_shape)


def _fused_call(parts, grid, name, vmem_mb, seqs_per_step=1):
    n_in = [len(p.operands) for p in parts]
    n_out = [len(p.out_shape) for p in parts]
    n_scr = [len(p.scratch) * seqs_per_step for p in parts]

    def view(ref, kind, rows, q):
        if kind == "rows":
            return ref.at[pl.ds(q * rows, rows)]
        return ref.at[pl.ds(q, 1)] if kind == "state" else ref

    def body(*refs):
        ins, outs, scr = refs[:sum(n_in)], refs[sum(n_in):sum(n_in) + sum(n_out)], refs[sum(n_in) + sum(n_out):]

        def run(first_chunk_init):
            i = o = s = 0
            for k, p in enumerate(parts):
                io = list(ins[i:i + n_in[k]]) + list(outs[o:o + n_out[k]])
                per_seq = n_scr[k] // seqs_per_step
                for q in range(seqs_per_step):
                    p.body(*[view(r, kind, p.rows, q) for r, kind in zip(io, p.kinds)],
                           *scr[s + q * per_seq:s + (q + 1) * per_seq], first_chunk_init=first_chunk_init)
                i, o, s = i + n_in[k], o + n_out[k], s + n_scr[k]

        pl.when(pl.program_id(len(grid) - 1) == 0)(functools.partial(run, True))
        run(False)

    aliases, i, o = {}, 0, 0
    for k, p in enumerate(parts):
        for a in range(p.n_alias):
            aliases[i + n_in[k] - p.n_alias + a] = o + 1 + a
        i, o = i + n_in[k], o + n_out[k]
    flat = pl.pallas_call(
        body,
        grid=grid,
        in_specs=[s for p in parts for s in p.in_specs],
        out_specs=[s for p in parts for s in p.out_specs],
        out_shape=[s for p in parts for s in p.out_shape],
        scratch_shapes=[s for p in parts for s in p.scratch * seqs_per_step],
        input_output_aliases=aliases,
        compiler_params=_params(("arbitrary",) * len(grid), vmem_mb),
        name=name,
    )(*[x for p in parts for x in p.operands])
    out, o = [], 0
    for k in range(len(parts)):
        out.append(list(flat[o:o + n_out[k]]))
        o += n_out[k]
    return out


def _state_spec(stacked, layer, seqs=1):
    tail = stacked.shape[2:]
    return pl.BlockSpec((None, seqs) + tail, lambda s, c: (layer, s) + (0,) * len(tail))


_NORM_STEPS = 8


def _inproj_kernel(x_ref, g_ref, w_ref, o_ref, h_ref):
    p = pl.program_id(0)
    j = pl.program_id(1)
    rows = x_ref.shape[0]

    def normalise_slice():
        rs = pl.ds(pl.multiple_of(jnp.minimum(j, _NORM_STEPS - 1) * rows, rows), rows)
        x = x_ref[...]
        ms = jnp.mean(x * x, axis=-1, keepdims=True)
        h_ref[p % 2, rs, :] = (x * lax.rsqrt(ms + RMS_EPS) * g_ref[...]).astype(BF16)

    pl.when(p == 0)(normalise_slice)

    @pl.when(p > 0)
    def _():
        normalise_slice()
        o_ref[...] = _dot(h_ref[(p - 1) % 2], w_ref[...])


def _inproj(x, g, w_all, layer, tm, tn):
    m = x.shape[0]
    n_row_tiles, n_col = m // tm, NPROJ // tn
    assert n_col >= _NORM_STEPS and tm % (_NORM_STEPS * 2 * SUBLANE) == 0
    col = lambda p, j: jnp.where(p > 0, j, 0)
    x_slice = lambda p, j: jnp.minimum(p, n_row_tiles - 1) * _NORM_STEPS + jnp.minimum(j, _NORM_STEPS - 1)
    return pl.pallas_call(
        _inproj_kernel,
        grid=(n_row_tiles + 1, n_col),
        in_specs=[pl.BlockSpec((tm // _NORM_STEPS, D_MODEL), lambda p, j: (x_slice(p, j), 0)),
                  pl.BlockSpec((1, D_MODEL), lambda p, j: (0, 0)),
                  pl.BlockSpec((None, D_MODEL, tn), lambda p, j: (layer, 0, col(p, j)))],
        out_specs=pl.BlockSpec((tm, tn), lambda p, j: (jnp.maximum(p - 1, 0), col(p, j))),
        out_shape=jax.ShapeDtypeStruct((m, NPROJ), F32),
        scratch_shapes=[pltpu.VMEM((2, tm, D_MODEL), BF16)],
        compiler_params=_params(("arbitrary", "arbitrary"), 58),
        name="inproj",
    )(x, g, w_all)


def _outproj_kernel(ya_ref, yb_ref, yc_ref, yd_ref, w_ref, x_ref, g_ref, o_ref, acc_ref, ss_ref, *, tn, n_row_tiles):
    i = pl.program_id(0)
    j = pl.program_id(1)
    slot = i % 2
    col = pl.ds(pl.multiple_of(j * tn, tn), tn)

    def finalize():
        rs = lax.rsqrt(ss_ref[1 - slot] * (1.0 / D_MODEL) + RMS_EPS)
        o_ref[...] = x_ref[...] + acc_ref[:, col] * rs * g_ref[...]

    def multiply():
        out = None
        for part, r in enumerate((ya_ref, yb_ref, yc_ref, yd_ref)):
            term = _dot(r[...].astype(BF16), w_ref[part * MIX:(part + 1) * MIX, :])
            out = term if out is None else out + term
        acc_ref[:, col] = out
        sq = jnp.sum(out * out, axis=1, keepdims=True)
        ss_ref[slot] = jnp.where(j == 0, sq, ss_ref[slot] + sq)

    pl.when(i == 0)(multiply)
    pl.when(i == n_row_tiles)(finalize)

    @pl.when(jnp.logical_and(i > 0, i < n_row_tiles))
    def _():
        finalize()
        multiply()


def _outproj(ys, w_all, layer, x, g, tm, tn):
    m = x.shape[0]
    n_row_tiles = m // tm
    yspec = pl.BlockSpec((tm, MIX), lambda i, j: (jnp.minimum(i, n_row_tiles - 1), 0))
    lagged = pl.BlockSpec((tm, tn), lambda i, j: (jnp.maximum(i - 1, 0), jnp.where(i > 0, j, 0)))
    return pl.pallas_call(
        functools.partial(_outproj_kernel, tn=tn, n_row_tiles=n_row_tiles),
        grid=(n_row_tiles + 1, D_MODEL // tn),
        in_specs=[yspec, yspec, yspec, yspec,
                  pl.BlockSpec((None, D_MODEL, tn), lambda i, j: (layer, 0, j)),
                  lagged,
                  pl.BlockSpec((1, tn), lambda i, j: (0, j))],
        out_specs=lagged,
        out_shape=jax.ShapeDtypeStruct((m, D_MODEL), F32),
        scratch_shapes=[pltpu.VMEM((tm, D_MODEL), F32), pltpu.VMEM((2, tm, 1), F32)],
        compiler_params=_params(("arbitrary", "arbitrary"), 56),
        name="outproj",
    )(*ys, w_all, x, g)


def _mlstm_kernel(q_ref, k_ref, v_ref, o_ref, z_ref, gate_ref, bias_ref, mn_ref, c0_ref, n0_ref, m0_ref,
                  c_alias, n_alias, m_alias, y_ref, c_ref, n_ref, m_ref, *, L, valid, first_chunk_init):
    del c_alias, n_alias, m_alias
    if first_chunk_init:
        c_ref[...] = c0_ref[...]
        n_ref[...] = n0_ref[...]
        m_ref[...] = m0_ref[...]
        return

    row = lax.broadcasted_iota(jnp.int32, (L, LANE), 0)
    row_ok = row < valid
    gates = GATE_CAP * jnp.tanh((gate_ref[...] + bias_ref[...]) / GATE_CAP)
    logf = jnp.where(row_ok, jax.nn.log_sigmoid(gates), 0.0)
    b_all = _cumsum_rows(pltpu.roll(logf, LANE - MLSTM_HEADS, axis=1))
    r_all = jnp.where(row_ok, gates - b_all, NEG)
    r_all_t = r_all.T

    tr = lax.broadcasted_iota(jnp.int32, (L, L), 0)
    tc = lax.broadcasted_iota(jnp.int32, (L, L), 1)
    tril = tc <= tr

    heads = range(MLSTM_HEADS)
    sl = [slice(h * MLSTM_DH, (h + 1) * MLSTM_DH) for h in heads]
    q = [q_ref[:, sl[h]] * (MLSTM_DH ** -0.5) for h in heads]
    k = [k_ref[:, sl[h]] for h in heads]
    v = [v_ref[:, sl[h]] for h in heads]
    c_prev = [c_ref[0, h] for h in heads]
    n_prev = [n_ref[0, h:h + 1, :] for h in heads]
    m_prev = [m_ref[0, h:h + 1, 0:1] for h in heads]
    qk = [_dot_nt(q[h], k[h]) for h in heads]
    qc = [_dot(q[h], c_prev[h]) for h in heads]
    qn = [jnp.sum(q[h] * n_prev[h], axis=1, keepdims=True) for h in heads]

    b_col = [b_all[:, h:h + 1] for h in heads]
    dmat = [jnp.where(tril, b_col[h] + r_all_t[h:h + 1, :], NEG) for h in heads]
    inter = [b_col[h] + m_prev[h] for h in heads]
    mt = [jnp.maximum(inter[h], jnp.max(dmat[h], axis=1, keepdims=True)) for h in heads]
    s = [jnp.exp(dmat[h] - mt[h]) * qk[h] for h in heads]
    sv = [_dot(s[h], v[h]) for h in heads]
    w_prev_t = [jnp.exp(inter[h] - mt[h]) for h in heads]
    den = [w_prev_t[h] * qn[h] + jnp.sum(s[h], axis=1, keepdims=True) for h in heads]
    hh = [(w_prev_t[h] * qc[h] + sv[h]) / jnp.maximum(jnp.abs(den[h]), jnp.exp(-mt[h])) for h in heads]

    b_last = [b_all[L - 1:L, h:h + 1] for h in heads]
    dec = [b_last[h] + r_all[:, h:h + 1] for h in heads]
    m_new = [jnp.maximum(b_last[h] + m_prev[h], jnp.max(dec[h], axis=0, keepdims=True)) for h in heads]
    w_prev = [jnp.exp(b_last[h] + m_prev[h] - m_new[h]) for h in heads]
    kw = [k[h] * jnp.exp(dec[h] - m_new[h]) for h in heads]
    c_new = [w_prev[h] * c_prev[h] + _dot_tn(kw[h], v[h]) for h in heads]
    n_new = [w_prev[h] * n_prev[h] + jnp.sum(kw[h], axis=0, keepdims=True) for h in heads]

    hn = [hh[h] * lax.rsqrt(jnp.mean(hh[h] * hh[h], axis=1, keepdims=True) + RMS_EPS) * mn_ref[:, sl[h]] for h in heads]
    ys = [hn[h] * _sigmoid(o_ref[:, sl[h]]) * _silu(z_ref[:, sl[h]]) for h in heads]

    y_ref[...] = jnp.concatenate(ys, axis=1).astype(y_ref.dtype)
    for h in heads:
        c_ref[0, h] = c_new[h]
    n_ref[0] = jnp.concatenate(n_new, axis=0)
    m_ref[0] = jnp.concatenate([jnp.broadcast_to(m_new[h], (1, LANE)) for h in heads], axis=0)


def _mlstm(proj, bias, mnorm, st0, l0, acc, layer, n_seq, t_pad, L, valid, y_dtype, S):
    nc = t_pad // L
    wide = lambda name: pl.BlockSpec((S * L, MIX), lambda s, c, cb=_colblock(name): (s * nc + c, cb))
    vec = lambda n: pl.BlockSpec((1, n), lambda s, c: (0, 0))
    return _Part(
        functools.partial(_mlstm_kernel, L=L, valid=valid),
        operands=[proj, proj, proj, proj, proj, proj, bias, mnorm, *st0, *acc],
        in_specs=[wide("qa"), wide("ka"), wide("va"), wide("oa"), wide("za"),
                  pl.BlockSpec((S * L, LANE), lambda s, c, cb=_colblock("gate"): (s * nc + c, cb)),
                  vec(LANE), vec(MIX)] + [_state_spec(a, l0, S) for a in st0] + [_ANY] * len(acc),
        out_specs=[pl.BlockSpec((S * L, MIX), lambda s, c: (s * nc + c, 0))] + [_state_spec(a, layer, S) for a in acc],
        out_shape=[jax.ShapeDtypeStruct((n_seq * t_pad, MIX), y_dtype)]
        + [jax.ShapeDtypeStruct(a.shape, F32) for a in acc],
        n_alias=len(acc),
        kinds=["rows"] * 6 + ["whole"] * 2 + ["state"] * 3 + ["whole"] * 3 + ["rows"] + ["state"] * 3, rows=L)


def _swa_kernel(q_ref, k_ref, v_ref, z_ref, cos_ref, sin_ref, sink_ref, kc_ref, vc_ref, k_alias, v_alias,
                y_ref, ko_ref, vo_ref, *, Lq, valid, has_cache, first_chunk_init):
    del k_alias, v_alias
    if first_chunk_init:
        ko_ref[...] = kc_ref[...]
        vo_ref[...] = vc_ref[...]
        return
    blk = pl.program_id(1)

    cos = cos_ref[...]
    sin = sin_ref[...]
    lane = lax.broadcasted_iota(jnp.int32, (Lq, LANE), 1)
    first_half = (lane % SWA_DH) < (SWA_DH // 2)

    def rope(x):
        partner = jnp.where(first_half, pltpu.roll(x, LANE - SWA_DH // 2, axis=1), pltpu.roll(x, SWA_DH // 2, axis=1))
        return x * cos + partner * sin

    k_cur = rope(k_ref[...])
    v_cur = v_ref[...]
    k_prev = ko_ref[0]
    v_prev = vo_ref[0]
    kk = jnp.concatenate([k_prev, k_cur], axis=0)
    vv = jnp.concatenate([v_prev, v_cur], axis=0)

    rows, nk = SWA_GROUP * Lq, WINDOW + Lq
    t = lax.broadcasted_iota(jnp.int32, (rows, nk), 0) & (Lq - 1)
    c = lax.broadcasted_iota(jnp.int32, (rows, nk), 1)
    prev_ok = jnp.logical_or(has_cache, blk > 0)
    mask = ((c < WINDOW) & (c > t) & prev_ok) | ((c >= WINDOW) & (c - WINDOW <= t) & (c - WINDOW < valid))

    pairs_per_group = SWA_GROUP // 2
    y_chunks = []
    for g in range(SWA_KV_HEADS):
        gsl = slice(g * SWA_DH, (g + 1) * SWA_DH)
        q_pairs = [rope(q_ref[:, (g * pairs_per_group + p) * LANE:(g * pairs_per_group + p + 1) * LANE])
                   * (SWA_DH ** -0.5 * LOG2E) for p in range(pairs_per_group)]
        q_st = jnp.concatenate([q_pairs[h // 2][:, (h % 2) * SWA_DH:(h % 2 + 1) * SWA_DH]
                                for h in range(SWA_GROUP)], axis=0)
        sink = jnp.concatenate([jnp.broadcast_to(sink_ref[0:1, g * SWA_GROUP + h:g * SWA_GROUP + h + 1], (Lq, 1))
                                for h in range(SWA_GROUP)], axis=0) * LOG2E
        s = jnp.where(mask, _dot_nt(q_st, kk[:, gsl]), NEG)
        mx = jnp.maximum(jnp.max(s, axis=1, keepdims=True), sink)
        p = jnp.exp2(s - mx)
        den = jnp.sum(p, axis=1, keepdims=True) + jnp.exp2(sink - mx)
        o = _dot(p, vv[:, gsl]) / den
        for pr in range(pairs_per_group):
            csl = slice((g * pairs_per_group + pr) * LANE, (g * pairs_per_group + pr + 1) * LANE)
            pair = jnp.concatenate([o[(2 * pr) * Lq:(2 * pr + 1) * Lq], o[(2 * pr + 1) * Lq:(2 * pr + 2) * Lq]], axis=1)
            y_chunks.append(pair * _silu(z_ref[:, csl]))
    y_ref[...] = jnp.concatenate(y_chunks, axis=1).astype(y_ref.dtype)

    if valid == WINDOW:
        ko_ref[0] = k_cur
        vo_ref[0] = v_cur
    else:
        ko_ref[0, 0:WINDOW - valid, :] = k_prev[valid:, :]
        ko_ref[0, WINDOW - valid:WINDOW, :] = k_cur[0:valid, :]
        vo_ref[0, 0:WINDOW - valid, :] = v_prev[valid:, :]
        vo_ref[0, WINDOW - valid:WINDOW, :] = v_cur[0:valid, :]


def _swa(proj, cos, sin, sinks, st0, l0, acc, layer, n_seq, t_pad, Lq, valid, has_cache, table_per_block, y_dtype, S):
    nb = t_pad // Lq
    rowblk = lambda s, b: s * nb + b
    wide = lambda name: pl.BlockSpec((S * Lq, MIX), lambda s, b, cb=_colblock(name): (rowblk(s, b), cb))
    narrow = lambda name: pl.BlockSpec((S * Lq, LANE), lambda s, b, cb=_colblock(name): (rowblk(s, b), cb))
    table = pl.BlockSpec((Lq, LANE), (lambda s, b: (b, 0)) if table_per_block else (lambda s, b: (0, 0)))
    return _Part(
        functools.partial(_swa_kernel, Lq=Lq, valid=valid, has_cache=has_cache),
        operands=[proj, proj, proj, proj, cos, sin, sinks, *st0, *acc],
        in_specs=[wide("qb"), narrow("kb"), narrow("vb"), wide("zb"), table, table,
                  pl.BlockSpec((1, LANE), lambda s, b: (0, 0))] + [_state_spec(a, l0, S) for a in st0] + [_ANY] * len(acc),
        out_specs=[pl.BlockSpec((S * Lq, MIX), lambda s, b: (rowblk(s, b), 0))] + [_state_spec(a, layer, S) for a in acc],
        out_shape=[jax.ShapeDtypeStruct((n_seq * t_pad, MIX), y_dtype)]
        + [jax.ShapeDtypeStruct(a.shape, F32) for a in acc],
        n_alias=len(acc),
        kinds=["rows"] * 4 + ["whole"] * 3 + ["state"] * 2 + ["whole"] * 2 + ["rows"] + ["state"] * 2, rows=Lq)


_CONV_PAD = SUBLANE


def _spread_heads(x, e):
    lane = lax.broadcasted_iota(jnp.int32, x.shape, 1)
    x = jnp.where(lane < SSD_HEADS, x, 0.0)
    hi = x.astype(BF16)
    rest = x - hi.astype(F32)
    mid = rest.astype(BF16)
    lo = (rest - mid.astype(F32)).astype(BF16)
    return _dot(hi, e) + _dot(mid, e) + _dot(lo, e)


def _ssd_kernel(xbc_ref, z_ref, dt_ref, cw_ref, cb_ref, dtb_ref, alog_ref, dskip_ref, sn_ref, e_ref, conv0_ref, s0_ref,
                conv_alias, s_alias, y_ref, convo_ref, so_ref, full_ref, *, L, valid, first_chunk_init):
    del conv_alias, s_alias
    hist = SSD_CONV - 1
    if first_chunk_init:
        so_ref[...] = s0_ref[...]
        full_ref[_CONV_PAD - hist:_CONV_PAD, :] = conv0_ref[0]
        return

    full_ref[_CONV_PAD:_CONV_PAD + L, :] = xbc_ref[...]
    full = full_ref[...]
    acc = cb_ref[...] + full[_CONV_PAD:, :] * cw_ref[hist:hist + 1, :]
    for j in range(hist):
        acc = acc + pltpu.roll(full, hist - j, axis=0)[_CONV_PAD:, :] * cw_ref[j:j + 1, :]
    xc = _silu(acc)
    new_hist = full_ref[_CONV_PAD + valid - hist:_CONV_PAD + valid, :]
    convo_ref[0] = new_hist
    full_ref[_CONV_PAD - hist:_CONV_PAD, :] = new_hist

    nbc = SSD_GROUPS * SSD_DSTATE
    row_ok = lax.broadcasted_iota(jnp.int32, (L, LANE), 0) < valid
    dtv = jnp.where(row_ok, jax.nn.softplus(dt_ref[...] + dtb_ref[...]), 0.0)
    a = dtv * (-jnp.exp(alog_ref[...]) * LOG2E)
    cum = _cumsum_rows(a)
    cum_t = cum.T
    tr = lax.broadcasted_iota(jnp.int32, (L, L), 0)
    tc = lax.broadcasted_iota(jnp.int32, (L, L), 1)
    tril = tc <= tr

    groups, heads = range(SSD_GROUPS), range(SSD_HEADS)
    gp = SSD_HPG * SSD_DH
    bg = [xc[:, MIX + g * SSD_DSTATE:MIX + (g + 1) * SSD_DSTATE] for g in groups]
    cg = [xc[:, MIX + nbc + g * SSD_DSTATE:MIX + nbc + (g + 1) * SSD_DSTATE] for g in groups]
    st = [so_ref[0, g * SSD_HPG:(g + 1) * SSD_HPG].reshape(gp, SSD_DSTATE) for g in groups]
    cb = [_dot_nt(cg[g], bg[g]) for g in groups]
    cst = [_dot_nt(cg[g], st[g]) for g in groups]
    c_col = [cum[:, h:h + 1] for h in heads]
    c_last = [cum[L - 1:L, h:h + 1] for h in heads]
    lm = [jnp.where(tril, jnp.exp2(jnp.where(tril, c_col[h] - cum_t[h:h + 1, :], 0.0)), 0.0) for h in heads]

    spread = _spread_heads(jnp.concatenate([dtv, jnp.exp2(cum), jnp.exp2(cum[L - 1:L, :] - cum)], axis=0), e_ref[...])
    xs = xc[:, :MIX]
    xdt = xs * spread[0:L]
    xw = xdt * spread[2 * L:3 * L]
    low_half = lax.broadcasted_iota(jnp.int32, (L, LANE), 1) < SSD_DH
    y_in = []
    for pair in range(SSD_HEADS // 2):
        g = (2 * pair) // SSD_HPG
        x_pair = xdt[:, pair * LANE:(pair + 1) * LANE]
        y_in.append(jnp.where(low_half, _dot(cb[g] * lm[2 * pair], x_pair), _dot(cb[g] * lm[2 * pair + 1], x_pair)))
    y_state = jnp.concatenate(cst, axis=1) * spread[L:2 * L]
    upd = [_dot_tn(xw[:, g * gp:(g + 1) * gp], bg[g]) for g in groups]
    for h in heads:
        g, hh = divmod(h, SSD_HPG)
        so_ref[0, h] = jnp.exp2(c_last[h]) * st[g][hh * SSD_DH:(hh + 1) * SSD_DH] + upd[g][hh * SSD_DH:(hh + 1) * SSD_DH]
    yc = (jnp.concatenate(y_in, axis=1) + y_state + dskip_ref[...] * xs) * _silu(z_ref[...])
    y_ref[...] = (yc * lax.rsqrt(jnp.mean(yc * yc, axis=1, keepdims=True) + RMS_EPS) * sn_ref[...]).astype(y_ref.dtype)


def _ssd(proj, conv_w, conv_b, dt_bias, a_log, d_skip, snorm, st0, l0, acc, layer, n_seq, t_pad, L, valid, y_dtype, S):
    nc = t_pad // L
    rowblk = lambda s, c: s * nc + c
    vec = lambda n: pl.BlockSpec((1, n), lambda s, c: (0, 0))
    spread = (jnp.arange(MIX)[None, :] // SSD_DH == jnp.arange(LANE)[:, None]).astype(BF16)
    return _Part(
        functools.partial(_ssd_kernel, L=L, valid=valid),
        operands=[proj, proj, proj, conv_w, conv_b, dt_bias, a_log, d_skip, snorm, spread, *st0, *acc],
        in_specs=[pl.BlockSpec((S * L, SSD_CONV_DIM), lambda s, c, cb=_colblock("xbc"): (rowblk(s, c), cb)),
                  pl.BlockSpec((S * L, MIX), lambda s, c, cb=_colblock("zc"): (rowblk(s, c), cb)),
                  pl.BlockSpec((S * L, LANE), lambda s, c, cb=_colblock("dt"): (rowblk(s, c), cb)),
                  pl.BlockSpec((SSD_CONV, SSD_CONV_DIM), lambda s, c: (0, 0)),
                  vec(SSD_CONV_DIM), vec(LANE), vec(LANE), vec(MIX), vec(MIX),
                  pl.BlockSpec((LANE, MIX), lambda s, c: (0, 0))]
        + [_state_spec(a, l0, S) for a in st0] + [_ANY] * len(acc),
        out_specs=[pl.BlockSpec((S * L, MIX), lambda s, c: (rowblk(s, c), 0))] + [_state_spec(a, layer, S) for a in acc],
        out_shape=[jax.ShapeDtypeStruct((n_seq * t_pad, MIX), y_dtype)]
        + [jax.ShapeDtypeStruct(a.shape, F32) for a in acc],
        n_alias=len(acc),
        kinds=["rows"] * 3 + ["whole"] * 7 + ["state"] * 2 + ["whole"] * 2 + ["rows"] + ["state"] * 2, rows=L,
        scratch=[pltpu.VMEM((_CONV_PAD + L, SSD_CONV_DIM), F32)])


_POOL_PAD = 2 * SUBLANE
assert all(w & (w - 1) == 0 for w in POOL_WINDOWS) and POOL_HIST < _POOL_PAD


def _pool_kernel(u_ref, z_ref, lin_ref, scale_ref, p0_ref, p_alias, y_ref, po_ref, full_ref, *, L, valid, n_hist,
                 first_chunk_init):
    del p_alias
    if first_chunk_init:
        full_ref[_POOL_PAD - POOL_HIST:_POOL_PAD, :] = p0_ref[0]
        return
    step = pl.program_id(1)

    full_ref[_POOL_PAD:_POOL_PAD + L, :] = u_ref[...]
    pos = n_hist + step * L + lax.broadcasted_iota(jnp.int32, (L, 1), 0)
    ys = []
    for g, w in enumerate(POOL_WINDOWS):
        sl = slice(g * POOL_GC, (g + 1) * POOL_GC)
        rows = full_ref[:, sl]
        tot, span = rows, 1
        while span < w:
            tot = tot + pltpu.roll(tot, span, axis=0)
            span *= 2
        cur = rows[_POOL_PAD:]
        cnt = jnp.minimum(pos + 1, w).astype(F32)
        d = tot[_POOL_PAD:] / cnt - cur
        ys.append(_dot(d, lin_ref[g]) * scale_ref[:, sl] * _silu(z_ref[:, sl]))
    y_ref[...] = jnp.concatenate(ys, axis=1).astype(y_ref.dtype)
    new_hist = full_ref[_POOL_PAD + valid - POOL_HIST:_POOL_PAD + valid, :]
    po_ref[0] = new_hist
    full_ref[_POOL_PAD - POOL_HIST:_POOL_PAD, :] = new_hist


def _pool(proj, lin, scale, st0, l0, acc, layer, n_seq, t_pad, L, valid, n_hist, y_dtype, S):
    nt = t_pad // L
    rowblk = lambda s, c: s * nt + c
    return _Part(
        functools.partial(_pool_kernel, L=L, valid=valid, n_hist=n_hist),
        operands=[proj, proj, lin, scale, *st0, *acc],
        in_specs=[pl.BlockSpec((S * L, MIX), lambda s, c, cb=_colblock("ud"): (rowblk(s, c), cb)),
                  pl.BlockSpec((S * L, MIX), lambda s, c, cb=_colblock("zd"): (rowblk(s, c), cb)),
                  pl.BlockSpec((len(POOL_WINDOWS), POOL_GC, POOL_GC), lambda s, c: (0, 0, 0)),
                  pl.BlockSpec((1, MIX), lambda s, c: (0, 0))] + [_state_spec(a, l0, S) for a in st0] + [_ANY] * len(acc),
        out_specs=[pl.BlockSpec((S * L, MIX), lambda s, c: (rowblk(s, c), 0))] + [_state_spec(a, layer, S) for a in acc],
        out_shape=[jax.ShapeDtypeStruct((n_seq * t_pad, MIX), y_dtype)]
        + [jax.ShapeDtypeStruct(a.shape, F32) for a in acc],
        n_alias=len(acc),
        kinds=["rows"] * 2 + ["whole"] * 2 + ["state", "whole", "rows", "state"], rows=L,
        scratch=[pltpu.VMEM((_POOL_PAD + L, MIX), F32)])


_REF_SIZES = (MIX, MIX, MIX, MIX, MIX, MLSTM_HEADS, MLSTM_HEADS,
              SWA_HEADS * SWA_DH, SWA_KV_HEADS * SWA_DH, SWA_KV_HEADS * SWA_DH, MIX,
              MIX, SSD_CONV_DIM, SSD_HEADS, MIX, MIX)
_REF_NAMES = ("qa", "ka", "va", "oa", "za", "ia", "fa", "qb", "kb", "vb", "zb", "zc", "xbc", "dt", "ud", "zd")


def _regroup_plan():
    ref_off, o = {}, 0
    for name, size in zip(_REF_NAMES, _REF_SIZES):
        ref_off[name] = o
        o += size
    src = []
    for name, (off, width) in _COL.items():
        start = ref_off["ia"] if name == "gate" else ref_off[name]
        src += [start + b * LANE for b in range(width // LANE)]
    assert len(src) == NPROJ // LANE and all(s % SUBLANE == 0 and s + LANE <= o for s in src)
    return src


_REGROUP_BLOCKS = 4


def _regroup_kernel(src_tab, *refs):
    del src_tab
    o_ref = refs[-1]
    for i, wt_ref in enumerate(refs[:-1]):
        o_ref[:, i * LANE:(i + 1) * LANE] = wt_ref[...].T.astype(BF16)


def _regroup_w_in(w_in):
    depth, d_in, _ = w_in.shape
    w_t = jnp.transpose(w_in, (0, 2, 1))
    src = jnp.asarray([s // SUBLANE for s in _regroup_plan()], jnp.int32)
    return pl.pallas_call(
        _regroup_kernel,
        grid_spec=pltpu.PrefetchScalarGridSpec(
            num_scalar_prefetch=1,
            grid=(depth, NPROJ // (_REGROUP_BLOCKS * LANE)),
            in_specs=[pl.BlockSpec((None, pl.Element(LANE), pl.Element(d_in)),
                                   lambda l, t, src, i=i: (l, src[t * _REGROUP_BLOCKS + i] * SUBLANE, 0))
                      for i in range(_REGROUP_BLOCKS)],
            out_specs=pl.BlockSpec((None, d_in, _REGROUP_BLOCKS * LANE), lambda l, t, src: (l, 0, t))),
        out_shape=jax.ShapeDtypeStruct((depth, d_in, NPROJ), BF16),
        compiler_params=_params(("arbitrary", "arbitrary"), 48),
        name="regroup",
    )(src, *([w_t] * _REGROUP_BLOCKS))


def _pad_lanes(v, n=LANE):
    return jnp.pad(v, (0, n - v.shape[0])).reshape(1, n)


def _rope_tables(pos):
    half = SWA_DH // 2
    inv = ROPE_THETA ** (-jnp.arange(half, dtype=F32) / half)
    ang = pos.astype(F32)[:, None] * inv[None, :]
    cos, sin = jnp.cos(ang), jnp.sin(ang)
    reps = LANE // SWA_DH
    return jnp.tile(jnp.concatenate([cos, cos], axis=1), (1, reps)), jnp.tile(jnp.concatenate([-sin, sin], axis=1), (1, reps))


class _Path:
    def __init__(self, n_seq, t_pad, valid, chunk, seqs, has_cache, n_hist, tm_in, tn_in, tm_out, tn_out, y_dtype):
        self.n_seq, self.t_pad, self.valid, self.chunk, self.seqs = n_seq, t_pad, valid, chunk, seqs
        self.has_cache, self.n_hist, self.y_dtype = has_cache, n_hist, y_dtype
        self.tm_in, self.tn_in, self.tm_out, self.tn_out = tm_in, tn_in, tm_out, tn_out


def _layer(x, layer, prm, states, l0, acc, tables, path):
    (g_pre, g_post, w_in, w_out, gate_bias, mnorm, sinks, conv_w, conv_b, dt_bias, a_log, d_skip, snorm,
     pool_lin, pool_scale) = prm
    c0, n0, m0, kc, vc, s0, conv0, p0 = states
    ca, na, ma, ka, va, sa, conva, pa = acc
    cos, sin = tables
    p = path
    proj = _inproj(x, g_pre, w_in, layer, p.tm_in, p.tn_in)
    L, valid, S = p.chunk, min(p.valid, p.chunk), p.seqs
    assert S == 1 or p.t_pad == L
    parts = [
        _mlstm(proj, gate_bias, mnorm, (c0, n0, m0), l0, (ca, na, ma), layer, p.n_seq, p.t_pad, L, valid, p.y_dtype, S),
        _swa(proj, cos, sin, sinks, (kc, vc), l0, (ka, va), layer, p.n_seq, p.t_pad, L, valid,
             p.has_cache, not p.has_cache, p.y_dtype, S),
        _ssd(proj, conv_w, conv_b, dt_bias, a_log, d_skip, snorm, (conv0, s0), l0, (conva, sa), layer,
             p.n_seq, p.t_pad, L, valid, p.y_dtype, S),
        _pool(proj, pool_lin, pool_scale, (p0,), l0, (pa,), layer, p.n_seq, p.t_pad, L, valid, p.n_hist, p.y_dtype, S)]
    (ya, ca, na, ma), (yb, ka, va), (yc, conva, sa), (yd, pa) = _fused_call(
        parts, (p.n_seq // S, p.t_pad // L), "mixers", 48, S)
    x = _outproj((ya, yb, yc, yd), w_out, layer, x, g_post, p.tm_out, p.tn_out)
    return x, (ca, na, ma, ka, va, sa, conva, pa)


def kernel(x_prompt, x_sample, state_mlstm_C, state_mlstm_n, state_mlstm_m, cache_swa_k, cache_swa_v, state_ssd,
           state_ssd_conv, state_pool, norm_pre, norm_post, w_in, w_out, mlstm_b_i, mlstm_b_f, mlstm_norm,
           swa_sinks, ssd_conv_w, ssd_conv_b, ssd_dt_bias, ssd_A_log, ssd_D, ssd_norm, pool_lin, pool_scale):
    bp, seq, _ = x_prompt.shape
    bs, dec_seq, _ = x_sample.shape
    t_s = SUBLANE * pl.cdiv(dec_seq, SUBLANE)

    assert WINDOW == SSD_CHUNK == 2 * MLSTM_CHUNK
    prompt = _Path(bp, seq, seq, WINDOW, 1, False, 0, 1024, 1280, 1024, 512, BF16)
    sample = _Path(bs, t_s, dec_seq, t_s, 4, True, POOL_HIST, bs * t_s, 1280, bs * t_s, 512, F32)

    w_in_r = _regroup_w_in(w_in)
    w_out_b = w_out.astype(BF16)

    xp = x_prompt.reshape(bp * seq, D_MODEL)
    xs = jnp.pad(x_sample, ((0, 0), (0, t_s - dec_seq), (0, 0))).reshape(bs * t_s, D_MODEL)

    tab_p = _rope_tables(jnp.arange(seq))
    tab_s = _rope_tables(PAST_LEN + jnp.arange(t_s))

    def state_shapes(layers, b):
        return ((layers, b, MLSTM_HEADS, MLSTM_DH, MLSTM_DH), (layers, b, MLSTM_HEADS, LANE), (layers, b, MLSTM_HEADS, LANE),
                (layers, b, WINDOW, LANE), (layers, b, WINDOW, LANE),
                (layers, b, SSD_HEADS, SSD_DH, SSD_DSTATE), (layers, b, SSD_CONV - 1, SSD_CONV_DIM),
                (layers, b, POOL_HIST, MIX))

    p_states0 = tuple(jnp.zeros(s, F32) for s in state_shapes(1, bp))
    s_states0 = (state_mlstm_C, state_mlstm_n,
                 jnp.broadcast_to(state_mlstm_m[..., None], (DEPTH, bs, MLSTM_HEADS, LANE)),
                 cache_swa_k.reshape(DEPTH, bs, WINDOW, LANE), cache_swa_v.reshape(DEPTH, bs, WINDOW, LANE),
                 state_ssd, state_ssd_conv, state_pool)
    p_acc = tuple(jnp.zeros(s, F32) for s in state_shapes(DEPTH, bp))
    s_acc = tuple(jnp.zeros(s, F32) for s in state_shapes(DEPTH, bs))

    for l in range(DEPTH):
        gate_bias = _pad_lanes(jnp.concatenate([mlstm_b_i[l], mlstm_b_f[l]]))
        prm = (norm_pre[l].reshape(1, D_MODEL), norm_post[l].reshape(1, D_MODEL), w_in_r, w_out_b,
               gate_bias, mlstm_norm[l].reshape(1, MIX), _pad_lanes(swa_sinks[l]),
               ssd_conv_w[l], ssd_conv_b[l].reshape(1, SSD_CONV_DIM), _pad_lanes(ssd_dt_bias[l]),
               _pad_lanes(ssd_A_log[l]), jnp.repeat(ssd_D[l], SSD_DH).reshape(1, MIX), ssd_norm[l].reshape(1, MIX),
               pool_lin[l], pool_scale[l].reshape(1, MIX))
        xp, p_acc = _layer(xp, l, prm, p_states0, 0, p_acc, tab_p, prompt)
        xs, s_acc = _layer(xs, l, prm, s_states0, l, s_acc, tab_s, sample)

    def finish(acc, b):
        c, n, m, k, v, s, conv, pool = acc
        kv_shape = (DEPTH, b, WINDOW, SWA_KV_HEADS, SWA_DH)
        return c, n, m[..., 0], k.reshape(kv_shape), v.reshape(kv_shape), s, conv, pool

    y_prompt = xp.reshape(bp, seq, D_MODEL)
    y_sample = xs.reshape(bs, t_s, D_MODEL)[:, :dec_seq]
    return (y_prompt, y_sample) + finish(p_acc, bp) + finish(s_acc, bs)
```

```python
import functools
import math

import jax
import jax.numpy as jnp
from jax import lax
from jax.experimental import pallas as pl
from jax.experimental.pallas import tpu as pltpu

F32 = jnp.float32
BF16 = jnp.bfloat16

D_MODEL = 4096
DEPTH = 4
PAST_LEN = 8192
MIX = D_MODEL // 4
MLSTM_DH = 128
MLSTM_HEADS = MIX // MLSTM_DH
MLSTM_CHUNK = 64
GATE_CAP = 15.0
SWA_DH = 64
SWA_HEADS = MIX // SWA_DH
SWA_KV_HEADS = 2
SWA_GROUP = SWA_HEADS // SWA_KV_HEADS
WINDOW = 128
ROPE_THETA = 10000.0
SSD_DH = 64
SSD_HEADS = MIX // SSD_DH
SSD_GROUPS = 4
SSD_HPG = SSD_HEADS // SSD_GROUPS
SSD_DSTATE = 128
SSD_CONV = 4
SSD_CONV_DIM = MIX + 2 * SSD_GROUPS * SSD_DSTATE
SSD_CHUNK = 128
POOL_WINDOWS = (2, 4, 8, 16)
POOL_GC = MIX // len(POOL_WINDOWS)
POOL_HIST = max(POOL_WINDOWS) - 1
RMS_EPS = 1e-6

LANE = 128
SUBLANE = 8
NEG = -1e30

_COL = {}
_off = 0
for _name, _width in (("qa", MIX), ("ka", MIX), ("va", MIX), ("oa", MIX), ("za", MIX),
                      ("qb", MIX), ("zb", MIX), ("zc", MIX), ("xbc", SSD_CONV_DIM),
                      ("ud", MIX), ("zd", MIX),
                      ("gate", LANE), ("kb", LANE), ("vb", LANE), ("dt", LANE)):
    _COL[_name] = (_off, _width)
    _off += _width
NPROJ = _off


def _colblock(name):
    off, width = _COL[name]
    assert off % width == 0
    return off // width


LOG2E = math.log2(math.e)


def _sigmoid(x):
    return 0.5 * jnp.tanh(0.5 * x) + 0.5


def _silu(x):
    half = 0.5 * x
    return half + half * jnp.tanh(half)


def _dot(a, b):
    return jnp.dot(a, b, preferred_element_type=F32)


def _dot_nt(a, b):
    return lax.dot_general(a, b, (((1,), (1,)), ((), ())), preferred_element_type=F32)


def _dot_tn(a, b):
    return lax.dot_general(a, b, (((0,), (0,)), ((), ())), preferred_element_type=F32)


def _cumsum_rows(x):
    n = x.shape[0]
    r = lax.broadcasted_iota(jnp.int32, (n, n), 0)
    c = lax.broadcasted_iota(jnp.int32, (n, n), 1)
    tri = (c <= r).astype(F32)
    return jnp.dot(tri, x, preferred_element_type=F32, precision=lax.Precision.HIGHEST)


def _params(sem, vmem_mb):
    return pltpu.CompilerParams(dimension_semantics=sem, vmem_limit_bytes=vmem_mb * 1024 * 1024)


_ANY = pl.BlockSpec(memory_space=pl.ANY)
_DONE = object()


class _Part:
    def __init__(self, body, operands, in_specs, out_specs, out_shape, n_alias, kinds, rows, scratch=()):
        self.body, self.operands, self.in_specs, self.out_specs = body, operands, in_specs, out_specs
        self.out_shape, self.n_alias, self.scratch = out_shape, n_alias, list(scratch)
        self.kinds, self.rows = kinds, rows
        assert len(kinds) == len(operands) + len(out_shape)


def _fused_call(parts, grid, name, vmem_mb, seqs_per_step=1):
    n_in = [len(p.operands) for p in parts]
    n_out = [len(p.out_shape) for p in parts]
    n_scr = [len(p.scratch) * seqs_per_step for p in parts]

    def view(ref, kind, rows, q):
        if kind == "rows":
            return ref.at[pl.ds(q * rows, rows)]
        return ref.at[pl.ds(q, 1)] if kind == "state" else ref

    def body(*refs):
        ins, outs, scr = refs[:sum(n_in)], refs[sum(n_in):sum(n_in) + sum(n_out)], refs[sum(n_in) + sum(n_out):]

        def run(first_chunk_init):
            pending, i, o, s = [], 0, 0, 0
            for k, p in enumerate(parts):
                io = list(ins[i:i + n_in[k]]) + list(outs[o:o + n_out[k]])
                per_seq = n_scr[k] // seqs_per_step
                for q in range(seqs_per_step):
                    pending.append(p.body(*[view(r, kind, p.rows, q) for r, kind in zip(io, p.kinds)],
                                          *scr[s + q * per_seq:s + (q + 1) * per_seq],
                                          first_chunk_init=first_chunk_init))
                i, o, s = i + n_in[k], o + n_out[k], s + n_scr[k]
            while pending:
                pending = [g for g in pending if next(g, _DONE) is not _DONE]

        pl.when(pl.program_id(len(grid) - 1) == 0)(functools.partial(run, True))
        run(False)

    aliases, i, o = {}, 0, 0
    for k, p in enumerate(parts):
        for a in range(p.n_alias):
            aliases[i + n_in[k] - p.n_alias + a] = o + 1 + a
        i, o = i + n_in[k], o + n_out[k]
    flat = pl.pallas_call(
        body,
        grid=grid,
        in_specs=[s for p in parts for s in p.in_specs],
        out_specs=[s for p in parts for s in p.out_specs],
        out_shape=[s for p in parts for s in p.out_shape],
        scratch_shapes=[s for p in parts for s in p.scratch * seqs_per_step],
        input_output_aliases=aliases,
        compiler_params=_params(("arbitrary",) * len(grid), vmem_mb),
        name=name,
    )(*[x for p in parts for x in p.operands])
    out, o = [], 0
    for k in range(len(parts)):
        out.append(list(flat[o:o + n_out[k]]))
        o += n_out[k]
    return out


def _state_spec(stacked, layer, seqs=1):
    tail = stacked.shape[2:]
    return pl.BlockSpec((None, seqs) + tail, lambda s, c: (layer, s) + (0,) * len(tail))


_NORM_STEPS = 8


def _inproj_kernel(x_ref, g_ref, w_ref, o_ref, h_ref):
    p = pl.program_id(0)
    j = pl.program_id(1)
    rows = x_ref.shape[0]

    def normalise_slice():
        rs = pl.ds(pl.multiple_of(jnp.minimum(j, _NORM_STEPS - 1) * rows, rows), rows)
        x = x_ref[...]
        ms = jnp.mean(x * x, axis=-1, keepdims=True)
        h_ref[p % 2, rs, :] = (x * lax.rsqrt(ms + RMS_EPS) * g_ref[...]).astype(BF16)

    pl.when(p == 0)(normalise_slice)

    @pl.when(p > 0)
    def _():
        normalise_slice()
        o_ref[...] = _dot(h_ref[(p - 1) % 2], w_ref[...])


def _inproj(x, g, w_all, layer, tm, tn):
    m = x.shape[0]
    n_row_tiles, n_col = m // tm, NPROJ // tn
    assert n_col >= _NORM_STEPS and tm % (_NORM_STEPS * 2 * SUBLANE) == 0
    col = lambda p, j: jnp.where(p > 0, j, 0)
    x_slice = lambda p, j: jnp.minimum(p, n_row_tiles - 1) * _NORM_STEPS + jnp.minimum(j, _NORM_STEPS - 1)
    return pl.pallas_call(
        _inproj_kernel,
        grid=(n_row_tiles + 1, n_col),
        in_specs=[pl.BlockSpec((tm // _NORM_STEPS, D_MODEL), lambda p, j: (x_slice(p, j), 0)),
                  pl.BlockSpec((1, D_MODEL), lambda p, j: (0, 0)),
                  pl.BlockSpec((None, D_MODEL, tn), lambda p, j: (layer, 0, col(p, j)))],
        out_specs=pl.BlockSpec((tm, tn), lambda p, j: (jnp.maximum(p - 1, 0), col(p, j))),
        out_shape=jax.ShapeDtypeStruct((m, NPROJ), F32),
        scratch_shapes=[pltpu.VMEM((2, tm, D_MODEL), BF16)],
        compiler_params=_params(("arbitrary", "arbitrary"), 58),
        name="inproj",
    )(x, g, w_all)


def _outproj_kernel(ya_ref, yb_ref, yc_ref, yd_ref, w_ref, x_ref, g_ref, o_ref, acc_ref, ss_ref, *, tn, n_row_tiles):
    i = pl.program_id(0)
    j = pl.program_id(1)
    slot = i % 2
    col = pl.ds(pl.multiple_of(j * tn, tn), tn)

    def finalize():
        rs = lax.rsqrt(ss_ref[1 - slot] * (1.0 / D_MODEL) + RMS_EPS)
        o_ref[...] = x_ref[...] + acc_ref[:, col] * rs * g_ref[...]

    def multiply():
        out = None
        for part, r in enumerate((ya_ref, yb_ref, yc_ref, yd_ref)):
            term = _dot(r[...].astype(BF16), w_ref[part * MIX:(part + 1) * MIX, :])
            out = term if out is None else out + term
        acc_ref[:, col] = out
        sq = jnp.sum(out * out, axis=1, keepdims=True)
        ss_ref[slot] = jnp.where(j == 0, sq, ss_ref[slot] + sq)

    pl.when(i == 0)(multiply)
    pl.when(i == n_row_tiles)(finalize)

    @pl.when(jnp.logical_and(i > 0, i < n_row_tiles))
    def _():
        finalize()
        multiply()


def _outproj(ys, w_all, layer, x, g, tm, tn):
    m = x.shape[0]
    n_row_tiles = m // tm
    yspec = pl.BlockSpec((tm, MIX), lambda i, j: (jnp.minimum(i, n_row_tiles - 1), 0))
    lagged = pl.BlockSpec((tm, tn), lambda i, j: (jnp.maximum(i - 1, 0), jnp.where(i > 0, j, 0)))
    return pl.pallas_call(
        functools.partial(_outproj_kernel, tn=tn, n_row_tiles=n_row_tiles),
        grid=(n_row_tiles + 1, D_MODEL // tn),
        in_specs=[yspec, yspec, yspec, yspec,
                  pl.BlockSpec((None, D_MODEL, tn), lambda i, j: (layer, 0, j)),
                  lagged,
                  pl.BlockSpec((1, tn), lambda i, j: (0, j))],
        out_specs=lagged,
        out_shape=jax.ShapeDtypeStruct((m, D_MODEL), F32),
        scratch_shapes=[pltpu.VMEM((tm, D_MODEL), F32), pltpu.VMEM((2, tm, 1), F32)],
        compiler_params=_params(("arbitrary", "arbitrary"), 56),
        name="outproj",
    )(*ys, w_all, x, g)


def _mlstm_kernel(q_ref, k_ref, v_ref, o_ref, z_ref, gate_ref, bias_ref, mn_ref, c0_ref, n0_ref, m0_ref,
                  c_alias, n_alias, m_alias, y_ref, c_ref, n_ref, m_ref, *, L, valid, first_chunk_init):
    del c_alias, n_alias, m_alias
    if first_chunk_init:
        c_ref[...] = c0_ref[...]
        n_ref[...] = n0_ref[...]
        m_ref[...] = m0_ref[...]
        return

    row = lax.broadcasted_iota(jnp.int32, (L, LANE), 0)
    row_ok = row < valid
    gates = GATE_CAP * jnp.tanh((gate_ref[...] + bias_ref[...]) / GATE_CAP)
    logf = jnp.where(row_ok, jax.nn.log_sigmoid(gates), 0.0)
    b_all = _cumsum_rows(pltpu.roll(logf, LANE - MLSTM_HEADS, axis=1))
    r_all = jnp.where(row_ok, gates - b_all, NEG)
    r_all_t = r_all.T
    yield

    tr = lax.broadcasted_iota(jnp.int32, (L, L), 0)
    tc = lax.broadcasted_iota(jnp.int32, (L, L), 1)
    tril = tc <= tr

    heads = range(MLSTM_HEADS)
    sl = [slice(h * MLSTM_DH, (h + 1) * MLSTM_DH) for h in heads]
    q = [q_ref[:, sl[h]] * (MLSTM_DH ** -0.5) for h in heads]
    k = [k_ref[:, sl[h]] for h in heads]
    v = [v_ref[:, sl[h]] for h in heads]
    c_prev = [c_ref[0, h] for h in heads]
    n_prev = [n_ref[0, h:h + 1, :] for h in heads]
    m_prev = [m_ref[0, h:h + 1, 0:1] for h in heads]
    qk = [_dot_nt(q[h], k[h]) for h in heads]
    qc = [_dot(q[h], c_prev[h]) for h in heads]
    qn = [jnp.sum(q[h] * n_prev[h], axis=1, keepdims=True) for h in heads]
    yield

    b_col = [b_all[:, h:h + 1] for h in heads]
    dmat = [jnp.where(tril, b_col[h] + r_all_t[h:h + 1, :], NEG) for h in heads]
    inter = [b_col[h] + m_prev[h] for h in heads]
    mt = [jnp.maximum(inter[h], jnp.max(dmat[h], axis=1, keepdims=True)) for h in heads]
    yield
    s = [jnp.exp(dmat[h] - mt[h]) * qk[h] for h in heads]
    sv = [_dot(s[h], v[h]) for h in heads]
    yield
    w_prev_t = [jnp.exp(inter[h] - mt[h]) for h in heads]
    den = [w_prev_t[h] * qn[h] + jnp.sum(s[h], axis=1, keepdims=True) for h in heads]
    hh = [(w_prev_t[h] * qc[h] + sv[h]) / jnp.maximum(jnp.abs(den[h]), jnp.exp(-mt[h])) for h in heads]
    yield

    b_last = [b_all[L - 1:L, h:h + 1] for h in heads]
    dec = [b_last[h] + r_all[:, h:h + 1] for h in heads]
    m_new = [jnp.maximum(b_last[h] + m_prev[h], jnp.max(dec[h], axis=0, keepdims=True)) for h in heads]
    w_prev = [jnp.exp(b_last[h] + m_prev[h] - m_new[h]) for h in heads]
    kw = [k[h] * jnp.exp(dec[h] - m_new[h]) for h in heads]
    c_new = [w_prev[h] * c_prev[h] + _dot_tn(kw[h], v[h]) for h in heads]
    n_new = [w_prev[h] * n_prev[h] + jnp.sum(kw[h], axis=0, keepdims=True) for h in heads]
    yield

    hn = [hh[h] * lax.rsqrt(jnp.mean(hh[h] * hh[h], axis=1, keepdims=True) + RMS_EPS) * mn_ref[:, sl[h]] for h in heads]
    ys = [hn[h] * _sigmoid(o_ref[:, sl[h]]) * _silu(z_ref[:, sl[h]]) for h in heads]
    yield

    y_ref[...] = jnp.concatenate(ys, axis=1).astype(y_ref.dtype)
    for h in heads:
        c_ref[0, h] = c_new[h]
    n_ref[0] = jnp.concatenate(n_new, axis=0)
    m_ref[0] = jnp.concatenate([jnp.broadcast_to(m_new[h], (1, LANE)) for h in heads], axis=0)


def _mlstm(proj, bias, mnorm, st0, l0, acc, layer, n_seq, t_pad, L, valid, y_dtype, S):
    nc = t_pad // L
    wide = lambda name: pl.BlockSpec((S * L, MIX), lambda s, c, cb=_colblock(name): (s * nc + c, cb))
    vec = lambda n: pl.BlockSpec((1, n), lambda s, c: (0, 0))
    return _Part(
        functools.partial(_mlstm_kernel, L=L, valid=valid),
        operands=[proj, proj, proj, proj, proj, proj, bias, mnorm, *st0, *acc],
        in_specs=[wide("qa"), wide("ka"), wide("va"), wide("oa"), wide("za"),
                  pl.BlockSpec((S * L, LANE), lambda s, c, cb=_colblock("gate"): (s * nc + c, cb)),
                  vec(LANE), vec(MIX)] + [_state_spec(a, l0, S) for a in st0] + [_ANY] * len(acc),
        out_specs=[pl.BlockSpec((S * L, MIX), lambda s, c: (s * nc + c, 0))] + [_state_spec(a, layer, S) for a in acc],
        out_shape=[jax.ShapeDtypeStruct((n_seq * t_pad, MIX), y_dtype)]
        + [jax.ShapeDtypeStruct(a.shape, F32) for a in acc],
        n_alias=len(acc),
        kinds=["rows"] * 6 + ["whole"] * 2 + ["state"] * 3 + ["whole"] * 3 + ["rows"] + ["state"] * 3, rows=L)


def _swa_kernel(q_ref, k_ref, v_ref, z_ref, cos_ref, sin_ref, sink_ref, kc_ref, vc_ref, k_alias, v_alias,
                y_ref, ko_ref, vo_ref, *, Lq, valid, has_cache, first_chunk_init):
    del k_alias, v_alias
    if first_chunk_init:
        ko_ref[...] = kc_ref[...]
        vo_ref[...] = vc_ref[...]
        return
    blk = pl.program_id(1)

    cos = cos_ref[...]
    sin = sin_ref[...]
    lane = lax.broadcasted_iota(jnp.int32, (Lq, LANE), 1)
    first_half = (lane % SWA_DH) < (SWA_DH // 2)

    def rope(x):
        partner = jnp.where(first_half, pltpu.roll(x, LANE - SWA_DH // 2, axis=1), pltpu.roll(x, SWA_DH // 2, axis=1))
        return x * cos + partner * sin

    k_cur = rope(k_ref[...])
    v_cur = v_ref[...]
    k_prev = ko_ref[0]
    v_prev = vo_ref[0]
    kk = jnp.concatenate([k_prev, k_cur], axis=0)
    vv = jnp.concatenate([v_prev, v_cur], axis=0)

    rows, nk = SWA_GROUP * Lq, WINDOW + Lq
    t = lax.broadcasted_iota(jnp.int32, (rows, nk), 0) & (Lq - 1)
    c = lax.broadcasted_iota(jnp.int32, (rows, nk), 1)
    prev_ok = jnp.logical_or(has_cache, blk > 0)
    mask = ((c < WINDOW) & (c > t) & prev_ok) | ((c >= WINDOW) & (c - WINDOW <= t) & (c - WINDOW < valid))

    yield
    pairs_per_group = SWA_GROUP // 2
    groups = range(SWA_KV_HEADS)
    gsl = [slice(g * SWA_DH, (g + 1) * SWA_DH) for g in groups]
    q_pairs = [[rope(q_ref[:, (g * pairs_per_group + p) * LANE:(g * pairs_per_group + p + 1) * LANE])
                * (SWA_DH ** -0.5 * LOG2E) for p in range(pairs_per_group)] for g in groups]
    q_st = [jnp.concatenate([q_pairs[g][h // 2][:, (h % 2) * SWA_DH:(h % 2 + 1) * SWA_DH]
                             for h in range(SWA_GROUP)], axis=0) for g in groups]
    sink = [jnp.concatenate([jnp.broadcast_to(sink_ref[0:1, g * SWA_GROUP + h:g * SWA_GROUP + h + 1], (Lq, 1))
                             for h in range(SWA_GROUP)], axis=0) * LOG2E for g in groups]
    yield
    s = [jnp.where(mask, _dot_nt(q_st[g], kk[:, gsl[g]]), NEG) for g in groups]
    mx = [jnp.maximum(jnp.max(s[g], axis=1, keepdims=True), sink[g]) for g in groups]
    yield
    p = [jnp.exp2(s[g] - mx[g]) for g in groups]
    den = [jnp.sum(p[g], axis=1, keepdims=True) + jnp.exp2(sink[g] - mx[g]) for g in groups]
    yield
    o = [_dot(p[g], vv[:, gsl[g]]) / den[g] for g in groups]
    yield
    y_chunks = []
    for g in groups:
        for pr in range(pairs_per_group):
            csl = slice((g * pairs_per_group + pr) * LANE, (g * pairs_per_group + pr + 1) * LANE)
            pair = jnp.concatenate([o[g][(2 * pr) * Lq:(2 * pr + 1) * Lq], o[g][(2 * pr + 1) * Lq:(2 * pr + 2) * Lq]],
                                   axis=1)
            y_chunks.append(pair * _silu(z_ref[:, csl]))
    y_ref[...] = jnp.concatenate(y_chunks, axis=1).astype(y_ref.dtype)

    if valid == WINDOW:
        ko_ref[0] = k_cur
        vo_ref[0] = v_cur
    else:
        ko_ref[0, 0:WINDOW - valid, :] = k_prev[valid:, :]
        ko_ref[0, WINDOW - valid:WINDOW, :] = k_cur[0:valid, :]
        vo_ref[0, 0:WINDOW - valid, :] = v_prev[valid:, :]
        vo_ref[0, WINDOW - valid:WINDOW, :] = v_cur[0:valid, :]


def _swa(proj, cos, sin, sinks, st0, l0, acc, layer, n_seq, t_pad, Lq, valid, has_cache, table_per_block, y_dtype, S):
    nb = t_pad // Lq
    rowblk = lambda s, b: s * nb + b
    wide = lambda name: pl.BlockSpec((S * Lq, MIX), lambda s, b, cb=_colblock(name): (rowblk(s, b), cb))
    narrow = lambda name: pl.BlockSpec((S * Lq, LANE), lambda s, b, cb=_colblock(name): (rowblk(s, b), cb))
    table = pl.BlockSpec((Lq, LANE), (lambda s, b: (b, 0)) if table_per_block else (lambda s, b: (0, 0)))
    return _Part(
        functools.partial(_swa_kernel, Lq=Lq, valid=valid, has_cache=has_cache),
        operands=[proj, proj, proj, proj, cos, sin, sinks, *st0, *acc],
        in_specs=[wide("qb"), narrow("kb"), narrow("vb"), wide("zb"), table, table,
                  pl.BlockSpec((1, LANE), lambda s, b: (0, 0))] + [_state_spec(a, l0, S) for a in st0] + [_ANY] * len(acc),
        out_specs=[pl.BlockSpec((S * Lq, MIX), lambda s, b: (rowblk(s, b), 0))] + [_state_spec(a, layer, S) for a in acc],
        out_shape=[jax.ShapeDtypeStruct((n_seq * t_pad, MIX), y_dtype)]
        + [jax.ShapeDtypeStruct(a.shape, F32) for a in acc],
        n_alias=len(acc),
        kinds=["rows"] * 4 + ["whole"] * 3 + ["state"] * 2 + ["whole"] * 2 + ["rows"] + ["state"] * 2, rows=Lq)


_CONV_PAD = SUBLANE


def _spread_heads(x, e):
    lane = lax.broadcasted_iota(jnp.int32, x.shape, 1)
    x = jnp.where(lane < SSD_HEADS, x, 0.0)
    hi = x.astype(BF16)
    rest = x - hi.astype(F32)
    mid = rest.astype(BF16)
    lo = (rest - mid.astype(F32)).astype(BF16)
    return _dot(hi, e) + _dot(mid, e) + _dot(lo, e)


def _ssd_kernel(xbc_ref, z_ref, dt_ref, cw_ref, cb_ref, dtb_ref, alog_ref, dskip_ref, sn_ref, e_ref, conv0_ref, s0_ref,
                conv_alias, s_alias, y_ref, convo_ref, so_ref, full_ref, *, L, valid, first_chunk_init):
    del conv_alias, s_alias
    hist = SSD_CONV - 1
    if first_chunk_init:
        so_ref[...] = s0_ref[...]
        full_ref[_CONV_PAD - hist:_CONV_PAD, :] = conv0_ref[0]
        return

    full_ref[_CONV_PAD:_CONV_PAD + L, :] = xbc_ref[...]
    full = full_ref[...]
    acc = cb_ref[...] + full[_CONV_PAD:, :] * cw_ref[hist:hist + 1, :]
    for j in range(hist):
        acc = acc + pltpu.roll(full, hist - j, axis=0)[_CONV_PAD:, :] * cw_ref[j:j + 1, :]
    xc = _silu(acc)
    new_hist = full_ref[_CONV_PAD + valid - hist:_CONV_PAD + valid, :]
    convo_ref[0] = new_hist
    full_ref[_CONV_PAD - hist:_CONV_PAD, :] = new_hist
    yield

    nbc = SSD_GROUPS * SSD_DSTATE
    row_ok = lax.broadcasted_iota(jnp.int32, (L, LANE), 0) < valid
    dtv = jnp.where(row_ok, jax.nn.softplus(dt_ref[...] + dtb_ref[...]), 0.0)
    a = dtv * (-jnp.exp(alog_ref[...]) * LOG2E)
    cum = _cumsum_rows(a)
    cum_t = cum.T
    tr = lax.broadcasted_iota(jnp.int32, (L, L), 0)
    tc = lax.broadcasted_iota(jnp.int32, (L, L), 1)
    tril = tc <= tr
    yield

    groups, heads = range(SSD_GROUPS), range(SSD_HEADS)
    gp = SSD_HPG * SSD_DH
    bg = [xc[:, MIX + g * SSD_DSTATE:MIX + (g + 1) * SSD_DSTATE] for g in groups]
    cg = [xc[:, MIX + nbc + g * SSD_DSTATE:MIX + nbc + (g + 1) * SSD_DSTATE] for g in groups]
    st = [so_ref[0, g * SSD_HPG:(g + 1) * SSD_HPG].reshape(gp, SSD_DSTATE) for g in groups]
    cb = [_dot_nt(cg[g], bg[g]) for g in groups]
    cst = [_dot_nt(cg[g], st[g]) for g in groups]
    yield
    c_col = [cum[:, h:h + 1] for h in heads]
    c_last = [cum[L - 1:L, h:h + 1] for h in heads]
    lm = [jnp.where(tril, jnp.exp2(jnp.where(tril, c_col[h] - cum_t[h:h + 1, :], 0.0)), 0.0) for h in heads]

    spread = _spread_heads(jnp.concatenate([dtv, jnp.exp2(cum), jnp.exp2(cum[L - 1:L, :] - cum)], axis=0), e_ref[...])
    xs = xc[:, :MIX]
    xdt = xs * spread[0:L]
    xw = xdt * spread[2 * L:3 * L]
    yield
    low_half = lax.broadcasted_iota(jnp.int32, (L, LANE), 1) < SSD_DH
    y_in = []
    for pair in range(SSD_HEADS // 2):
        g = (2 * pair) // SSD_HPG
        x_pair = xdt[:, pair * LANE:(pair + 1) * LANE]
        y_in.append(jnp.where(low_half, _dot(cb[g] * lm[2 * pair], x_pair), _dot(cb[g] * lm[2 * pair + 1], x_pair)))
    y_state = jnp.concatenate(cst, axis=1) * spread[L:2 * L]
    yield
    upd = [_dot_tn(xw[:, g * gp:(g + 1) * gp], bg[g]) for g in groups]
    for h in heads:
        g, hh = divmod(h, SSD_HPG)
        so_ref[0, h] = jnp.exp2(c_last[h]) * st[g][hh * SSD_DH:(hh + 1) * SSD_DH] + upd[g][hh * SSD_DH:(hh + 1) * SSD_DH]
    yc = (jnp.concatenate(y_in, axis=1) + y_state + dskip_ref[...] * xs) * _silu(z_ref[...])
    y_ref[...] = (yc * lax.rsqrt(jnp.mean(yc * yc, axis=1, keepdims=True) + RMS_EPS) * sn_ref[...]).astype(y_ref.dtype)


def _ssd(proj, conv_w, conv_b, dt_bias, a_log, d_skip, snorm, st0, l0, acc, layer, n_seq, t_pad, L, valid, y_dtype, S):
    nc = t_pad // L
    rowblk = lambda s, c: s * nc + c
    vec = lambda n: pl.BlockSpec((1, n), lambda s, c: (0, 0))
    spread = (jnp.arange(MIX)[None, :] // SSD_DH == jnp.arange(LANE)[:, None]).astype(BF16)
    return _Part(
        functools.partial(_ssd_kernel, L=L, valid=valid),
        operands=[proj, proj, proj, conv_w, conv_b, dt_bias, a_log, d_skip, snorm, spread, *st0, *acc],
        in_specs=[pl.BlockSpec((S * L, SSD_CONV_DIM), lambda s, c, cb=_colblock("xbc"): (rowblk(s, c), cb)),
                  pl.BlockSpec((S * L, MIX), lambda s, c, cb=_colblock("zc"): (rowblk(s, c), cb)),
                  pl.BlockSpec((S * L, LANE), lambda s, c, cb=_colblock("dt"): (rowblk(s, c), cb)),
                  pl.BlockSpec((SSD_CONV, SSD_CONV_DIM), lambda s, c: (0, 0)),
                  vec(SSD_CONV_DIM), vec(LANE), vec(LANE), vec(MIX), vec(MIX),
                  pl.BlockSpec((LANE, MIX), lambda s, c: (0, 0))]
        + [_state_spec(a, l0, S) for a in st0] + [_ANY] * len(acc),
        out_specs=[pl.BlockSpec((S * L, MIX), lambda s, c: (rowblk(s, c), 0))] + [_state_spec(a, layer, S) for a in acc],
        out_shape=[jax.ShapeDtypeStruct((n_seq * t_pad, MIX), y_dtype)]
        + [jax.ShapeDtypeStruct(a.shape, F32) for a in acc],
        n_alias=len(acc),
        kinds=["rows"] * 3 + ["whole"] * 7 + ["state"] * 2 + ["whole"] * 2 + ["rows"] + ["state"] * 2, rows=L,
        scratch=[pltpu.VMEM((_CONV_PAD + L, SSD_CONV_DIM), F32)])


_POOL_PAD = 2 * SUBLANE
assert all(w & (w - 1) == 0 for w in POOL_WINDOWS) and POOL_HIST < _POOL_PAD


def _pool_kernel(u_ref, z_ref, lin_ref, scale_ref, p0_ref, p_alias, y_ref, po_ref, full_ref, *, L, valid, n_hist,
                 first_chunk_init):
    del p_alias
    if first_chunk_init:
        full_ref[_POOL_PAD - POOL_HIST:_POOL_PAD, :] = p0_ref[0]
        return
    step = pl.program_id(1)

    full_ref[_POOL_PAD:_POOL_PAD + L, :] = u_ref[...]
    pos = n_hist + step * L + lax.broadcasted_iota(jnp.int32, (L, 1), 0)
    ys = []
    for g, w in enumerate(POOL_WINDOWS):
        sl = slice(g * POOL_GC, (g + 1) * POOL_GC)
        rows = full_ref[:, sl]
        tot, span = rows, 1
        while span < w:
            tot = tot + pltpu.roll(tot, span, axis=0)
            span *= 2
        cur = rows[_POOL_PAD:]
        cnt = jnp.minimum(pos + 1, w).astype(F32)
        d = tot[_POOL_PAD:] / cnt - cur
        ys.append(_dot(d, lin_ref[g]) * scale_ref[:, sl] * _silu(z_ref[:, sl]))
        yield
    y_ref[...] = jnp.concatenate(ys, axis=1).astype(y_ref.dtype)
    new_hist = full_ref[_POOL_PAD + valid - POOL_HIST:_POOL_PAD + valid, :]
    po_ref[0] = new_hist
    full_ref[_POOL_PAD - POOL_HIST:_POOL_PAD, :] = new_hist


def _pool(proj, lin, scale, st0, l0, acc, layer, n_seq, t_pad, L, valid, n_hist, y_dtype, S):
    nt = t_pad // L
    rowblk = lambda s, c: s * nt + c
    return _Part(
        functools.partial(_pool_kernel, L=L, valid=valid, n_hist=n_hist),
        operands=[proj, proj, lin, scale, *st0, *acc],
        in_specs=[pl.BlockSpec((S * L, MIX), lambda s, c, cb=_colblock("ud"): (rowblk(s, c), cb)),
                  pl.BlockSpec((S * L, MIX), lambda s, c, cb=_colblock("zd"): (rowblk(s, c), cb)),
                  pl.BlockSpec((len(POOL_WINDOWS), POOL_GC, POOL_GC), lambda s, c: (0, 0, 0)),
                  pl.BlockSpec((1, MIX), lambda s, c: (0, 0))] + [_state_spec(a, l0, S) for a in st0] + [_ANY] * len(acc),
        out_specs=[pl.BlockSpec((S * L, MIX), lambda s, c: (rowblk(s, c), 0))] + [_state_spec(a, layer, S) for a in acc],
        out_shape=[jax.ShapeDtypeStruct((n_seq * t_pad, MIX), y_dtype)]
        + [jax.ShapeDtypeStruct(a.shape, F32) for a in acc],
        n_alias=len(acc),
        kinds=["rows"] * 2 + ["whole"] * 2 + ["state", "whole", "rows", "state"], rows=L,
        scratch=[pltpu.VMEM((_POOL_PAD + L, MIX), F32)])


_REF_SIZES = (MIX, MIX, MIX, MIX, MIX, MLSTM_HEADS, MLSTM_HEADS,
              SWA_HEADS * SWA_DH, SWA_KV_HEADS * SWA_DH, SWA_KV_HEADS * SWA_DH, MIX,
              MIX, SSD_CONV_DIM, SSD_HEADS, MIX, MIX)
_REF_NAMES = ("qa", "ka", "va", "oa", "za", "ia", "fa", "qb", "kb", "vb", "zb", "zc", "xbc", "dt", "ud", "zd")


def _regroup_plan():
    ref_off, o = {}, 0
    for name, size in zip(_REF_NAMES, _REF_SIZES):
        ref_off[name] = o
        o += size
    src = []
    for name, (off, width) in _COL.items():
        start = ref_off["ia"] if name == "gate" else ref_off[name]
        src += [start + b * LANE for b in range(width // LANE)]
    assert len(src) == NPROJ // LANE and all(s % SUBLANE == 0 and s + LANE <= o for s in src)
    return src


_REGROUP_BLOCKS = 4


def _regroup_kernel(src_tab, *refs):
    del src_tab
    o_ref = refs[-1]
    for i, wt_ref in enumerate(refs[:-1]):
        o_ref[:, i * LANE:(i + 1) * LANE] = wt_ref[...].T.astype(BF16)


def _regroup_w_in(w_in):
    depth, d_in, _ = w_in.shape
    w_t = jnp.transpose(w_in, (0, 2, 1))
    src = jnp.asarray([s // SUBLANE for s in _regroup_plan()], jnp.int32)
    return pl.pallas_call(
        _regroup_kernel,
        grid_spec=pltpu.PrefetchScalarGridSpec(
            num_scalar_prefetch=1,
            grid=(depth, NPROJ // (_REGROUP_BLOCKS * LANE)),
            in_specs=[pl.BlockSpec((None, pl.Element(LANE), pl.Element(d_in)),
                                   lambda l, t, src, i=i: (l, src[t * _REGROUP_BLOCKS + i] * SUBLANE, 0))
                      for i in range(_REGROUP_BLOCKS)],
            out_specs=pl.BlockSpec((None, d_in, _REGROUP_BLOCKS * LANE), lambda l, t, src: (l, 0, t))),
        out_shape=jax.ShapeDtypeStruct((depth, d_in, NPROJ), BF16),
        compiler_params=_params(("arbitrary", "arbitrary"), 48),
        name="regroup",
    )(src, *([w_t] * _REGROUP_BLOCKS))


def _pad_lanes(v, n=LANE):
    return jnp.pad(v, (0, n - v.shape[0])).reshape(1, n)


def _rope_tables(pos):
    half = SWA_DH // 2
    inv = ROPE_THETA ** (-jnp.arange(half, dtype=F32) / half)
    ang = pos.astype(F32)[:, None] * inv[None, :]
    cos, sin = jnp.cos(ang), jnp.sin(ang)
    reps = LANE // SWA_DH
    return jnp.tile(jnp.concatenate([cos, cos], axis=1), (1, reps)), jnp.tile(jnp.concatenate([-sin, sin], axis=1), (1, reps))


class _Path:
    def __init__(self, n_seq, t_pad, valid, chunk, seqs, has_cache, n_hist, tm_in, tn_in, tm_out, tn_out, y_dtype):
        self.n_seq, self.t_pad, self.valid, self.chunk, self.seqs = n_seq, t_pad, valid, chunk, seqs
        self.has_cache, self.n_hist, self.y_dtype = has_cache, n_hist, y_dtype
        self.tm_in, self.tn_in, self.tm_out, self.tn_out = tm_in, tn_in, tm_out, tn_out


_MIXER_ORDER = (0, 1, 2, 3)


def _layer(x, layer, prm, states, l0, acc, tables, path):
    (g_pre, g_post, w_in, w_out, gate_bias, mnorm, sinks, conv_w, conv_b, dt_bias, a_log, d_skip, snorm,
     pool_lin, pool_scale) = prm
    c0, n0, m0, kc, vc, s0, conv0, p0 = states
    ca, na, ma, ka, va, sa, conva, pa = acc
    cos, sin = tables
    p = path
    proj = _inproj(x, g_pre, w_in, layer, p.tm_in, p.tn_in)
    L, valid, S = p.chunk, min(p.valid, p.chunk), p.seqs
    assert S == 1 or p.t_pad == L
    parts = [
        _mlstm(proj, gate_bias, mnorm, (c0, n0, m0), l0, (ca, na, ma), layer, p.n_seq, p.t_pad, L, valid, p.y_dtype, S),
        _swa(proj, cos, sin, sinks, (kc, vc), l0, (ka, va), layer, p.n_seq, p.t_pad, L, valid,
             p.has_cache, not p.has_cache, p.y_dtype, S),
        _ssd(proj, conv_w, conv_b, dt_bias, a_log, d_skip, snorm, (conv0, s0), l0, (conva, sa), layer,
             p.n_seq, p.t_pad, L, valid, p.y_dtype, S),
        _pool(proj, pool_lin, pool_scale, (p0,), l0, (pa,), layer, p.n_seq, p.t_pad, L, valid, p.n_hist, p.y_dtype, S)]
    outs = _fused_call([parts[i] for i in _MIXER_ORDER], (p.n_seq // S, p.t_pad // L), "mixers", 48, S)
    (ya, ca, na, ma), (yb, ka, va), (yc, conva, sa), (yd, pa) = [outs[_MIXER_ORDER.index(i)] for i in range(len(parts))]
    x = _outproj((ya, yb, yc, yd), w_out, layer, x, g_post, p.tm_out, p.tn_out)
    return x, (ca, na, ma, ka, va, sa, conva, pa)


def kernel(x_prompt, x_sample, state_mlstm_C, state_mlstm_n, state_mlstm_m, cache_swa_k, cache_swa_v, state_ssd,
           state_ssd_conv, state_pool, norm_pre, norm_post, w_in, w_out, mlstm_b_i, mlstm_b_f, mlstm_norm,
           swa_sinks, ssd_conv_w, ssd_conv_b, ssd_dt_bias, ssd_A_log, ssd_D, ssd_norm, pool_lin, pool_scale):
    bp, seq, _ = x_prompt.shape
    bs, dec_seq, _ = x_sample.shape
    t_s = SUBLANE * pl.cdiv(dec_seq, SUBLANE)

    assert WINDOW == SSD_CHUNK == 2 * MLSTM_CHUNK
    prompt = _Path(bp, seq, seq, WINDOW, 1, False, 0, 1024, 1280, 1024, 512, BF16)
    sample = _Path(bs, t_s, dec_seq, t_s, 4, True, POOL_HIST, bs * t_s, 1280, bs * t_s, 512, F32)

    w_in_r = _regroup_w_in(w_in)
    w_out_b = w_out.astype(BF16)

    xp = x_prompt.reshape(bp * seq, D_MODEL)
    xs = jnp.pad(x_sample, ((0, 0), (0, t_s - dec_seq), (0, 0))).reshape(bs * t_s, D_MODEL)

    tab_p = _rope_tables(jnp.arange(seq))
    tab_s = _rope_tables(PAST_LEN + jnp.arange(t_s))

    def state_shapes(layers, b):
        return ((layers, b, MLSTM_HEADS, MLSTM_DH, MLSTM_DH), (layers, b, MLSTM_HEADS, LANE), (layers, b, MLSTM_HEADS, LANE),
                (layers, b, WINDOW, LANE), (layers, b, WINDOW, LANE),
                (layers, b, SSD_HEADS, SSD_DH, SSD_DSTATE), (layers, b, SSD_CONV - 1, SSD_CONV_DIM),
                (layers, b, POOL_HIST, MIX))

    p_states0 = tuple(jnp.zeros(s, F32) for s in state_shapes(1, bp))
    s_states0 = (state_mlstm_C, state_mlstm_n,
                 jnp.broadcast_to(state_mlstm_m[..., None], (DEPTH, bs, MLSTM_HEADS, LANE)),
                 cache_swa_k.reshape(DEPTH, bs, WINDOW, LANE), cache_swa_v.reshape(DEPTH, bs, WINDOW, LANE),
                 state_ssd, state_ssd_conv, state_pool)
    p_acc = tuple(jnp.zeros(s, F32) for s in state_shapes(DEPTH, bp))
    s_acc = tuple(jnp.zeros(s, F32) for s in state_shapes(DEPTH, bs))

    for l in range(DEPTH):
        gate_bias = _pad_lanes(jnp.concatenate([mlstm_b_i[l], mlstm_b_f[l]]))
        prm = (norm_pre[l].reshape(1, D_MODEL), norm_post[l].reshape(1, D_MODEL), w_in_r, w_out_b,
               gate_bias, mlstm_norm[l].reshape(1, MIX), _pad_lanes(swa_sinks[l]),
               ssd_conv_w[l], ssd_conv_b[l].reshape(1, SSD_CONV_DIM), _pad_lanes(ssd_dt_bias[l]),
               _pad_lanes(ssd_A_log[l]), jnp.repeat(ssd_D[l], SSD_DH).reshape(1, MIX), ssd_norm[l].reshape(1, MIX),
               pool_lin[l], pool_scale[l].reshape(1, MIX))
        xp, p_acc = _layer(xp, l, prm, p_states0, 0, p_acc, tab_p, prompt)
        xs, s_acc = _layer(xs, l, prm, s_states0, l, s_acc, tab_s, sample)

    def finish(acc, b):
        c, n, m, k, v, s, conv, pool = acc
        kv_shape = (DEPTH, b, WINDOW, SWA_KV_HEADS, SWA_DH)
        return c, n, m[..., 0], k.reshape(kv_shape), v.reshape(kv_shape), s, conv, pool

    y_prompt = xp.reshape(bp, seq, D_MODEL)
    y_sample = xs.reshape(bs, t_s, D_MODEL)[:, :dec_seq]
    return (y_prompt, y_sample) + finish(p_acc, bp) + finish(s_acc, bs)
```

```python
import functools
import math

import jax
import jax.numpy as jnp
from jax import lax
from jax.experimental import pallas as pl
from jax.experimental.pallas import tpu as pltpu

F32 = jnp.float32
BF16 = jnp.bfloat16

D_MODEL = 4096
DEPTH = 4
PAST_LEN = 8192
MIX = D_MODEL // 4
MLSTM_DH = 128
MLSTM_HEADS = MIX // MLSTM_DH
MLSTM_CHUNK = 64
GATE_CAP = 15.0
SWA_DH = 64
SWA_HEADS = MIX // SWA_DH
SWA_KV_HEADS = 2
SWA_GROUP = SWA_HEADS // SWA_KV_HEADS
WINDOW = 128
ROPE_THETA = 10000.0
SSD_DH = 64
SSD_HEADS = MIX // SSD_DH
SSD_GROUPS = 4
SSD_HPG = SSD_HEADS // SSD_GROUPS
SSD_DSTATE = 128
SSD_CONV = 4
SSD_CONV_DIM = MIX + 2 * SSD_GROUPS * SSD_DSTATE
SSD_CHUNK = 128
POOL_WINDOWS = (2, 4, 8, 16)
POOL_GC = MIX // len(POOL_WINDOWS)
POOL_HIST = max(POOL_WINDOWS) - 1
RMS_EPS = 1e-6

LANE = 128
SUBLANE = 8
NEG = -1e30

_COL = {}
_off = 0
for _name, _width in (("qa", MIX), ("ka", MIX), ("va", MIX), ("oa", MIX), ("za", MIX),
                      ("qb", MIX), ("zb", MIX), ("zc", MIX), ("xbc", SSD_CONV_DIM),
                      ("ud", MIX), ("zd", MIX),
                      ("gate", LANE), ("kb", LANE), ("vb", LANE), ("dt", LANE)):
    _COL[_name] = (_off, _width)
    _off += _width
NPROJ = _off


def _colblock(name):
    off, width = _COL[name]
    assert off % width == 0
    return off // width


LOG2E = math.log2(math.e)


def _sigmoid(x):
    return 0.5 * jnp.tanh(0.5 * x) + 0.5


def _silu(x):
    half = 0.5 * x
    return half + half * jnp.tanh(half)


def _dot(a, b):
    return jnp.dot(a, b, preferred_element_type=F32)


def _dot_nt(a, b):
    return lax.dot_general(a, b, (((1,), (1,)), ((), ())), preferred_element_type=F32)


def _dot_tn(a, b):
    return lax.dot_general(a, b, (((0,), (0,)), ((), ())), preferred_element_type=F32)


def _cumsum_rows(x):
    n = x.shape[0]
    r = lax.broadcasted_iota(jnp.int32, (n, n), 0)
    c = lax.broadcasted_iota(jnp.int32, (n, n), 1)
    tri = (c <= r).astype(F32)
    return jnp.dot(tri, x, preferred_element_type=F32, precision=lax.Precision.HIGHEST)


def _params(sem, vmem_mb):
    return pltpu.CompilerParams(dimension_semantics=sem, vmem_limit_bytes=vmem_mb * 1024 * 1024)


_ANY = pl.BlockSpec(memory_space=pl.ANY)
_DONE = object()


class _Part:
    def __init__(self, body, operands, in_specs, out_specs, out_shape, n_alias, kinds, rows, scratch=()):
        self.body, self.operands, self.in_specs, self.out_specs = body, operands, in_specs, out_specs
        self.out_shape, self.n_alias, self.scratch = out_shape, n_alias, list(scratch)
        self.kinds, self.rows = kinds, rows
        assert len(kinds) == len(operands) + len(out_shape)


def _fused_call(parts, grid, name, vmem_mb, seqs_per_step=1):
    n_in = [len(p.operands) for p in parts]
    n_out = [len(p.out_shape) for p in parts]
    n_scr = [len(p.scratch) * seqs_per_step for p in parts]

    def view(ref, kind, rows, q):
        if kind == "rows":
            return ref.at[pl.ds(q * rows, rows)]
        return ref.at[pl.ds(q, 1)] if kind == "state" else ref

    def body(*refs):
        ins, outs, scr = refs[:sum(n_in)], refs[sum(n_in):sum(n_in) + sum(n_out)], refs[sum(n_in) + sum(n_out):]

        def run(first_chunk_init):
            pending, i, o, s = [], 0, 0, 0
            for k, p in enumerate(parts):
                io = list(ins[i:i + n_in[k]]) + list(outs[o:o + n_out[k]])
                per_seq = n_scr[k] // seqs_per_step
                for q in range(seqs_per_step):
                    pending.append(p.body(*[view(r, kind, p.rows, q) for r, kind in zip(io, p.kinds)],
                                          *scr[s + q * per_seq:s + (q + 1) * per_seq],
                                          first_chunk_init=first_chunk_init))
                i, o, s = i + n_in[k], o + n_out[k], s + n_scr[k]
            while pending:
                pending = [g for g in pending if next(g, _DONE) is not _DONE]

        pl.when(pl.program_id(len(grid) - 1) == 0)(functools.partial(run, True))
        run(False)

    aliases, i, o = {}, 0, 0
    for k, p in enumerate(parts):
        for a in range(p.n_alias):
            aliases[i + n_in[k] - p.n_alias + a] = o + 1 + a
        i, o = i + n_in[k], o + n_out[k]
    flat = pl.pallas_call(
        body,
        grid=grid,
        in_specs=[s for p in parts for s in p.in_specs],
        out_specs=[s for p in parts for s in p.out_specs],
        out_shape=[s for p in parts for s in p.out_shape],
        scratch_shapes=[s for p in parts for s in p.scratch * seqs_per_step],
        input_output_aliases=aliases,
        compiler_params=_params(("arbitrary",) * len(grid), vmem_mb),
        name=name,
    )(*[x for p in parts for x in p.operands])
    out, o = [], 0
    for k in range(len(parts)):
        out.append(list(flat[o:o + n_out[k]]))
        o += n_out[k]
    return out


def _state_spec(stacked, layer, seqs=1):
    tail = stacked.shape[2:]
    return pl.BlockSpec((None, seqs) + tail, lambda s, c: (layer, s) + (0,) * len(tail))


_NORM_STEPS = 8


def _inproj_kernel(x_ref, g_ref, w_ref, o_ref, h_ref):
    p = pl.program_id(0)
    j = pl.program_id(1)
    rows = x_ref.shape[0]

    def normalise_slice():
        rs = pl.ds(pl.multiple_of(jnp.minimum(j, _NORM_STEPS - 1) * rows, rows), rows)
        x = x_ref[...]
        ms = jnp.mean(x * x, axis=-1, keepdims=True)
        h_ref[p % 2, rs, :] = (x * lax.rsqrt(ms + RMS_EPS) * g_ref[...]).astype(BF16)

    pl.when(p == 0)(normalise_slice)

    @pl.when(p > 0)
    def _():
        normalise_slice()
        o_ref[...] = _dot(h_ref[(p - 1) % 2], w_ref[...])


def _inproj(x, g, w_all, layer, tm, tn):
    m = x.shape[0]
    n_row_tiles, n_col = m // tm, NPROJ // tn
    assert n_col >= _NORM_STEPS and tm % (_NORM_STEPS * 2 * SUBLANE) == 0
    col = lambda p, j: jnp.where(p > 0, j, 0)
    x_slice = lambda p, j: jnp.minimum(p, n_row_tiles - 1) * _NORM_STEPS + jnp.minimum(j, _NORM_STEPS - 1)
    return pl.pallas_call(
        _inproj_kernel,
        grid=(n_row_tiles + 1, n_col),
        in_specs=[pl.BlockSpec((tm // _NORM_STEPS, D_MODEL), lambda p, j: (x_slice(p, j), 0)),
                  pl.BlockSpec((1, D_MODEL), lambda p, j: (0, 0)),
                  pl.BlockSpec((None, None, D_MODEL, tn), lambda p, j: (layer, col(p, j), 0, 0))],
        out_specs=pl.BlockSpec((tm, tn), lambda p, j: (jnp.maximum(p - 1, 0), col(p, j))),
        out_shape=jax.ShapeDtypeStruct((m, NPROJ), F32),
        scratch_shapes=[pltpu.VMEM((2, tm, D_MODEL), BF16)],
        compiler_params=_params(("arbitrary", "arbitrary"), 58),
        name="inproj",
    )(x, g, w_all)


def _outproj_kernel(ya_ref, yb_ref, yc_ref, yd_ref, w_ref, x_ref, g_ref, o_ref, acc_ref, ss_ref, *, tn, n_row_tiles):
    i = pl.program_id(0)
    j = pl.program_id(1)
    slot = i % 2
    col = pl.ds(pl.multiple_of(j * tn, tn), tn)

    def finalize():
        rs = lax.rsqrt(ss_ref[1 - slot] * (1.0 / D_MODEL) + RMS_EPS)
        o_ref[...] = x_ref[...] + acc_ref[:, col] * rs * g_ref[...]

    def multiply():
        out = None
        for part, r in enumerate((ya_ref, yb_ref, yc_ref, yd_ref)):
            term = _dot(r[...].astype(BF16), w_ref[part * MIX:(part + 1) * MIX, :])
            out = term if out is None else out + term
        acc_ref[:, col] = out
        sq = jnp.sum(out * out, axis=1, keepdims=True)
        ss_ref[slot] = jnp.where(j == 0, sq, ss_ref[slot] + sq)

    pl.when(i == 0)(multiply)
    pl.when(i == n_row_tiles)(finalize)

    @pl.when(jnp.logical_and(i > 0, i < n_row_tiles))
    def _():
        finalize()
        multiply()


def _outproj(ys, w_all, layer, x, g, tm, tn):
    m = x.shape[0]
    n_row_tiles = m // tm
    yspec = pl.BlockSpec((tm, MIX), lambda i, j: (jnp.minimum(i, n_row_tiles - 1), 0))
    lagged = pl.BlockSpec((tm, tn), lambda i, j: (jnp.maximum(i - 1, 0), jnp.where(i > 0, j, 0)))
    return pl.pallas_call(
        functools.partial(_outproj_kernel, tn=tn, n_row_tiles=n_row_tiles),
        grid=(n_row_tiles + 1, D_MODEL // tn),
        in_specs=[yspec, yspec, yspec, yspec,
                  pl.BlockSpec((None, D_MODEL, tn), lambda i, j: (layer, 0, j)),
                  lagged,
                  pl.BlockSpec((1, tn), lambda i, j: (0, j))],
        out_specs=lagged,
        out_shape=jax.ShapeDtypeStruct((m, D_MODEL), F32),
        scratch_shapes=[pltpu.VMEM((tm, D_MODEL), F32), pltpu.VMEM((2, tm, 1), F32)],
        compiler_params=_params(("arbitrary", "arbitrary"), 56),
        name="outproj",
    )(*ys, w_all, x, g)


def _mlstm_kernel(q_ref, k_ref, v_ref, o_ref, z_ref, gate_ref, bias_ref, mn_ref, c0_ref, n0_ref, m0_ref,
                  c_alias, n_alias, m_alias, y_ref, c_ref, n_ref, m_ref, *, L, valid, first_chunk_init):
    del c_alias, n_alias, m_alias
    if first_chunk_init:
        c_ref[...] = c0_ref[...]
        n_ref[...] = n0_ref[...]
        m_ref[...] = m0_ref[...]
        return

    row = lax.broadcasted_iota(jnp.int32, (L, LANE), 0)
    row_ok = row < valid
    gates = GATE_CAP * jnp.tanh((gate_ref[...] + bias_ref[...]) / GATE_CAP)
    logf = jnp.where(row_ok, jax.nn.log_sigmoid(gates), 0.0)
    b_all = _cumsum_rows(pltpu.roll(logf, LANE - MLSTM_HEADS, axis=1))
    r_all = jnp.where(row_ok, gates - b_all, NEG)
    r_all_t = r_all.T
    yield

    tr = lax.broadcasted_iota(jnp.int32, (L, L), 0)
    tc = lax.broadcasted_iota(jnp.int32, (L, L), 1)
    tril = tc <= tr

    heads = range(MLSTM_HEADS)
    sl = [slice(h * MLSTM_DH, (h + 1) * MLSTM_DH) for h in heads]
    q = [q_ref[:, sl[h]] * (MLSTM_DH ** -0.5) for h in heads]
    k = [k_ref[:, sl[h]] for h in heads]
    v = [v_ref[:, sl[h]] for h in heads]
    c_prev = [c_ref[0, h] for h in heads]
    n_prev = [n_ref[0, h:h + 1, :] for h in heads]
    m_prev = [m_ref[0, h:h + 1, 0:1] for h in heads]
    qk = [_dot_nt(q[h], k[h]) for h in heads]
    qc = [_dot(q[h], c_prev[h]) for h in heads]
    qn = [jnp.sum(q[h] * n_prev[h], axis=1, keepdims=True) for h in heads]
    yield

    b_col = [b_all[:, h:h + 1] for h in heads]
    dmat = [jnp.where(tril, b_col[h] + r_all_t[h:h + 1, :], NEG) for h in heads]
    inter = [b_col[h] + m_prev[h] for h in heads]
    mt = [jnp.maximum(inter[h], jnp.max(dmat[h], axis=1, keepdims=True)) for h in heads]
    yield
    s = [jnp.exp(dmat[h] - mt[h]) * qk[h] for h in heads]
    sv = [_dot(s[h], v[h]) for h in heads]
    yield
    w_prev_t = [jnp.exp(inter[h] - mt[h]) for h in heads]
    den = [w_prev_t[h] * qn[h] + jnp.sum(s[h], axis=1, keepdims=True) for h in heads]
    hh = [(w_prev_t[h] * qc[h] + sv[h]) / jnp.maximum(jnp.abs(den[h]), jnp.exp(-mt[h])) for h in heads]
    yield

    b_last = [b_all[L - 1:L, h:h + 1] for h in heads]
    dec = [b_last[h] + r_all[:, h:h + 1] for h in heads]
    m_new = [jnp.maximum(b_last[h] + m_prev[h], jnp.max(dec[h], axis=0, keepdims=True)) for h in heads]
    w_prev = [jnp.exp(b_last[h] + m_prev[h] - m_new[h]) for h in heads]
    kw = [k[h] * jnp.exp(dec[h] - m_new[h]) for h in heads]
    c_new = [w_prev[h] * c_prev[h] + _dot_tn(kw[h], v[h]) for h in heads]
    n_new = [w_prev[h] * n_prev[h] + jnp.sum(kw[h], axis=0, keepdims=True) for h in heads]
    yield

    hn = [hh[h] * lax.rsqrt(jnp.mean(hh[h] * hh[h], axis=1, keepdims=True) + RMS_EPS) * mn_ref[:, sl[h]] for h in heads]
    ys = [hn[h] * _sigmoid(o_ref[:, sl[h]]) * _silu(z_ref[:, sl[h]]) for h in heads]
    yield

    y_ref[...] = jnp.concatenate(ys, axis=1).astype(y_ref.dtype)
    for h in heads:
        c_ref[0, h] = c_new[h]
    n_ref[0] = jnp.concatenate(n_new, axis=0)
    m_ref[0] = jnp.concatenate([jnp.broadcast_to(m_new[h], (1, LANE)) for h in heads], axis=0)


def _mlstm(proj, bias, mnorm, st0, l0, acc, layer, n_seq, t_pad, L, valid, y_dtype, S):
    nc = t_pad // L
    wide = lambda name: pl.BlockSpec((S * L, MIX), lambda s, c, cb=_colblock(name): (s * nc + c, cb))
    vec = lambda n: pl.BlockSpec((1, n), lambda s, c: (0, 0))
    return _Part(
        functools.partial(_mlstm_kernel, L=L, valid=valid),
        operands=[proj, proj, proj, proj, proj, proj, bias, mnorm, *st0, *acc],
        in_specs=[wide("qa"), wide("ka"), wide("va"), wide("oa"), wide("za"),
                  pl.BlockSpec((S * L, LANE), lambda s, c, cb=_colblock("gate"): (s * nc + c, cb)),
                  vec(LANE), vec(MIX)] + [_state_spec(a, l0, S) for a in st0] + [_ANY] * len(acc),
        out_specs=[pl.BlockSpec((S * L, MIX), lambda s, c: (s * nc + c, 0))] + [_state_spec(a, layer, S) for a in acc],
        out_shape=[jax.ShapeDtypeStruct((n_seq * t_pad, MIX), y_dtype)]
        + [jax.ShapeDtypeStruct(a.shape, F32) for a in acc],
        n_alias=len(acc),
        kinds=["rows"] * 6 + ["whole"] * 2 + ["state"] * 3 + ["whole"] * 3 + ["rows"] + ["state"] * 3, rows=L)


def _swa_kernel(q_ref, k_ref, v_ref, z_ref, cos_ref, sin_ref, sink_ref, kc_ref, vc_ref, k_alias, v_alias,
                y_ref, ko_ref, vo_ref, *, Lq, valid, has_cache, first_chunk_init):
    del k_alias, v_alias
    if first_chunk_init:
        ko_ref[...] = kc_ref[...]
        vo_ref[...] = vc_ref[...]
        return
    blk = pl.program_id(1)

    cos = cos_ref[...]
    sin = sin_ref[...]
    lane = lax.broadcasted_iota(jnp.int32, (Lq, LANE), 1)
    first_half = (lane % SWA_DH) < (SWA_DH // 2)

    def rope(x):
        partner = jnp.where(first_half, pltpu.roll(x, LANE - SWA_DH // 2, axis=1), pltpu.roll(x, SWA_DH // 2, axis=1))
        return x * cos + partner * sin

    k_cur = rope(k_ref[...])
    v_cur = v_ref[...]
    k_prev = ko_ref[0]
    v_prev = vo_ref[0]
    kk = jnp.concatenate([k_prev, k_cur], axis=0)
    vv = jnp.concatenate([v_prev, v_cur], axis=0)

    rows, nk = SWA_GROUP * Lq, WINDOW + Lq
    t = lax.broadcasted_iota(jnp.int32, (rows, nk), 0) & (Lq - 1)
    c = lax.broadcasted_iota(jnp.int32, (rows, nk), 1)
    prev_ok = jnp.logical_or(has_cache, blk > 0)
    mask = ((c < WINDOW) & (c > t) & prev_ok) | ((c >= WINDOW) & (c - WINDOW <= t) & (c - WINDOW < valid))

    yield
    pairs_per_group = SWA_GROUP // 2
    groups = range(SWA_KV_HEADS)
    gsl = [slice(g * SWA_DH, (g + 1) * SWA_DH) for g in groups]
    q_pairs = [[rope(q_ref[:, (g * pairs_per_group + p) * LANE:(g * pairs_per_group + p + 1) * LANE])
                * (SWA_DH ** -0.5 * LOG2E) for p in range(pairs_per_group)] for g in groups]
    q_st = [jnp.concatenate([q_pairs[g][h // 2][:, (h % 2) * SWA_DH:(h % 2 + 1) * SWA_DH]
                             for h in range(SWA_GROUP)], axis=0) for g in groups]
    sink = [jnp.concatenate([jnp.broadcast_to(sink_ref[0:1, g * SWA_GROUP + h:g * SWA_GROUP + h + 1], (Lq, 1))
                             for h in range(SWA_GROUP)], axis=0) * LOG2E for g in groups]
    yield
    s = [jnp.where(mask, _dot_nt(q_st[g], kk[:, gsl[g]]), NEG) for g in groups]
    mx = [jnp.maximum(jnp.max(s[g], axis=1, keepdims=True), sink[g]) for g in groups]
    yield
    p = [jnp.exp2(s[g] - mx[g]) for g in groups]
    den = [jnp.sum(p[g], axis=1, keepdims=True) + jnp.exp2(sink[g] - mx[g]) for g in groups]
    yield
    o = [_dot(p[g], vv[:, gsl[g]]) / den[g] for g in groups]
    yield
    y_chunks = []
    for g in groups:
        for pr in range(pairs_per_group):
            csl = slice((g * pairs_per_group + pr) * LANE, (g * pairs_per_group + pr + 1) * LANE)
            pair = jnp.concatenate([o[g][(2 * pr) * Lq:(2 * pr + 1) * Lq], o[g][(2 * pr + 1) * Lq:(2 * pr + 2) * Lq]],
                                   axis=1)
            y_chunks.append(pair * _silu(z_ref[:, csl]))
    y_ref[...] = jnp.concatenate(y_chunks, axis=1).astype(y_ref.dtype)

    if valid == WINDOW:
        ko_ref[0] = k_cur
        vo_ref[0] = v_cur
    else:
        ko_ref[0, 0:WINDOW - valid, :] = k_prev[valid:, :]
        ko_ref[0, WINDOW - valid:WINDOW, :] = k_cur[0:valid, :]
        vo_ref[0, 0:WINDOW - valid, :] = v_prev[valid:, :]
        vo_ref[0, WINDOW - valid:WINDOW, :] = v_cur[0:valid, :]


def _swa(proj, cos, sin, sinks, st0, l0, acc, layer, n_seq, t_pad, Lq, valid, has_cache, table_per_block, y_dtype, S):
    nb = t_pad // Lq
    rowblk = lambda s, b: s * nb + b
    wide = lambda name: pl.BlockSpec((S * Lq, MIX), lambda s, b, cb=_colblock(name): (rowblk(s, b), cb))
    narrow = lambda name: pl.BlockSpec((S * Lq, LANE), lambda s, b, cb=_colblock(name): (rowblk(s, b), cb))
    table = pl.BlockSpec((Lq, LANE), (lambda s, b: (b, 0)) if table_per_block else (lambda s, b: (0, 0)))
    return _Part(
        functools.partial(_swa_kernel, Lq=Lq, valid=valid, has_cache=has_cache),
        operands=[proj, proj, proj, proj, cos, sin, sinks, *st0, *acc],
        in_specs=[wide("qb"), narrow("kb"), narrow("vb"), wide("zb"), table, table,
                  pl.BlockSpec((1, LANE), lambda s, b: (0, 0))] + [_state_spec(a, l0, S) for a in st0] + [_ANY] * len(acc),
        out_specs=[pl.BlockSpec((S * Lq, MIX), lambda s, b: (rowblk(s, b), 0))] + [_state_spec(a, layer, S) for a in acc],
        out_shape=[jax.ShapeDtypeStruct((n_seq * t_pad, MIX), y_dtype)]
        + [jax.ShapeDtypeStruct(a.shape, F32) for a in acc],
        n_alias=len(acc),
        kinds=["rows"] * 4 + ["whole"] * 3 + ["state"] * 2 + ["whole"] * 2 + ["rows"] + ["state"] * 2, rows=Lq)


_CONV_PAD = SUBLANE


def _spread_heads(x, e):
    lane = lax.broadcasted_iota(jnp.int32, x.shape, 1)
    x = jnp.where(lane < SSD_HEADS, x, 0.0)
    hi = x.astype(BF16)
    rest = x - hi.astype(F32)
    mid = rest.astype(BF16)
    lo = (rest - mid.astype(F32)).astype(BF16)
    return _dot(hi, e) + _dot(mid, e) + _dot(lo, e)


def _ssd_kernel(xbc_ref, z_ref, dt_ref, cw_ref, cb_ref, dtb_ref, alog_ref, dskip_ref, sn_ref, e_ref, conv0_ref, s0_ref,
                conv_alias, s_alias, y_ref, convo_ref, so_ref, full_ref, *, L, valid, first_chunk_init):
    del conv_alias, s_alias
    hist = SSD_CONV - 1
    if first_chunk_init:
        so_ref[...] = s0_ref[...]
        full_ref[_CONV_PAD - hist:_CONV_PAD, :] = conv0_ref[0]
        return

    full_ref[_CONV_PAD:_CONV_PAD + L, :] = xbc_ref[...]
    full = full_ref[...]
    acc = cb_ref[...] + full[_CONV_PAD:, :] * cw_ref[hist:hist + 1, :]
    for j in range(hist):
        acc = acc + pltpu.roll(full, hist - j, axis=0)[_CONV_PAD:, :] * cw_ref[j:j + 1, :]
    xc = _silu(acc)
    new_hist = full_ref[_CONV_PAD + valid - hist:_CONV_PAD + valid, :]
    convo_ref[0] = new_hist
    full_ref[_CONV_PAD - hist:_CONV_PAD, :] = new_hist
    yield

    nbc = SSD_GROUPS * SSD_DSTATE
    row_ok = lax.broadcasted_iota(jnp.int32, (L, LANE), 0) < valid
    dtv = jnp.where(row_ok, jax.nn.softplus(dt_ref[...] + dtb_ref[...]), 0.0)
    a = dtv * (-jnp.exp(alog_ref[...]) * LOG2E)
    cum = _cumsum_rows(a)
    cum_t = cum.T
    tr = lax.broadcasted_iota(jnp.int32, (L, L), 0)
    tc = lax.broadcasted_iota(jnp.int32, (L, L), 1)
    tril = tc <= tr
    yield

    groups, heads = range(SSD_GROUPS), range(SSD_HEADS)
    gp = SSD_HPG * SSD_DH
    bg = [xc[:, MIX + g * SSD_DSTATE:MIX + (g + 1) * SSD_DSTATE] for g in groups]
    cg = [xc[:, MIX + nbc + g * SSD_DSTATE:MIX + nbc + (g + 1) * SSD_DSTATE] for g in groups]
    st = [so_ref[0, g * SSD_HPG:(g + 1) * SSD_HPG].reshape(gp, SSD_DSTATE) for g in groups]
    cb = [_dot_nt(cg[g], bg[g]) for g in groups]
    cst = [_dot_nt(cg[g], st[g]) for g in groups]
    yield
    c_col = [cum[:, h:h + 1] for h in heads]
    c_last = [cum[L - 1:L, h:h + 1] for h in heads]
    lm = [jnp.where(tril, jnp.exp2(jnp.where(tril, c_col[h] - cum_t[h:h + 1, :], 0.0)), 0.0) for h in heads]

    spread = _spread_heads(jnp.concatenate([dtv, jnp.exp2(cum), jnp.exp2(cum[L - 1:L, :] - cum)], axis=0), e_ref[...])
    xs = xc[:, :MIX]
    xdt = xs * spread[0:L]
    xw = xdt * spread[2 * L:3 * L]
    yield
    low_half = lax.broadcasted_iota(jnp.int32, (L, LANE), 1) < SSD_DH
    y_in = []
    for pair in range(SSD_HEADS // 2):
        g = (2 * pair) // SSD_HPG
        x_pair = xdt[:, pair * LANE:(pair + 1) * LANE]
        y_in.append(jnp.where(low_half, _dot(cb[g] * lm[2 * pair], x_pair), _dot(cb[g] * lm[2 * pair + 1], x_pair)))
    y_state = jnp.concatenate(cst, axis=1) * spread[L:2 * L]
    yield
    upd = [_dot_tn(xw[:, g * gp:(g + 1) * gp], bg[g]) for g in groups]
    for h in heads:
        g, hh = divmod(h, SSD_HPG)
        so_ref[0, h] = jnp.exp2(c_last[h]) * st[g][hh * SSD_DH:(hh + 1) * SSD_DH] + upd[g][hh * SSD_DH:(hh + 1) * SSD_DH]
    yc = (jnp.concatenate(y_in, axis=1) + y_state + dskip_ref[...] * xs) * _silu(z_ref[...])
    y_ref[...] = (yc * lax.rsqrt(jnp.mean(yc * yc, axis=1, keepdims=True) + RMS_EPS) * sn_ref[...]).astype(y_ref.dtype)


def _ssd(proj, conv_w, conv_b, dt_bias, a_log, d_skip, snorm, st0, l0, acc, layer, n_seq, t_pad, L, valid, y_dtype, S):
    nc = t_pad // L
    rowblk = lambda s, c: s * nc + c
    vec = lambda n: pl.BlockSpec((1, n), lambda s, c: (0, 0))
    spread = (jnp.arange(MIX)[None, :] // SSD_DH == jnp.arange(LANE)[:, None]).astype(BF16)
    return _Part(
        functools.partial(_ssd_kernel, L=L, valid=valid),
        operands=[proj, proj, proj, conv_w, conv_b, dt_bias, a_log, d_skip, snorm, spread, *st0, *acc],
        in_specs=[pl.BlockSpec((S * L, SSD_CONV_DIM), lambda s, c, cb=_colblock("xbc"): (rowblk(s, c), cb)),
                  pl.BlockSpec((S * L, MIX), lambda s, c, cb=_colblock("zc"): (rowblk(s, c), cb)),
                  pl.BlockSpec((S * L, LANE), lambda s, c, cb=_colblock("dt"): (rowblk(s, c), cb)),
                  pl.BlockSpec((SSD_CONV, SSD_CONV_DIM), lambda s, c: (0, 0)),
                  vec(SSD_CONV_DIM), vec(LANE), vec(LANE), vec(MIX), vec(MIX),
                  pl.BlockSpec((LANE, MIX), lambda s, c: (0, 0))]
        + [_state_spec(a, l0, S) for a in st0] + [_ANY] * len(acc),
        out_specs=[pl.BlockSpec((S * L, MIX), lambda s, c: (rowblk(s, c), 0))] + [_state_spec(a, layer, S) for a in acc],
        out_shape=[jax.ShapeDtypeStruct((n_seq * t_pad, MIX), y_dtype)]
        + [jax.ShapeDtypeStruct(a.shape, F32) for a in acc],
        n_alias=len(acc),
        kinds=["rows"] * 3 + ["whole"] * 7 + ["state"] * 2 + ["whole"] * 2 + ["rows"] + ["state"] * 2, rows=L,
        scratch=[pltpu.VMEM((_CONV_PAD + L, SSD_CONV_DIM), F32)])


_POOL_PAD = 2 * SUBLANE
assert all(w & (w - 1) == 0 for w in POOL_WINDOWS) and POOL_HIST < _POOL_PAD


def _pool_kernel(u_ref, z_ref, lin_ref, scale_ref, p0_ref, p_alias, y_ref, po_ref, full_ref, *, L, valid, n_hist,
                 first_chunk_init):
    del p_alias
    if first_chunk_init:
        full_ref[_POOL_PAD - POOL_HIST:_POOL_PAD, :] = p0_ref[0]
        return
    step = pl.program_id(1)

    full_ref[_POOL_PAD:_POOL_PAD + L, :] = u_ref[...]
    pos = n_hist + step * L + lax.broadcasted_iota(jnp.int32, (L, 1), 0)
    ys = []
    for g, w in enumerate(POOL_WINDOWS):
        sl = slice(g * POOL_GC, (g + 1) * POOL_GC)
        rows = full_ref[:, sl]
        tot, span = rows, 1
        while span < w:
            tot = tot + pltpu.roll(tot, span, axis=0)
            span *= 2
        cur = rows[_POOL_PAD:]
        cnt = jnp.minimum(pos + 1, w).astype(F32)
        d = tot[_POOL_PAD:] / cnt - cur
        ys.append(_dot(d, lin_ref[g]) * scale_ref[:, sl] * _silu(z_ref[:, sl]))
        yield
    y_ref[...] = jnp.concatenate(ys, axis=1).astype(y_ref.dtype)
    new_hist = full_ref[_POOL_PAD + valid - POOL_HIST:_POOL_PAD + valid, :]
    po_ref[0] = new_hist
    full_ref[_POOL_PAD - POOL_HIST:_POOL_PAD, :] = new_hist


def _pool(proj, lin, scale, st0, l0, acc, layer, n_seq, t_pad, L, valid, n_hist, y_dtype, S):
    nt = t_pad // L
    rowblk = lambda s, c: s * nt + c
    return _Part(
        functools.partial(_pool_kernel, L=L, valid=valid, n_hist=n_hist),
        operands=[proj, proj, lin, scale, *st0, *acc],
        in_specs=[pl.BlockSpec((S * L, MIX), lambda s, c, cb=_colblock("ud"): (rowblk(s, c), cb)),
                  pl.BlockSpec((S * L, MIX), lambda s, c, cb=_colblock("zd"): (rowblk(s, c), cb)),
                  pl.BlockSpec((len(POOL_WINDOWS), POOL_GC, POOL_GC), lambda s, c: (0, 0, 0)),
                  pl.BlockSpec((1, MIX), lambda s, c: (0, 0))] + [_state_spec(a, l0, S) for a in st0] + [_ANY] * len(acc),
        out_specs=[pl.BlockSpec((S * L, MIX), lambda s, c: (rowblk(s, c), 0))] + [_state_spec(a, layer, S) for a in acc],
        out_shape=[jax.ShapeDtypeStruct((n_seq * t_pad, MIX), y_dtype)]
        + [jax.ShapeDtypeStruct(a.shape, F32) for a in acc],
        n_alias=len(acc),
        kinds=["rows"] * 2 + ["whole"] * 2 + ["state", "whole", "rows", "state"], rows=L,
        scratch=[pltpu.VMEM((_POOL_PAD + L, MIX), F32)])


_REF_SIZES = (MIX, MIX, MIX, MIX, MIX, MLSTM_HEADS, MLSTM_HEADS,
              SWA_HEADS * SWA_DH, SWA_KV_HEADS * SWA_DH, SWA_KV_HEADS * SWA_DH, MIX,
              MIX, SSD_CONV_DIM, SSD_HEADS, MIX, MIX)
_REF_NAMES = ("qa", "ka", "va", "oa", "za", "ia", "fa", "qb", "kb", "vb", "zb", "zc", "xbc", "dt", "ud", "zd")


def _regroup_plan():
    ref_off, o = {}, 0
    for name, size in zip(_REF_NAMES, _REF_SIZES):
        ref_off[name] = o
        o += size
    src = []
    for name, (off, width) in _COL.items():
        start = ref_off["ia"] if name == "gate" else ref_off[name]
        src += [start + b * LANE for b in range(width // LANE)]
    assert len(src) == NPROJ // LANE and all(s % SUBLANE == 0 and s + LANE <= o for s in src)
    return src


_REGROUP_BLOCKS = 5


def _regroup_kernel(src_tab, *refs):
    del src_tab
    o_ref = refs[-1]
    for i, wt_ref in enumerate(refs[:-1]):
        o_ref[:, i * LANE:(i + 1) * LANE] = wt_ref[...].T.astype(BF16)


def _regroup_w_in(w_in, tn):
    depth, d_in, _ = w_in.shape
    step_cols = _REGROUP_BLOCKS * LANE
    steps_per_tile = tn // step_cols
    assert tn % step_cols == 0 and NPROJ % tn == 0
    w_t = jnp.transpose(w_in, (0, 2, 1))
    src = jnp.asarray([s // SUBLANE for s in _regroup_plan()], jnp.int32)
    return pl.pallas_call(
        _regroup_kernel,
        grid_spec=pltpu.PrefetchScalarGridSpec(
            num_scalar_prefetch=1,
            grid=(depth, NPROJ // step_cols),
            in_specs=[pl.BlockSpec((None, pl.Element(LANE), pl.Element(d_in)),
                                   lambda l, t, src, i=i: (l, src[t * _REGROUP_BLOCKS + i] * SUBLANE, 0))
                      for i in range(_REGROUP_BLOCKS)],
            out_specs=pl.BlockSpec((None, None, d_in, step_cols),
                                   lambda l, t, src: (l, t // steps_per_tile, 0, t % steps_per_tile))),
        out_shape=jax.ShapeDtypeStruct((depth, NPROJ // tn, d_in, tn), BF16),
        compiler_params=_params(("arbitrary", "arbitrary"), 48),
        name="regroup",
    )(src, *([w_t] * _REGROUP_BLOCKS))


def _pad_lanes(v, n=LANE):
    return jnp.pad(v, (0, n - v.shape[0])).reshape(1, n)


def _rope_tables(pos):
    half = SWA_DH // 2
    inv = ROPE_THETA ** (-jnp.arange(half, dtype=F32) / half)
    ang = pos.astype(F32)[:, None] * inv[None, :]
    cos, sin = jnp.cos(ang), jnp.sin(ang)
    reps = LANE // SWA_DH
    return jnp.tile(jnp.concatenate([cos, cos], axis=1), (1, reps)), jnp.tile(jnp.concatenate([-sin, sin], axis=1), (1, reps))


class _Path:
    def __init__(self, n_seq, t_pad, valid, chunk, seqs, has_cache, n_hist, tm_in, tn_in, tm_out, tn_out, y_dtype):
        self.n_seq, self.t_pad, self.valid, self.chunk, self.seqs = n_seq, t_pad, valid, chunk, seqs
        self.has_cache, self.n_hist, self.y_dtype = has_cache, n_hist, y_dtype
        self.tm_in, self.tn_in, self.tm_out, self.tn_out = tm_in, tn_in, tm_out, tn_out


_MIXER_ORDER = (0, 1, 2, 3)


def _layer(x, layer, prm, states, l0, acc, tables, path):
    (g_pre, g_post, w_in, w_out, gate_bias, mnorm, sinks, conv_w, conv_b, dt_bias, a_log, d_skip, snorm,
     pool_lin, pool_scale) = prm
    c0, n0, m0, kc, vc, s0, conv0, p0 = states
    ca, na, ma, ka, va, sa, conva, pa = acc
    cos, sin = tables
    p = path
    proj = _inproj(x, g_pre, w_in, layer, p.tm_in, p.tn_in)
    L, valid, S = p.chunk, min(p.valid, p.chunk), p.seqs
    assert S == 1 or p.t_pad == L
    parts = [
        _mlstm(proj, gate_bias, mnorm, (c0, n0, m0), l0, (ca, na, ma), layer, p.n_seq, p.t_pad, L, valid, p.y_dtype, S),
        _swa(proj, cos, sin, sinks, (kc, vc), l0, (ka, va), layer, p.n_seq, p.t_pad, L, valid,
             p.has_cache, not p.has_cache, p.y_dtype, S),
        _ssd(proj, conv_w, conv_b, dt_bias, a_log, d_skip, snorm, (conv0, s0), l0, (conva, sa), layer,
             p.n_seq, p.t_pad, L, valid, p.y_dtype, S),
        _pool(proj, pool_lin, pool_scale, (p0,), l0, (pa,), layer, p.n_seq, p.t_pad, L, valid, p.n_hist, p.y_dtype, S)]
    outs = _fused_call([parts[i] for i in _MIXER_ORDER], (p.n_seq // S, p.t_pad // L), "mixers", 48, S)
    (ya, ca, na, ma), (yb, ka, va), (yc, conva, sa), (yd, pa) = [outs[_MIXER_ORDER.index(i)] for i in range(len(parts))]
    x = _outproj((ya, yb, yc, yd), w_out, layer, x, g_post, p.tm_out, p.tn_out)
    return x, (ca, na, ma, ka, va, sa, conva, pa)


def kernel(x_prompt, x_sample, state_mlstm_C, state_mlstm_n, state_mlstm_m, cache_swa_k, cache_swa_v, state_ssd,
           state_ssd_conv, state_pool, norm_pre, norm_post, w_in, w_out, mlstm_b_i, mlstm_b_f, mlstm_norm,
           swa_sinks, ssd_conv_w, ssd_conv_b, ssd_dt_bias, ssd_A_log, ssd_D, ssd_norm, pool_lin, pool_scale):
    bp, seq, _ = x_prompt.shape
    bs, dec_seq, _ = x_sample.shape
    t_s = SUBLANE * pl.cdiv(dec_seq, SUBLANE)

    assert WINDOW == SSD_CHUNK == 2 * MLSTM_CHUNK
    prompt = _Path(bp, seq, seq, WINDOW, 1, False, 0, 1024, 1280, 1024, 512, BF16)
    sample = _Path(bs, t_s, dec_seq, t_s, 4, True, POOL_HIST, bs * t_s, 1280, bs * t_s, 1024, F32)

    assert prompt.tn_in == sample.tn_in
    w_in_r = _regroup_w_in(w_in, prompt.tn_in)
    w_out_b = w_out.astype(BF16)

    xp = x_prompt.reshape(bp * seq, D_MODEL)
    xs = jnp.pad(x_sample, ((0, 0), (0, t_s - dec_seq), (0, 0))).reshape(bs * t_s, D_MODEL)

    tab_p = _rope_tables(jnp.arange(seq))
    tab_s = _rope_tables(PAST_LEN + jnp.arange(t_s))

    def state_shapes(layers, b):
        return ((layers, b, MLSTM_HEADS, MLSTM_DH, MLSTM_DH), (layers, b, MLSTM_HEADS, LANE), (layers, b, MLSTM_HEADS, LANE),
                (layers, b, WINDOW, LANE), (layers, b, WINDOW, LANE),
                (layers, b, SSD_HEADS, SSD_DH, SSD_DSTATE), (layers, b, SSD_CONV - 1, SSD_CONV_DIM),
                (layers, b, POOL_HIST, MIX))

    p_states0 = tuple(jnp.zeros(s, F32) for s in state_shapes(1, bp))
    s_states0 = (state_mlstm_C, state_mlstm_n,
                 jnp.broadcast_to(state_mlstm_m[..., None], (DEPTH, bs, MLSTM_HEADS, LANE)),
                 cache_swa_k.reshape(DEPTH, bs, WINDOW, LANE), cache_swa_v.reshape(DEPTH, bs, WINDOW, LANE),
                 state_ssd, state_ssd_conv, state_pool)
    p_acc = tuple(jnp.zeros(s, F32) for s in state_shapes(DEPTH, bp))
    s_acc = tuple(jnp.zeros(s, F32) for s in state_shapes(DEPTH, bs))

    for l in range(DEPTH):
        gate_bias = _pad_lanes(jnp.concatenate([mlstm_b_i[l], mlstm_b_f[l]]))
        prm = (norm_pre[l].reshape(1, D_MODEL), norm_post[l].reshape(1, D_MODEL), w_in_r, w_out_b,
               gate_bias, mlstm_norm[l].reshape(1, MIX), _pad_lanes(swa_sinks[l]),
               ssd_conv_w[l], ssd_conv_b[l].reshape(1, SSD_CONV_DIM), _pad_lanes(ssd_dt_bias[l]),
               _pad_lanes(ssd_A_log[l]), jnp.repeat(ssd_D[l], SSD_DH).reshape(1, MIX), ssd_norm[l].reshape(1, MIX),
               pool_lin[l], pool_scale[l].reshape(1, MIX))
        xp, p_acc = _layer(xp, l, prm, p_states0, 0, p_acc, tab_p, prompt)
        xs, s_acc = _layer(xs, l, prm, s_states0, l, s_acc, tab_s, sample)

    def finish(acc, b):
        c, n, m, k, v, s, conv, pool = acc
        kv_shape = (DEPTH, b, WINDOW, SWA_KV_HEADS, SWA_DH)
        return c, n, m[..., 0], k.reshape(kv_shape), v.reshape(kv_shape), s, conv, pool

    y_prompt = xp.reshape(bp, seq, D_MODEL)
    y_sample = xs.reshape(bs, t_s, D_MODEL)[:, :dec_seq]
    return (y_prompt, y_sample) + finish(p_acc, bp) + finish(s_acc, bs)
```

```python
import functools
import math

import jax
import jax.numpy as jnp
from jax import lax
from jax.experimental import pallas as pl
from jax.experimental.pallas import tpu as pltpu

F32 = jnp.float32
BF16 = jnp.bfloat16

D_MODEL = 4096
DEPTH = 4
PAST_LEN = 8192
MIX = D_MODEL // 4
MLSTM_DH = 128
MLSTM_HEADS = MIX // MLSTM_DH
MLSTM_CHUNK = 64
GATE_CAP = 15.0
SWA_DH = 64
SWA_HEADS = MIX // SWA_DH
SWA_KV_HEADS = 2
SWA_GROUP = SWA_HEADS // SWA_KV_HEADS
WINDOW = 128
ROPE_THETA = 10000.0
SSD_DH = 64
SSD_HEADS = MIX // SSD_DH
SSD_GROUPS = 4
SSD_HPG = SSD_HEADS // SSD_GROUPS
SSD_DSTATE = 128
SSD_CONV = 4
SSD_CONV_DIM = MIX + 2 * SSD_GROUPS * SSD_DSTATE
SSD_CHUNK = 128
POOL_WINDOWS = (2, 4, 8, 16)
POOL_GC = MIX // len(POOL_WINDOWS)
POOL_HIST = max(POOL_WINDOWS) - 1
RMS_EPS = 1e-6

LANE = 128
SUBLANE = 8
NEG = -1e30

_COL = {}
_off = 0
for _name, _width in (("qa", MIX), ("ka", MIX), ("va", MIX), ("oa", MIX), ("za", MIX),
                      ("qb", MIX), ("zb", MIX), ("zc", MIX), ("xbc", SSD_CONV_DIM),
                      ("ud", MIX), ("zd", MIX),
                      ("gate", LANE), ("kb", LANE), ("vb", LANE), ("dt", LANE)):
    _COL[_name] = (_off, _width)
    _off += _width
NPROJ = _off


def _colblock(name):
    off, width = _COL[name]
    assert off % width == 0
    return off // width


LOG2E = math.log2(math.e)


def _sigmoid(x):
    return 0.5 * jnp.tanh(0.5 * x) + 0.5


def _silu(x):
    half = 0.5 * x
    return half + half * jnp.tanh(half)


def _dot(a, b):
    return jnp.dot(a, b, preferred_element_type=F32)


def _dot_nt(a, b):
    return lax.dot_general(a, b, (((1,), (1,)), ((), ())), preferred_element_type=F32)


def _dot_tn(a, b):
    return lax.dot_general(a, b, (((0,), (0,)), ((), ())), preferred_element_type=F32)


def _cumsum_rows(x):
    n = x.shape[0]
    r = lax.broadcasted_iota(jnp.int32, (n, n), 0)
    c = lax.broadcasted_iota(jnp.int32, (n, n), 1)
    tri = (c <= r).astype(F32)
    return jnp.dot(tri, x, preferred_element_type=F32, precision=lax.Precision.HIGHEST)


def _params(sem, vmem_mb):
    return pltpu.CompilerParams(dimension_semantics=sem, vmem_limit_bytes=vmem_mb * 1024 * 1024)


_ANY = pl.BlockSpec(memory_space=pl.ANY)
_DONE = object()


class _Part:
    def __init__(self, body, operands, in_specs, out_specs, out_shape, n_alias, kinds, rows, scratch=()):
        self.body, self.operands, self.in_specs, self.out_specs = body, operands, in_specs, out_specs
        self.out_shape, self.n_alias, self.scratch = out_shape, n_alias, list(scratch)
        self.kinds, self.rows = kinds, rows
        assert len(kinds) == len(operands) + len(out_shape)


def _fused_call(parts, grid, name, vmem_mb, seqs_per_step=1):
    n_in = [len(p.operands) for p in parts]
    n_out = [len(p.out_shape) for p in parts]
    n_scr = [len(p.scratch) * seqs_per_step for p in parts]

    def view(ref, kind, rows, q):
        if kind == "rows":
            return ref.at[pl.ds(q * rows, rows)]
        return ref.at[pl.ds(q, 1)] if kind == "state" else ref

    def body(*refs):
        ins, outs, scr = refs[:sum(n_in)], refs[sum(n_in):sum(n_in) + sum(n_out)], refs[sum(n_in) + sum(n_out):]

        def run(first_chunk_init):
            pending, i, o, s = [], 0, 0, 0
            for k, p in enumerate(parts):
                io = list(ins[i:i + n_in[k]]) + list(outs[o:o + n_out[k]])
                per_seq = n_scr[k] // seqs_per_step
                for q in range(seqs_per_step):
                    pending.append(p.body(*[view(r, kind, p.rows, q) for r, kind in zip(io, p.kinds)],
                                          *scr[s + q * per_seq:s + (q + 1) * per_seq],
                                          first_chunk_init=first_chunk_init))
                i, o, s = i + n_in[k], o + n_out[k], s + n_scr[k]
            while pending:
                pending = [g for g in pending if next(g, _DONE) is not _DONE]

        pl.when(pl.program_id(len(grid) - 1) == 0)(functools.partial(run, True))
        run(False)

    aliases, i, o = {}, 0, 0
    for k, p in enumerate(parts):
        for a in range(p.n_alias):
            aliases[i + n_in[k] - p.n_alias + a] = o + 1 + a
        i, o = i + n_in[k], o + n_out[k]
    flat = pl.pallas_call(
        body,
        grid=grid,
        in_specs=[s for p in parts for s in p.in_specs],
        out_specs=[s for p in parts for s in p.out_specs],
        out_shape=[s for p in parts for s in p.out_shape],
        scratch_shapes=[s for p in parts for s in p.scratch * seqs_per_step],
        input_output_aliases=aliases,
        compiler_params=_params(("arbitrary",) * len(grid), vmem_mb),
        name=name,
    )(*[x for p in parts for x in p.operands])
    out, o = [], 0
    for k in range(len(parts)):
        out.append(list(flat[o:o + n_out[k]]))
        o += n_out[k]
    return out


def _state_spec(stacked, layer, seqs=1):
    tail = stacked.shape[2:]
    return pl.BlockSpec((None, seqs) + tail, lambda s, c: (layer, s) + (0,) * len(tail))


_NORM_STEPS = 8


def _inproj_kernel(x_ref, g_ref, *refs):
    w_refs, o_ref, h_ref = refs[:-2], refs[-2], refs[-1]
    band = D_MODEL // len(w_refs)
    p = pl.program_id(0)
    j = pl.program_id(1)
    rows = x_ref.shape[0]

    def normalise_slice():
        rs = pl.ds(pl.multiple_of(jnp.minimum(j, _NORM_STEPS - 1) * rows, rows), rows)
        x = x_ref[...]
        ms = jnp.mean(x * x, axis=-1, keepdims=True)
        h_ref[p % 2, rs, :] = (x * lax.rsqrt(ms + RMS_EPS) * g_ref[...]).astype(BF16)

    pl.when(p == 0)(normalise_slice)

    @pl.when(p > 0)
    def _():
        normalise_slice()
        out = None
        for i, w_ref in enumerate(w_refs):
            term = _dot(h_ref[(p - 1) % 2, :, i * band:(i + 1) * band], w_ref[...])
            out = term if out is None else out + term
        o_ref[...] = out


def _inproj(x, g, w_all, layer, tm, tn, w_bands):
    m = x.shape[0]
    n_row_tiles, n_col = m // tm, NPROJ // tn
    assert n_col >= _NORM_STEPS and tm % (_NORM_STEPS * 2 * SUBLANE) == 0
    col = lambda p, j: jnp.where(p > 0, j, 0)
    x_slice = lambda p, j: jnp.minimum(p, n_row_tiles - 1) * _NORM_STEPS + jnp.minimum(j, _NORM_STEPS - 1)
    return pl.pallas_call(
        _inproj_kernel,
        grid=(n_row_tiles + 1, n_col),
        in_specs=[pl.BlockSpec((tm // _NORM_STEPS, D_MODEL), lambda p, j: (x_slice(p, j), 0)),
                  pl.BlockSpec((1, D_MODEL), lambda p, j: (0, 0))]
        + [pl.BlockSpec((None, D_MODEL // w_bands, tn), lambda p, j, i=i: (layer, i, col(p, j))) for i in range(w_bands)],
        out_specs=pl.BlockSpec((tm, tn), lambda p, j: (jnp.maximum(p - 1, 0), col(p, j))),
        out_shape=jax.ShapeDtypeStruct((m, NPROJ), F32),
        scratch_shapes=[pltpu.VMEM((2, tm, D_MODEL), BF16)],
        compiler_params=_params(("arbitrary", "arbitrary"), 58),
        name="inproj",
    )(x, g, *([w_all] * w_bands))


def _outproj_kernel(ya_ref, yb_ref, yc_ref, yd_ref, w_ref, x_ref, g_ref, o_ref, acc_ref, ss_ref, *, tn, n_row_tiles):
    i = pl.program_id(0)
    j = pl.program_id(1)
    slot = i % 2
    col = pl.ds(pl.multiple_of(j * tn, tn), tn)

    def finalize():
        rs = lax.rsqrt(ss_ref[1 - slot] * (1.0 / D_MODEL) + RMS_EPS)
        o_ref[...] = x_ref[...] + acc_ref[:, col] * rs * g_ref[...]

    def multiply():
        out = None
        for part, r in enumerate((ya_ref, yb_ref, yc_ref, yd_ref)):
            term = _dot(r[...].astype(BF16), w_ref[part * MIX:(part + 1) * MIX, :])
            out = term if out is None else out + term
        acc_ref[:, col] = out
        sq = jnp.sum(out * out, axis=1, keepdims=True)
        ss_ref[slot] = jnp.where(j == 0, sq, ss_ref[slot] + sq)

    pl.when(i == 0)(multiply)
    pl.when(i == n_row_tiles)(finalize)

    @pl.when(jnp.logical_and(i > 0, i < n_row_tiles))
    def _():
        finalize()
        multiply()


def _outproj(ys, w_all, layer, x, g, tm, tn):
    m = x.shape[0]
    n_row_tiles = m // tm
    yspec = pl.BlockSpec((tm, MIX), lambda i, j: (jnp.minimum(i, n_row_tiles - 1), 0))
    lagged = pl.BlockSpec((tm, tn), lambda i, j: (jnp.maximum(i - 1, 0), jnp.where(i > 0, j, 0)))
    return pl.pallas_call(
        functools.partial(_outproj_kernel, tn=tn, n_row_tiles=n_row_tiles),
        grid=(n_row_tiles + 1, D_MODEL // tn),
        in_specs=[yspec, yspec, yspec, yspec,
                  pl.BlockSpec((None, D_MODEL, tn), lambda i, j: (layer, 0, j)),
                  lagged,
                  pl.BlockSpec((1, tn), lambda i, j: (0, j))],
        out_specs=lagged,
        out_shape=jax.ShapeDtypeStruct((m, D_MODEL), F32),
        scratch_shapes=[pltpu.VMEM((tm, D_MODEL), F32), pltpu.VMEM((2, tm, 1), F32)],
        compiler_params=_params(("arbitrary", "arbitrary"), 56),
        name="outproj",
    )(*ys, w_all, x, g)


def _mlstm_kernel(q_ref, k_ref, v_ref, o_ref, z_ref, gate_ref, bias_ref, mn_ref, c0_ref, n0_ref, m0_ref,
                  c_alias, n_alias, m_alias, y_ref, c_ref, n_ref, m_ref, *, L, valid, first_chunk_init):
    del c_alias, n_alias, m_alias
    if first_chunk_init:
        c_ref[...] = c0_ref[...]
        n_ref[...] = n0_ref[...]
        m_ref[...] = m0_ref[...]
        return

    row = lax.broadcasted_iota(jnp.int32, (L, LANE), 0)
    row_ok = row < valid
    gates = GATE_CAP * jnp.tanh((gate_ref[...] + bias_ref[...]) / GATE_CAP)
    logf = jnp.where(row_ok, jax.nn.log_sigmoid(gates), 0.0)
    b_all = _cumsum_rows(pltpu.roll(logf, LANE - MLSTM_HEADS, axis=1))
    r_all = jnp.where(row_ok, gates - b_all, NEG)
    r_all_t = r_all.T
    yield

    tr = lax.broadcasted_iota(jnp.int32, (L, L), 0)
    tc = lax.broadcasted_iota(jnp.int32, (L, L), 1)
    tril = tc <= tr

    heads = range(MLSTM_HEADS)
    sl = [slice(h * MLSTM_DH, (h + 1) * MLSTM_DH) for h in heads]
    q = [q_ref[:, sl[h]] * (MLSTM_DH ** -0.5) for h in heads]
    k = [k_ref[:, sl[h]] for h in heads]
    v = [v_ref[:, sl[h]] for h in heads]
    c_prev = [c_ref[0, h] for h in heads]
    n_prev = [n_ref[0, h:h + 1, :] for h in heads]
    m_prev = [m_ref[0, h:h + 1, 0:1] for h in heads]
    qk = [_dot_nt(q[h], k[h]) for h in heads]
    qc = [_dot(q[h], c_prev[h]) for h in heads]
    qn = [jnp.sum(q[h] * n_prev[h], axis=1, keepdims=True) for h in heads]
    yield

    b_col = [b_all[:, h:h + 1] for h in heads]
    dmat = [jnp.where(tril, b_col[h] + r_all_t[h:h + 1, :], NEG) for h in heads]
    inter = [b_col[h] + m_prev[h] for h in heads]
    mt = [jnp.maximum(inter[h], jnp.max(dmat[h], axis=1, keepdims=True)) for h in heads]
    yield
    s = [jnp.exp(dmat[h] - mt[h]) * qk[h] for h in heads]
    sv = [_dot(s[h], v[h]) for h in heads]
    yield
    w_prev_t = [jnp.exp(inter[h] - mt[h]) for h in heads]
    den = [w_prev_t[h] * qn[h] + jnp.sum(s[h], axis=1, keepdims=True) for h in heads]
    hh = [(w_prev_t[h] * qc[h] + sv[h]) / jnp.maximum(jnp.abs(den[h]), jnp.exp(-mt[h])) for h in heads]
    yield

    b_last = [b_all[L - 1:L, h:h + 1] for h in heads]
    dec = [b_last[h] + r_all[:, h:h + 1] for h in heads]
    m_new = [jnp.maximum(b_last[h] + m_prev[h], jnp.max(dec[h], axis=0, keepdims=True)) for h in heads]
    w_prev = [jnp.exp(b_last[h] + m_prev[h] - m_new[h]) for h in heads]
    kw = [k[h] * jnp.exp(dec[h] - m_new[h]) for h in heads]
    c_new = [w_prev[h] * c_prev[h] + _dot_tn(kw[h], v[h]) for h in heads]
    n_new = [w_prev[h] * n_prev[h] + jnp.sum(kw[h], axis=0, keepdims=True) for h in heads]
    yield

    hn = [hh[h] * lax.rsqrt(jnp.mean(hh[h] * hh[h], axis=1, keepdims=True) + RMS_EPS) * mn_ref[:, sl[h]] for h in heads]
    ys = [hn[h] * _sigmoid(o_ref[:, sl[h]]) * _silu(z_ref[:, sl[h]]) for h in heads]
    yield

    y_ref[...] = jnp.concatenate(ys, axis=1).astype(y_ref.dtype)
    for h in heads:
        c_ref[0, h] = c_new[h]
    n_ref[0] = jnp.concatenate(n_new, axis=0)
    m_ref[0] = jnp.concatenate([jnp.broadcast_to(m_new[h], (1, LANE)) for h in heads], axis=0)


def _mlstm(proj, bias, mnorm, st0, l0, acc, layer, n_seq, t_pad, L, valid, y_dtype, S):
    nc = t_pad // L
    wide = lambda name: pl.BlockSpec((S * L, MIX), lambda s, c, cb=_colblock(name): (s * nc + c, cb))
    vec = lambda n: pl.BlockSpec((1, n), lambda s, c: (0, 0))
    return _Part(
        functools.partial(_mlstm_kernel, L=L, valid=valid),
        operands=[proj, proj, proj, proj, proj, proj, bias, mnorm, *st0, *acc],
        in_specs=[wide("qa"), wide("ka"), wide("va"), wide("oa"), wide("za"),
                  pl.BlockSpec((S * L, LANE), lambda s, c, cb=_colblock("gate"): (s * nc + c, cb)),
                  vec(LANE), vec(MIX)] + [_state_spec(a, l0, S) for a in st0] + [_ANY] * len(acc),
        out_specs=[pl.BlockSpec((S * L, MIX), lambda s, c: (s * nc + c, 0))] + [_state_spec(a, layer, S) for a in acc],
        out_shape=[jax.ShapeDtypeStruct((n_seq * t_pad, MIX), y_dtype)]
        + [jax.ShapeDtypeStruct(a.shape, F32) for a in acc],
        n_alias=len(acc),
        kinds=["rows"] * 6 + ["whole"] * 2 + ["state"] * 3 + ["whole"] * 3 + ["rows"] + ["state"] * 3, rows=L)


def _swa_kernel(q_ref, k_ref, v_ref, z_ref, cos_ref, sin_ref, sink_ref, kc_ref, vc_ref, k_alias, v_alias,
                y_ref, ko_ref, vo_ref, *, Lq, valid, has_cache, first_chunk_init):
    del k_alias, v_alias
    if first_chunk_init:
        ko_ref[...] = kc_ref[...]
        vo_ref[...] = vc_ref[...]
        return
    blk = pl.program_id(1)

    cos = cos_ref[...]
    sin = sin_ref[...]
    lane = lax.broadcasted_iota(jnp.int32, (Lq, LANE), 1)
    first_half = (lane % SWA_DH) < (SWA_DH // 2)

    def rope(x):
        partner = jnp.where(first_half, pltpu.roll(x, LANE - SWA_DH // 2, axis=1), pltpu.roll(x, SWA_DH // 2, axis=1))
        return x * cos + partner * sin

    k_cur = rope(k_ref[...])
    v_cur = v_ref[...]
    k_prev = ko_ref[0]
    v_prev = vo_ref[0]
    kk = jnp.concatenate([k_prev, k_cur], axis=0)
    vv = jnp.concatenate([v_prev, v_cur], axis=0)

    rows, nk = SWA_GROUP * Lq, WINDOW + Lq
    t = lax.broadcasted_iota(jnp.int32, (rows, nk), 0) & (Lq - 1)
    c = lax.broadcasted_iota(jnp.int32, (rows, nk), 1)
    prev_ok = jnp.logical_or(has_cache, blk > 0)
    mask = ((c < WINDOW) & (c > t) & prev_ok) | ((c >= WINDOW) & (c - WINDOW <= t) & (c - WINDOW < valid))

    yield
    pairs_per_group = SWA_GROUP // 2
    groups = range(SWA_KV_HEADS)
    gsl = [slice(g * SWA_DH, (g + 1) * SWA_DH) for g in groups]
    q_pairs = [[rope(q_ref[:, (g * pairs_per_group + p) * LANE:(g * pairs_per_group + p + 1) * LANE])
                * (SWA_DH ** -0.5 * LOG2E) for p in range(pairs_per_group)] for g in groups]
    q_st = [jnp.concatenate([q_pairs[g][h // 2][:, (h % 2) * SWA_DH:(h % 2 + 1) * SWA_DH]
                             for h in range(SWA_GROUP)], axis=0) for g in groups]
    sink = [jnp.concatenate([jnp.broadcast_to(sink_ref[0:1, g * SWA_GROUP + h:g * SWA_GROUP + h + 1], (Lq, 1))
                             for h in range(SWA_GROUP)], axis=0) * LOG2E for g in groups]
    yield
    s = [jnp.where(mask, _dot_nt(q_st[g], kk[:, gsl[g]]), NEG) for g in groups]
    mx = [jnp.maximum(jnp.max(s[g], axis=1, keepdims=True), sink[g]) for g in groups]
    yield
    p = [jnp.exp2(s[g] - mx[g]) for g in groups]
    den = [jnp.sum(p[g], axis=1, keepdims=True) + jnp.exp2(sink[g] - mx[g]) for g in groups]
    yield
    o = [_dot(p[g], vv[:, gsl[g]]) / den[g] for g in groups]
    yield
    y_chunks = []
    for g in groups:
        for pr in range(pairs_per_group):
            csl = slice((g * pairs_per_group + pr) * LANE, (g * pairs_per_group + pr + 1) * LANE)
            pair = jnp.concatenate([o[g][(2 * pr) * Lq:(2 * pr + 1) * Lq], o[g][(2 * pr + 1) * Lq:(2 * pr + 2) * Lq]],
                                   axis=1)
            y_chunks.append(pair * _silu(z_ref[:, csl]))
    y_ref[...] = jnp.concatenate(y_chunks, axis=1).astype(y_ref.dtype)

    if valid == WINDOW:
        ko_ref[0] = k_cur
        vo_ref[0] = v_cur
    else:
        ko_ref[0, 0:WINDOW - valid, :] = k_prev[valid:, :]
        ko_ref[0, WINDOW - valid:WINDOW, :] = k_cur[0:valid, :]
        vo_ref[0, 0:WINDOW - valid, :] = v_prev[valid:, :]
        vo_ref[0, WINDOW - valid:WINDOW, :] = v_cur[0:valid, :]


def _swa(proj, cos, sin, sinks, st0, l0, acc, layer, n_seq, t_pad, Lq, valid, has_cache, table_per_block, y_dtype, S):
    nb = t_pad // Lq
    rowblk = lambda s, b: s * nb + b
    wide = lambda name: pl.BlockSpec((S * Lq, MIX), lambda s, b, cb=_colblock(name): (rowblk(s, b), cb))
    narrow = lambda name: pl.BlockSpec((S * Lq, LANE), lambda s, b, cb=_colblock(name): (rowblk(s, b), cb))
    table = pl.BlockSpec((Lq, LANE), (lambda s, b: (b, 0)) if table_per_block else (lambda s, b: (0, 0)))
    return _Part(
        functools.partial(_swa_kernel, Lq=Lq, valid=valid, has_cache=has_cache),
        operands=[proj, proj, proj, proj, cos, sin, sinks, *st0, *acc],
        in_specs=[wide("qb"), narrow("kb"), narrow("vb"), wide("zb"), table, table,
                  pl.BlockSpec((1, LANE), lambda s, b: (0, 0))] + [_state_spec(a, l0, S) for a in st0] + [_ANY] * len(acc),
        out_specs=[pl.BlockSpec((S * Lq, MIX), lambda s, b: (rowblk(s, b), 0))] + [_state_spec(a, layer, S) for a in acc],
        out_shape=[jax.ShapeDtypeStruct((n_seq * t_pad, MIX), y_dtype)]
        + [jax.ShapeDtypeStruct(a.shape, F32) for a in acc],
        n_alias=len(acc),
        kinds=["rows"] * 4 + ["whole"] * 3 + ["state"] * 2 + ["whole"] * 2 + ["rows"] + ["state"] * 2, rows=Lq)


_CONV_PAD = SUBLANE


def _spread_heads(x, e):
    lane = lax.broadcasted_iota(jnp.int32, x.shape, 1)
    x = jnp.where(lane < SSD_HEADS, x, 0.0)
    hi = x.astype(BF16)
    rest = x - hi.astype(F32)
    mid = rest.astype(BF16)
    lo = (rest - mid.astype(F32)).astype(BF16)
    return _dot(hi, e) + _dot(mid, e) + _dot(lo, e)


def _ssd_kernel(xbc_ref, z_ref, dt_ref, cw_ref, cb_ref, dtb_ref, alog_ref, dskip_ref, sn_ref, e_ref, conv0_ref, s0_ref,
                conv_alias, s_alias, y_ref, convo_ref, so_ref, full_ref, *, L, valid, first_chunk_init):
    del conv_alias, s_alias
    hist = SSD_CONV - 1
    if first_chunk_init:
        so_ref[...] = s0_ref[...]
        full_ref[_CONV_PAD - hist:_CONV_PAD, :] = conv0_ref[0]
        return

    full_ref[_CONV_PAD:_CONV_PAD + L, :] = xbc_ref[...]
    full = full_ref[...]
    acc = cb_ref[...] + full[_CONV_PAD:, :] * cw_ref[hist:hist + 1, :]
    for j in range(hist):
        acc = acc + pltpu.roll(full, hist - j, axis=0)[_CONV_PAD:, :] * cw_ref[j:j + 1, :]
    xc = _silu(acc)
    new_hist = full_ref[_CONV_PAD + valid - hist:_CONV_PAD + valid, :]
    convo_ref[0] = new_hist
    full_ref[_CONV_PAD - hist:_CONV_PAD, :] = new_hist
    yield

    nbc = SSD_GROUPS * SSD_DSTATE
    row_ok = lax.broadcasted_iota(jnp.int32, (L, LANE), 0) < valid
    dtv = jnp.where(row_ok, jax.nn.softplus(dt_ref[...] + dtb_ref[...]), 0.0)
    a = dtv * (-jnp.exp(alog_ref[...]) * LOG2E)
    cum = _cumsum_rows(a)
    cum_t = cum.T
    tr = lax.broadcasted_iota(jnp.int32, (L, L), 0)
    tc = lax.broadcasted_iota(jnp.int32, (L, L), 1)
    tril = tc <= tr
    yield

    groups, heads = range(SSD_GROUPS), range(SSD_HEADS)
    gp = SSD_HPG * SSD_DH
    bg = [xc[:, MIX + g * SSD_DSTATE:MIX + (g + 1) * SSD_DSTATE] for g in groups]
    cg = [xc[:, MIX + nbc + g * SSD_DSTATE:MIX + nbc + (g + 1) * SSD_DSTATE] for g in groups]
    st = [so_ref[0, g * SSD_HPG:(g + 1) * SSD_HPG].reshape(gp, SSD_DSTATE) for g in groups]
    cb = [_dot_nt(cg[g], bg[g]) for g in groups]
    cst = [_dot_nt(cg[g], st[g]) for g in groups]
    yield
    c_col = [cum[:, h:h + 1] for h in heads]
    c_last = [cum[L - 1:L, h:h + 1] for h in heads]
    lm = [jnp.where(tril, jnp.exp2(jnp.where(tril, c_col[h] - cum_t[h:h + 1, :], 0.0)), 0.0) for h in heads]

    spread = _spread_heads(jnp.concatenate([dtv, jnp.exp2(cum), jnp.exp2(cum[L - 1:L, :] - cum)], axis=0), e_ref[...])
    xs = xc[:, :MIX]
    xdt = xs * spread[0:L]
    xw = xdt * spread[2 * L:3 * L]
    yield
    low_half = lax.broadcasted_iota(jnp.int32, (L, LANE), 1) < SSD_DH
    y_in = []
    for pair in range(SSD_HEADS // 2):
        g = (2 * pair) // SSD_HPG
        x_pair = xdt[:, pair * LANE:(pair + 1) * LANE]
        y_in.append(jnp.where(low_half, _dot(cb[g] * lm[2 * pair], x_pair), _dot(cb[g] * lm[2 * pair + 1], x_pair)))
    y_state = jnp.concatenate(cst, axis=1) * spread[L:2 * L]
    yield
    upd = [_dot_tn(xw[:, g * gp:(g + 1) * gp], bg[g]) for g in groups]
    for h in heads:
        g, hh = divmod(h, SSD_HPG)
        so_ref[0, h] = jnp.exp2(c_last[h]) * st[g][hh * SSD_DH:(hh + 1) * SSD_DH] + upd[g][hh * SSD_DH:(hh + 1) * SSD_DH]
    yc = (jnp.concatenate(y_in, axis=1) + y_state + dskip_ref[...] * xs) * _silu(z_ref[...])
    y_ref[...] = (yc * lax.rsqrt(jnp.mean(yc * yc, axis=1, keepdims=True) + RMS_EPS) * sn_ref[...]).astype(y_ref.dtype)


def _ssd(proj, conv_w, conv_b, dt_bias, a_log, d_skip, snorm, st0, l0, acc, layer, n_seq, t_pad, L, valid, y_dtype, S):
    nc = t_pad // L
    rowblk = lambda s, c: s * nc + c
    vec = lambda n: pl.BlockSpec((1, n), lambda s, c: (0, 0))
    spread = (jnp.arange(MIX)[None, :] // SSD_DH == jnp.arange(LANE)[:, None]).astype(BF16)
    return _Part(
        functools.partial(_ssd_kernel, L=L, valid=valid),
        operands=[proj, proj, proj, conv_w, conv_b, dt_bias, a_log, d_skip, snorm, spread, *st0, *acc],
        in_specs=[pl.BlockSpec((S * L, SSD_CONV_DIM), lambda s, c, cb=_colblock("xbc"): (rowblk(s, c), cb)),
                  pl.BlockSpec((S * L, MIX), lambda s, c, cb=_colblock("zc"): (rowblk(s, c), cb)),
                  pl.BlockSpec((S * L, LANE), lambda s, c, cb=_colblock("dt"): (rowblk(s, c), cb)),
                  pl.BlockSpec((SSD_CONV, SSD_CONV_DIM), lambda s, c: (0, 0)),
                  vec(SSD_CONV_DIM), vec(LANE), vec(LANE), vec(MIX), vec(MIX),
                  pl.BlockSpec((LANE, MIX), lambda s, c: (0, 0))]
        + [_state_spec(a, l0, S) for a in st0] + [_ANY] * len(acc),
        out_specs=[pl.BlockSpec((S * L, MIX), lambda s, c: (rowblk(s, c), 0))] + [_state_spec(a, layer, S) for a in acc],
        out_shape=[jax.ShapeDtypeStruct((n_seq * t_pad, MIX), y_dtype)]
        + [jax.ShapeDtypeStruct(a.shape, F32) for a in acc],
        n_alias=len(acc),
        kinds=["rows"] * 3 + ["whole"] * 7 + ["state"] * 2 + ["whole"] * 2 + ["rows"] + ["state"] * 2, rows=L,
        scratch=[pltpu.VMEM((_CONV_PAD + L, SSD_CONV_DIM), F32)])


_POOL_PAD = 2 * SUBLANE
assert all(w & (w - 1) == 0 for w in POOL_WINDOWS) and POOL_HIST < _POOL_PAD


def _pool_kernel(u_ref, z_ref, lin_ref, scale_ref, p0_ref, p_alias, y_ref, po_ref, full_ref, *, L, valid, n_hist,
                 first_chunk_init):
    del p_alias
    if first_chunk_init:
        full_ref[_POOL_PAD - POOL_HIST:_POOL_PAD, :] = p0_ref[0]
        return
    step = pl.program_id(1)

    full_ref[_POOL_PAD:_POOL_PAD + L, :] = u_ref[...]
    pos = n_hist + step * L + lax.broadcasted_iota(jnp.int32, (L, 1), 0)
    ys = []
    for g, w in enumerate(POOL_WINDOWS):
        sl = slice(g * POOL_GC, (g + 1) * POOL_GC)
        rows = full_ref[:, sl]
        tot, span = rows, 1
        while span < w:
            tot = tot + pltpu.roll(tot, span, axis=0)
            span *= 2
        cur = rows[_POOL_PAD:]
        cnt = jnp.minimum(pos + 1, w).astype(F32)
        d = tot[_POOL_PAD:] / cnt - cur
        ys.append(_dot(d, lin_ref[g]) * scale_ref[:, sl] * _silu(z_ref[:, sl]))
        yield
    y_ref[...] = jnp.concatenate(ys, axis=1).astype(y_ref.dtype)
    new_hist = full_ref[_POOL_PAD + valid - POOL_HIST:_POOL_PAD + valid, :]
    po_ref[0] = new_hist
    full_ref[_POOL_PAD - POOL_HIST:_POOL_PAD, :] = new_hist


def _pool(proj, lin, scale, st0, l0, acc, layer, n_seq, t_pad, L, valid, n_hist, y_dtype, S):
    nt = t_pad // L
    rowblk = lambda s, c: s * nt + c
    return _Part(
        functools.partial(_pool_kernel, L=L, valid=valid, n_hist=n_hist),
        operands=[proj, proj, lin, scale, *st0, *acc],
        in_specs=[pl.BlockSpec((S * L, MIX), lambda s, c, cb=_colblock("ud"): (rowblk(s, c), cb)),
                  pl.BlockSpec((S * L, MIX), lambda s, c, cb=_colblock("zd"): (rowblk(s, c), cb)),
                  pl.BlockSpec((len(POOL_WINDOWS), POOL_GC, POOL_GC), lambda s, c: (0, 0, 0)),
                  pl.BlockSpec((1, MIX), lambda s, c: (0, 0))] + [_state_spec(a, l0, S) for a in st0] + [_ANY] * len(acc),
        out_specs=[pl.BlockSpec((S * L, MIX), lambda s, c: (rowblk(s, c), 0))] + [_state_spec(a, layer, S) for a in acc],
        out_shape=[jax.ShapeDtypeStruct((n_seq * t_pad, MIX), y_dtype)]
        + [jax.ShapeDtypeStruct(a.shape, F32) for a in acc],
        n_alias=len(acc),
        kinds=["rows"] * 2 + ["whole"] * 2 + ["state", "whole", "rows", "state"], rows=L,
        scratch=[pltpu.VMEM((_POOL_PAD + L, MIX), F32)])


_REF_SIZES = (MIX, MIX, MIX, MIX, MIX, MLSTM_HEADS, MLSTM_HEADS,
              SWA_HEADS * SWA_DH, SWA_KV_HEADS * SWA_DH, SWA_KV_HEADS * SWA_DH, MIX,
              MIX, SSD_CONV_DIM, SSD_HEADS, MIX, MIX)
_REF_NAMES = ("qa", "ka", "va", "oa", "za", "ia", "fa", "qb", "kb", "vb", "zb", "zc", "xbc", "dt", "ud", "zd")


def _regroup_plan():
    ref_off, o = {}, 0
    for name, size in zip(_REF_NAMES, _REF_SIZES):
        ref_off[name] = o
        o += size
    src = []
    for name, (off, width) in _COL.items():
        start = ref_off["ia"] if name == "gate" else ref_off[name]
        src += [start + b * LANE for b in range(width // LANE)]
    assert len(src) == NPROJ // LANE and all(s % SUBLANE == 0 and s + LANE <= o for s in src)
    return src


_REGROUP_BLOCKS = 4


def _regroup_kernel(src_tab, *refs):
    del src_tab
    o_ref = refs[-1]
    for i, wt_ref in enumerate(refs[:-1]):
        o_ref[:, i * LANE:(i + 1) * LANE] = wt_ref[...].T.astype(BF16)


def _regroup_w_in(w_in):
    depth, d_in, _ = w_in.shape
    w_t = jnp.transpose(w_in, (0, 2, 1))
    src = jnp.asarray([s // SUBLANE for s in _regroup_plan()], jnp.int32)
    return pl.pallas_call(
        _regroup_kernel,
        grid_spec=pltpu.PrefetchScalarGridSpec(
            num_scalar_prefetch=1,
            grid=(depth, NPROJ // (_REGROUP_BLOCKS * LANE)),
            in_specs=[pl.BlockSpec((None, pl.Element(LANE), pl.Element(d_in)),
                                   lambda l, t, src, i=i: (l, src[t * _REGROUP_BLOCKS + i] * SUBLANE, 0))
                      for i in range(_REGROUP_BLOCKS)],
            out_specs=pl.BlockSpec((None, d_in, _REGROUP_BLOCKS * LANE), lambda l, t, src: (l, 0, t))),
        out_shape=jax.ShapeDtypeStruct((depth, d_in, NPROJ), BF16),
        compiler_params=_params(("arbitrary", "arbitrary"), 48),
        name="regroup",
    )(src, *([w_t] * _REGROUP_BLOCKS))


def _pad_lanes(v, n=LANE):
    return jnp.pad(v, (0, n - v.shape[0])).reshape(1, n)


def _rope_tables(pos):
    half = SWA_DH // 2
    inv = ROPE_THETA ** (-jnp.arange(half, dtype=F32) / half)
    ang = pos.astype(F32)[:, None] * inv[None, :]
    cos, sin = jnp.cos(ang), jnp.sin(ang)
    reps = LANE // SWA_DH
    return jnp.tile(jnp.concatenate([cos, cos], axis=1), (1, reps)), jnp.tile(jnp.concatenate([-sin, sin], axis=1), (1, reps))


class _Path:
    def __init__(self, n_seq, t_pad, valid, chunk, seqs, has_cache, n_hist, tm_in, tn_in, w_bands, tm_out, tn_out,
                 y_dtype):
        self.n_seq, self.t_pad, self.valid, self.chunk, self.seqs = n_seq, t_pad, valid, chunk, seqs
        self.has_cache, self.n_hist, self.y_dtype = has_cache, n_hist, y_dtype
        self.tm_in, self.tn_in, self.w_bands, self.tm_out, self.tn_out = tm_in, tn_in, w_bands, tm_out, tn_out


_MIXER_ORDER = (0, 1, 2, 3)


def _layer(x, layer, prm, states, l0, acc, tables, path):
    (g_pre, g_post, w_in, w_out, gate_bias, mnorm, sinks, conv_w, conv_b, dt_bias, a_log, d_skip, snorm,
     pool_lin, pool_scale) = prm
    c0, n0, m0, kc, vc, s0, conv0, p0 = states
    ca, na, ma, ka, va, sa, conva, pa = acc
    cos, sin = tables
    p = path
    proj = _inproj(x, g_pre, w_in, layer, p.tm_in, p.tn_in, p.w_bands)
    L, valid, S = p.chunk, min(p.valid, p.chunk), p.seqs
    assert S == 1 or p.t_pad == L
    parts = [
        _mlstm(proj, gate_bias, mnorm, (c0, n0, m0), l0, (ca, na, ma), layer, p.n_seq, p.t_pad, L, valid, p.y_dtype, S),
        _swa(proj, cos, sin, sinks, (kc, vc), l0, (ka, va), layer, p.n_seq, p.t_pad, L, valid,
             p.has_cache, not p.has_cache, p.y_dtype, S),
        _ssd(proj, conv_w, conv_b, dt_bias, a_log, d_skip, snorm, (conv0, s0), l0, (conva, sa), layer,
             p.n_seq, p.t_pad, L, valid, p.y_dtype, S),
        _pool(proj, pool_lin, pool_scale, (p0,), l0, (pa,), layer, p.n_seq, p.t_pad, L, valid, p.n_hist, p.y_dtype, S)]
    outs = _fused_call([parts[i] for i in _MIXER_ORDER], (p.n_seq // S, p.t_pad // L), "mixers", 48, S)
    (ya, ca, na, ma), (yb, ka, va), (yc, conva, sa), (yd, pa) = [outs[_MIXER_ORDER.index(i)] for i in range(len(parts))]
    x = _outproj((ya, yb, yc, yd), w_out, layer, x, g_post, p.tm_out, p.tn_out)
    return x, (ca, na, ma, ka, va, sa, conva, pa)


def kernel(x_prompt, x_sample, state_mlstm_C, state_mlstm_n, state_mlstm_m, cache_swa_k, cache_swa_v, state_ssd,
           state_ssd_conv, state_pool, norm_pre, norm_post, w_in, w_out, mlstm_b_i, mlstm_b_f, mlstm_norm,
           swa_sinks, ssd_conv_w, ssd_conv_b, ssd_dt_bias, ssd_A_log, ssd_D, ssd_norm, pool_lin, pool_scale):
    bp, seq, _ = x_prompt.shape
    bs, dec_seq, _ = x_sample.shape
    t_s = SUBLANE * pl.cdiv(dec_seq, SUBLANE)

    assert WINDOW == SSD_CHUNK == 2 * MLSTM_CHUNK
    prompt = _Path(bp, seq, seq, WINDOW, 1, False, 0, 1024, 1280, 1, 1024, 512, BF16)
    sample = _Path(bs, t_s, dec_seq, t_s, 4, True, POOL_HIST, bs * t_s, 1280, 4, bs * t_s, 512, F32)

    w_in_r = _regroup_w_in(w_in)
    w_out_b = w_out.astype(BF16)

    xp = x_prompt.reshape(bp * seq, D_MODEL)
    xs = jnp.pad(x_sample, ((0, 0), (0, t_s - dec_seq), (0, 0))).reshape(bs * t_s, D_MODEL)

    tab_p = _rope_tables(jnp.arange(seq))
    tab_s = _rope_tables(PAST_LEN + jnp.arange(t_s))

    def state_shapes(layers, b):
        return ((layers, b, MLSTM_HEADS, MLSTM_DH, MLSTM_DH), (layers, b, MLSTM_HEADS, LANE), (layers, b, MLSTM_HEADS, LANE),
                (layers, b, WINDOW, LANE), (layers, b, WINDOW, LANE),
                (layers, b, SSD_HEADS, SSD_DH, SSD_DSTATE), (layers, b, SSD_CONV - 1, SSD_CONV_DIM),
                (layers, b, POOL_HIST, MIX))

    p_states0 = tuple(jnp.zeros(s, F32) for s in state_shapes(1, bp))
    s_states0 = (state_mlstm_C, state_mlstm_n,
                 jnp.broadcast_to(state_mlstm_m[..., None], (DEPTH, bs, MLSTM_HEADS, LANE)),
                 cache_swa_k.reshape(DEPTH, bs, WINDOW, LANE), cache_swa_v.reshape(DEPTH, bs, WINDOW, LANE),
                 state_ssd, state_ssd_conv, state_pool)
    p_acc = tuple(jnp.zeros(s, F32) for s in state_shapes(DEPTH, bp))
    s_acc = tuple(jnp.zeros(s, F32) for s in state_shapes(DEPTH, bs))

    for l in range(DEPTH):
        gate_bias = _pad_lanes(jnp.concatenate([mlstm_b_i[l], mlstm_b_f[l]]))
        prm = (norm_pre[l].reshape(1, D_MODEL), norm_post[l].reshape(1, D_MODEL), w_in_r, w_out_b,
               gate_bias, mlstm_norm[l].reshape(1, MIX), _pad_lanes(swa_sinks[l]),
               ssd_conv_w[l], ssd_conv_b[l].reshape(1, SSD_CONV_DIM), _pad_lanes(ssd_dt_bias[l]),
               _pad_lanes(ssd_A_log[l]), jnp.repeat(ssd_D[l], SSD_DH).reshape(1, MIX), ssd_norm[l].reshape(1, MIX),
               pool_lin[l], pool_scale[l].reshape(1, MIX))
        xp, p_acc = _layer(xp, l, prm, p_states0, 0, p_acc, tab_p, prompt)
        xs, s_acc = _layer(xs, l, prm, s_states0, l, s_acc, tab_s, sample)

    def finish(acc, b):
        c, n, m, k, v, s, conv, pool = acc
        kv_shape = (DEPTH, b, WINDOW, SWA_KV_HEADS, SWA_DH)
        return c, n, m[..., 0], k.reshape(kv_shape), v.reshape(kv_shape), s, conv, pool

    y_prompt = xp.reshape(bp, seq, D_MODEL)
    y_sample = xs.reshape(bs, t_s, D_MODEL)[:, :dec_seq]
    return (y_prompt, y_sample) + finish(p_acc, bp) + finish(s_acc, bs)
```

```python
import functools
import math

import jax
import jax.numpy as jnp
from jax import lax
from jax.experimental import pallas as pl
from jax.experimental.pallas import tpu as pltpu

F32 = jnp.float32
BF16 = jnp.bfloat16

D_MODEL = 4096
DEPTH = 4
PAST_LEN = 8192
MIX = D_MODEL // 4
MLSTM_DH = 128
MLSTM_HEADS = MIX // MLSTM_DH
MLSTM_CHUNK = 64
GATE_CAP = 15.0
SWA_DH = 64
SWA_HEADS = MIX // SWA_DH
SWA_KV_HEADS = 2
SWA_GROUP = SWA_HEADS // SWA_KV_HEADS
WINDOW = 128
ROPE_THETA = 10000.0
SSD_DH = 64
SSD_HEADS = MIX // SSD_DH
SSD_GROUPS = 4
SSD_HPG = SSD_HEADS // SSD_GROUPS
SSD_DSTATE = 128
SSD_CONV = 4
SSD_CONV_DIM = MIX + 2 * SSD_GROUPS * SSD_DSTATE
SSD_CHUNK = 128
POOL_WINDOWS = (2, 4, 8, 16)
POOL_GC = MIX // len(POOL_WINDOWS)
POOL_HIST = max(POOL_WINDOWS) - 1
RMS_EPS = 1e-6

LANE = 128
SUBLANE = 8
NEG = -1e30

_COL = {}
_off = 0
for _name, _width in (("qa", MIX), ("ka", MIX), ("va", MIX), ("oa", MIX), ("za", MIX),
                      ("qb", MIX), ("zb", MIX), ("zc", MIX), ("xbc", SSD_CONV_DIM),
                      ("ud", MIX), ("zd", MIX),
                      ("gate", LANE), ("kb", LANE), ("vb", LANE), ("dt", LANE)):
    _COL[_name] = (_off, _width)
    _off += _width
NPROJ = _off


def _colblock(name):
    off, width = _COL[name]
    assert off % width == 0
    return off // width


LOG2E = math.log2(math.e)


def _sigmoid(x):
    return 0.5 * jnp.tanh(0.5 * x) + 0.5


def _silu(x):
    half = 0.5 * x
    return half + half * jnp.tanh(half)


def _dot(a, b):
    return jnp.dot(a, b, preferred_element_type=F32)


def _dot_nt(a, b):
    return lax.dot_general(a, b, (((1,), (1,)), ((), ())), preferred_element_type=F32)


def _dot_tn(a, b):
    return lax.dot_general(a, b, (((0,), (0,)), ((), ())), preferred_element_type=F32)


def _cumsum_rows(x):
    n = x.shape[0]
    r = lax.broadcasted_iota(jnp.int32, (n, n), 0)
    c = lax.broadcasted_iota(jnp.int32, (n, n), 1)
    tri = (c <= r).astype(F32)
    return jnp.dot(tri, x, preferred_element_type=F32, precision=lax.Precision.HIGHEST)


def _params(sem, vmem_mb):
    return pltpu.CompilerParams(dimension_semantics=sem, vmem_limit_bytes=vmem_mb * 1024 * 1024)


_ANY = pl.BlockSpec(memory_space=pl.ANY)
_DONE = object()


class _Part:
    def __init__(self, body, operands, in_specs, out_specs, out_shape, n_alias, kinds, rows, scratch=()):
        self.body, self.operands, self.in_specs, self.out_specs = body, operands, in_specs, out_specs
        self.out_shape, self.n_alias, self.scratch = out_shape, n_alias, list(scratch)
        self.kinds, self.rows = kinds, rows
        assert len(kinds) == len(operands) + len(out_shape)


def _fused_call(parts, grid, name, vmem_mb, seqs_per_step=1):
    n_in = [len(p.operands) for p in parts]
    n_out = [len(p.out_shape) for p in parts]
    n_scr = [len(p.scratch) * seqs_per_step for p in parts]

    def view(ref, kind, rows, q):
        if kind == "rows":
            return ref.at[pl.ds(q * rows, rows)]
        return ref.at[pl.ds(q, 1)] if kind == "state" else ref

    def body(*refs):
        ins, outs, scr = refs[:sum(n_in)], refs[sum(n_in):sum(n_in) + sum(n_out)], refs[sum(n_in) + sum(n_out):]

        def run(first_chunk_init):
            pending, i, o, s = [], 0, 0, 0
            for k, p in enumerate(parts):
                io = list(ins[i:i + n_in[k]]) + list(outs[o:o + n_out[k]])
                per_seq = n_scr[k] // seqs_per_step
                for q in range(seqs_per_step):
                    pending.append(p.body(*[view(r, kind, p.rows, q) for r, kind in zip(io, p.kinds)],
                                          *scr[s + q * per_seq:s + (q + 1) * per_seq],
                                          first_chunk_init=first_chunk_init))
                i, o, s = i + n_in[k], o + n_out[k], s + n_scr[k]
            while pending:
                pending = [g for g in pending if next(g, _DONE) is not _DONE]

        pl.when(pl.program_id(len(grid) - 1) == 0)(functools.partial(run, True))
        run(False)

    aliases, i, o = {}, 0, 0
    for k, p in enumerate(parts):
        for a in range(p.n_alias):
            aliases[i + n_in[k] - p.n_alias + a] = o + 1 + a
        i, o = i + n_in[k], o + n_out[k]
    flat = pl.pallas_call(
        body,
        grid=grid,
        in_specs=[s for p in parts for s in p.in_specs],
        out_specs=[s for p in parts for s in p.out_specs],
        out_shape=[s for p in parts for s in p.out_shape],
        scratch_shapes=[s for p in parts for s in p.scratch * seqs_per_step],
        input_output_aliases=aliases,
        compiler_params=_params(("arbitrary",) * len(grid), vmem_mb),
        name=name,
    )(*[x for p in parts for x in p.operands])
    out, o = [], 0
    for k in range(len(parts)):
        out.append(list(flat[o:o + n_out[k]]))
        o += n_out[k]
    return out


def _state_spec(stacked, layer, seqs=1):
    tail = stacked.shape[2:]
    return pl.BlockSpec((None, seqs) + tail, lambda s, c: (layer, s) + (0,) * len(tail))


_NORM_STEPS = 8


def _inproj_kernel(x_ref, g_ref, w_ref, o_ref, h_ref):
    p = pl.program_id(0)
    j = pl.program_id(1)
    rows = x_ref.shape[0]

    def normalise_slice():
        rs = pl.ds(pl.multiple_of(jnp.minimum(j, _NORM_STEPS - 1) * rows, rows), rows)
        x = x_ref[...]
        ms = jnp.mean(x * x, axis=-1, keepdims=True)
        h_ref[p % 2, rs, :] = (x * lax.rsqrt(ms + RMS_EPS) * g_ref[...]).astype(BF16)

    pl.when(p == 0)(normalise_slice)

    @pl.when(p > 0)
    def _():
        normalise_slice()
        o_ref[...] = _dot(h_ref[(p - 1) % 2], w_ref[...])


def _inproj(x, g, w_all, layer, tm, tn):
    m = x.shape[0]
    n_row_tiles, n_col = m // tm, NPROJ // tn
    assert n_col >= _NORM_STEPS and tm % (_NORM_STEPS * 2 * SUBLANE) == 0
    col = lambda p, j: jnp.where(p > 0, j, 0)
    x_slice = lambda p, j: jnp.minimum(p, n_row_tiles - 1) * _NORM_STEPS + jnp.minimum(j, _NORM_STEPS - 1)
    return pl.pallas_call(
        _inproj_kernel,
        grid=(n_row_tiles + 1, n_col),
        in_specs=[pl.BlockSpec((tm // _NORM_STEPS, D_MODEL), lambda p, j: (x_slice(p, j), 0)),
                  pl.BlockSpec((1, D_MODEL), lambda p, j: (0, 0)),
                  pl.BlockSpec((None, D_MODEL, tn), lambda p, j: (layer, 0, col(p, j)))],
        out_specs=pl.BlockSpec((tm, tn), lambda p, j: (jnp.maximum(p - 1, 0), col(p, j))),
        out_shape=jax.ShapeDtypeStruct((m, NPROJ), F32),
        scratch_shapes=[pltpu.VMEM((2, tm, D_MODEL), BF16)],
        compiler_params=_params(("arbitrary", "arbitrary"), 58),
        name="inproj",
    )(x, g, w_all)


def _outproj_kernel(ya_ref, yb_ref, yc_ref, yd_ref, w_ref, x_ref, g_ref, o_ref, acc_ref, ss_ref, *, tn, n_row_tiles):
    i = pl.program_id(0)
    j = pl.program_id(1)
    slot = i % 2
    col = pl.ds(pl.multiple_of(j * tn, tn), tn)

    def finalize():
        rs = lax.rsqrt(ss_ref[1 - slot] * (1.0 / D_MODEL) + RMS_EPS)
        o_ref[...] = x_ref[...] + acc_ref[:, col] * rs * g_ref[...]

    def multiply():
        out = None
        for part, r in enumerate((ya_ref, yb_ref, yc_ref, yd_ref)):
            term = _dot(r[...].astype(BF16), w_ref[part * MIX:(part + 1) * MIX, :])
            out = term if out is None else out + term
        acc_ref[:, col] = out
        sq = jnp.sum(out * out, axis=1, keepdims=True)
        ss_ref[slot] = jnp.where(j == 0, sq, ss_ref[slot] + sq)

    pl.when(i == 0)(multiply)
    pl.when(i == n_row_tiles)(finalize)

    @pl.when(jnp.logical_and(i > 0, i < n_row_tiles))
    def _():
        finalize()
        multiply()


def _outproj(ys, w_all, layer, x, g, tm, tn):
    m = x.shape[0]
    n_row_tiles = m // tm
    yspec = pl.BlockSpec((tm, MIX), lambda i, j: (jnp.minimum(i, n_row_tiles - 1), 0))
    lagged = pl.BlockSpec((tm, tn), lambda i, j: (jnp.maximum(i - 1, 0), jnp.where(i > 0, j, 0)))
    return pl.pallas_call(
        functools.partial(_outproj_kernel, tn=tn, n_row_tiles=n_row_tiles),
        grid=(n_row_tiles + 1, D_MODEL // tn),
        in_specs=[yspec, yspec, yspec, yspec,
                  pl.BlockSpec((None, D_MODEL, tn), lambda i, j: (layer, 0, j)),
                  lagged,
                  pl.BlockSpec((1, tn), lambda i, j: (0, j))],
        out_specs=lagged,
        out_shape=jax.ShapeDtypeStruct((m, D_MODEL), F32),
        scratch_shapes=[pltpu.VMEM((tm, D_MODEL), F32), pltpu.VMEM((2, tm, 1), F32)],
        compiler_params=_params(("arbitrary", "arbitrary"), 56),
        name="outproj",
    )(*ys, w_all, x, g)


def _mlstm_kernel(q_ref, k_ref, v_ref, o_ref, z_ref, gate_ref, bias_ref, mn_ref, c0_ref, n0_ref, m0_ref,
                  c_alias, n_alias, m_alias, y_ref, c_ref, n_ref, m_ref, *, L, valid, first_chunk_init):
    del c_alias, n_alias, m_alias
    if first_chunk_init:
        c_ref[...] = c0_ref[...]
        n_ref[...] = n0_ref[...]
        m_ref[...] = m0_ref[...]
        return

    row = lax.broadcasted_iota(jnp.int32, (L, LANE), 0)
    row_ok = row < valid
    gates = GATE_CAP * jnp.tanh((gate_ref[...] + bias_ref[...]) / GATE_CAP)
    logf = jnp.where(row_ok, jax.nn.log_sigmoid(gates), 0.0)
    b_all = _cumsum_rows(pltpu.roll(logf, LANE - MLSTM_HEADS, axis=1))
    r_all = jnp.where(row_ok, gates - b_all, NEG)
    r_all_t = r_all.T
    yield

    tr = lax.broadcasted_iota(jnp.int32, (L, L), 0)
    tc = lax.broadcasted_iota(jnp.int32, (L, L), 1)
    tril = tc <= tr

    heads = range(MLSTM_HEADS)
    sl = [slice(h * MLSTM_DH, (h + 1) * MLSTM_DH) for h in heads]
    q = [q_ref[:, sl[h]] * (MLSTM_DH ** -0.5) for h in heads]
    k = [k_ref[:, sl[h]] for h in heads]
    v = [v_ref[:, sl[h]] for h in heads]
    c_prev = [c_ref[0, h] for h in heads]
    n_prev = [n_ref[0, h:h + 1, :] for h in heads]
    m_prev = [m_ref[0, h:h + 1, 0:1] for h in heads]
    qk = [_dot_nt(q[h], k[h]) for h in heads]
    qc = [_dot(q[h], c_prev[h]) for h in heads]
    qn = [jnp.sum(q[h] * n_prev[h], axis=1, keepdims=True) for h in heads]
    yield

    b_col = [b_all[:, h:h + 1] for h in heads]
    dmat = [jnp.where(tril, b_col[h] + r_all_t[h:h + 1, :], NEG) for h in heads]
    inter = [b_col[h] + m_prev[h] for h in heads]
    mt = [jnp.maximum(inter[h], jnp.max(dmat[h], axis=1, keepdims=True)) for h in heads]
    yield
    s = [jnp.exp(dmat[h] - mt[h]) * qk[h] for h in heads]
    sv = [_dot(s[h], v[h]) for h in heads]
    yield
    w_prev_t = [jnp.exp(inter[h] - mt[h]) for h in heads]
    den = [w_prev_t[h] * qn[h] + jnp.sum(s[h], axis=1, keepdims=True) for h in heads]
    hh = [(w_prev_t[h] * qc[h] + sv[h]) / jnp.maximum(jnp.abs(den[h]), jnp.exp(-mt[h])) for h in heads]
    yield

    b_last = [b_all[L - 1:L, h:h + 1] for h in heads]
    dec = [b_last[h] + r_all[:, h:h + 1] for h in heads]
    m_new = [jnp.maximum(b_last[h] + m_prev[h], jnp.max(dec[h], axis=0, keepdims=True)) for h in heads]
    w_prev = [jnp.exp(b_last[h] + m_prev[h] - m_new[h]) for h in heads]
    kw = [k[h] * jnp.exp(dec[h] - m_new[h]) for h in heads]
    c_new = [w_prev[h] * c_prev[h] + _dot_tn(kw[h], v[h]) for h in heads]
    n_new = [w_prev[h] * n_prev[h] + jnp.sum(kw[h], axis=0, keepdims=True) for h in heads]
    yield

    hn = [hh[h] * lax.rsqrt(jnp.mean(hh[h] * hh[h], axis=1, keepdims=True) + RMS_EPS) * mn_ref[:, sl[h]] for h in heads]
    ys = [hn[h] * _sigmoid(o_ref[:, sl[h]]) * _silu(z_ref[:, sl[h]]) for h in heads]
    yield

    y_ref[...] = jnp.concatenate(ys, axis=1).astype(y_ref.dtype)
    for h in heads:
        c_ref[0, h] = c_new[h]
    n_ref[0] = jnp.concatenate(n_new, axis=0)
    m_ref[0] = jnp.concatenate([jnp.broadcast_to(m_new[h], (1, LANE)) for h in heads], axis=0)


def _mlstm(proj, bias, mnorm, st0, l0, acc, layer, n_seq, t_pad, L, valid, y_dtype, S):
    nc = t_pad // L
    wide = lambda name: pl.BlockSpec((S * L, MIX), lambda s, c, cb=_colblock(name): (s * nc + c, cb))
    vec = lambda n: pl.BlockSpec((1, n), lambda s, c: (0, 0))
    return _Part(
        functools.partial(_mlstm_kernel, L=L, valid=valid),
        operands=[proj, proj, proj, proj, proj, proj, bias, mnorm, *st0, *acc],
        in_specs=[wide("qa"), wide("ka"), wide("va"), wide("oa"), wide("za"),
                  pl.BlockSpec((S * L, LANE), lambda s, c, cb=_colblock("gate"): (s * nc + c, cb)),
                  vec(LANE), vec(MIX)] + [_state_spec(a, l0, S) for a in st0] + [_ANY] * len(acc),
        out_specs=[pl.BlockSpec((S * L, MIX), lambda s, c: (s * nc + c, 0))] + [_state_spec(a, layer, S) for a in acc],
        out_shape=[jax.ShapeDtypeStruct((n_seq * t_pad, MIX), y_dtype)]
        + [jax.ShapeDtypeStruct(a.shape, F32) for a in acc],
        n_alias=len(acc),
        kinds=["rows"] * 6 + ["whole"] * 2 + ["state"] * 3 + ["whole"] * 3 + ["rows"] + ["state"] * 3, rows=L)


def _swa_kernel(q_ref, k_ref, v_ref, z_ref, cos_ref, sin_ref, sink_ref, kc_ref, vc_ref, k_alias, v_alias,
                y_ref, ko_ref, vo_ref, *, Lq, valid, has_cache, first_chunk_init):
    del k_alias, v_alias
    if first_chunk_init:
        ko_ref[...] = kc_ref[...]
        vo_ref[...] = vc_ref[...]
        return
    blk = pl.program_id(1)

    cos = cos_ref[...]
    sin = sin_ref[...]
    lane = lax.broadcasted_iota(jnp.int32, (Lq, LANE), 1)
    first_half = (lane % SWA_DH) < (SWA_DH // 2)

    def rope(x):
        partner = jnp.where(first_half, pltpu.roll(x, LANE - SWA_DH // 2, axis=1), pltpu.roll(x, SWA_DH // 2, axis=1))
        return x * cos + partner * sin

    k_cur = rope(k_ref[...])
    v_cur = v_ref[...]
    k_prev = ko_ref[0]
    v_prev = vo_ref[0]
    kk = jnp.concatenate([k_prev, k_cur], axis=0)
    vv = jnp.concatenate([v_prev, v_cur], axis=0)

    rows, nk = SWA_GROUP * Lq, WINDOW + Lq
    t = lax.broadcasted_iota(jnp.int32, (rows, nk), 0) & (Lq - 1)
    c = lax.broadcasted_iota(jnp.int32, (rows, nk), 1)
    prev_ok = jnp.logical_or(has_cache, blk > 0)
    mask = ((c < WINDOW) & (c > t) & prev_ok) | ((c >= WINDOW) & (c - WINDOW <= t) & (c - WINDOW < valid))

    yield
    pairs_per_group = SWA_GROUP // 2
    groups = range(SWA_KV_HEADS)
    gsl = [slice(g * SWA_DH, (g + 1) * SWA_DH) for g in groups]
    q_pairs = [[rope(q_ref[:, (g * pairs_per_group + p) * LANE:(g * pairs_per_group + p + 1) * LANE])
                * (SWA_DH ** -0.5 * LOG2E) for p in range(pairs_per_group)] for g in groups]
    q_st = [jnp.concatenate([q_pairs[g][h // 2][:, (h % 2) * SWA_DH:(h % 2 + 1) * SWA_DH]
                             for h in range(SWA_GROUP)], axis=0) for g in groups]
    sink = [jnp.concatenate([jnp.broadcast_to(sink_ref[0:1, g * SWA_GROUP + h:g * SWA_GROUP + h + 1], (Lq, 1))
                             for h in range(SWA_GROUP)], axis=0) * LOG2E for g in groups]
    yield
    s = [jnp.where(mask, _dot_nt(q_st[g], kk[:, gsl[g]]), NEG) for g in groups]
    mx = [jnp.maximum(jnp.max(s[g], axis=1, keepdims=True), sink[g]) for g in groups]
    yield
    p = [jnp.exp2(s[g] - mx[g]) for g in groups]
    den = [jnp.sum(p[g], axis=1, keepdims=True) + jnp.exp2(sink[g] - mx[g]) for g in groups]
    yield
    o = [_dot(p[g], vv[:, gsl[g]]) / den[g] for g in groups]
    yield
    y_chunks = []
    for g in groups:
        for pr in range(pairs_per_group):
            csl = slice((g * pairs_per_group + pr) * LANE, (g * pairs_per_group + pr + 1) * LANE)
            pair = jnp.concatenate([o[g][(2 * pr) * Lq:(2 * pr + 1) * Lq], o[g][(2 * pr + 1) * Lq:(2 * pr + 2) * Lq]],
                                   axis=1)
            y_chunks.append(pair * _silu(z_ref[:, csl]))
    y_ref[...] = jnp.concatenate(y_chunks, axis=1).astype(y_ref.dtype)

    if valid == WINDOW:
        ko_ref[0] = k_cur
        vo_ref[0] = v_cur
    else:
        ko_ref[0, 0:WINDOW - valid, :] = k_prev[valid:, :]
        ko_ref[0, WINDOW - valid:WINDOW, :] = k_cur[0:valid, :]
        vo_ref[0, 0:WINDOW - valid, :] = v_prev[valid:, :]
        vo_ref[0, WINDOW - valid:WINDOW, :] = v_cur[0:valid, :]


def _swa(proj, cos, sin, sinks, st0, l0, acc, layer, n_seq, t_pad, Lq, valid, has_cache, table_per_block, y_dtype, S):
    nb = t_pad // Lq
    rowblk = lambda s, b: s * nb + b
    wide = lambda name: pl.BlockSpec((S * Lq, MIX), lambda s, b, cb=_colblock(name): (rowblk(s, b), cb))
    narrow = lambda name: pl.BlockSpec((S * Lq, LANE), lambda s, b, cb=_colblock(name): (rowblk(s, b), cb))
    table = pl.BlockSpec((Lq, LANE), (lambda s, b: (b, 0)) if table_per_block else (lambda s, b: (0, 0)))
    return _Part(
        functools.partial(_swa_kernel, Lq=Lq, valid=valid, has_cache=has_cache),
        operands=[proj, proj, proj, proj, cos, sin, sinks, *st0, *acc],
        in_specs=[wide("qb"), narrow("kb"), narrow("vb"), wide("zb"), table, table,
                  pl.BlockSpec((1, LANE), lambda s, b: (0, 0))] + [_state_spec(a, l0, S) for a in st0] + [_ANY] * len(acc),
        out_specs=[pl.BlockSpec((S * Lq, MIX), lambda s, b: (rowblk(s, b), 0))] + [_state_spec(a, layer, S) for a in acc],
        out_shape=[jax.ShapeDtypeStruct((n_seq * t_pad, MIX), y_dtype)]
        + [jax.ShapeDtypeStruct(a.shape, F32) for a in acc],
        n_alias=len(acc),
        kinds=["rows"] * 4 + ["whole"] * 3 + ["state"] * 2 + ["whole"] * 2 + ["rows"] + ["state"] * 2, rows=Lq)


_CONV_PAD = SUBLANE


def _spread_heads(x, e):
    lane = lax.broadcasted_iota(jnp.int32, x.shape, 1)
    x = jnp.where(lane < SSD_HEADS, x, 0.0)
    hi = x.astype(BF16)
    rest = x - hi.astype(F32)
    mid = rest.astype(BF16)
    lo = (rest - mid.astype(F32)).astype(BF16)
    return _dot(hi, e) + _dot(mid, e) + _dot(lo, e)


def _ssd_kernel(xbc_ref, z_ref, dt_ref, cw_ref, cb_ref, dtb_ref, alog_ref, dskip_ref, sn_ref, e_ref, conv0_ref, s0_ref,
                conv_alias, s_alias, y_ref, convo_ref, so_ref, full_ref, *, L, valid, first_chunk_init):
    del conv_alias, s_alias
    hist = SSD_CONV - 1
    if first_chunk_init:
        so_ref[...] = s0_ref[...]
        full_ref[_CONV_PAD - hist:_CONV_PAD, :] = conv0_ref[0]
        return

    full_ref[_CONV_PAD:_CONV_PAD + L, :] = xbc_ref[...]
    full = full_ref[...]
    acc = cb_ref[...] + full[_CONV_PAD:, :] * cw_ref[hist:hist + 1, :]
    for j in range(hist):
        acc = acc + pltpu.roll(full, hist - j, axis=0)[_CONV_PAD:, :] * cw_ref[j:j + 1, :]
    xc = _silu(acc)
    new_hist = full_ref[_CONV_PAD + valid - hist:_CONV_PAD + valid, :]
    convo_ref[0] = new_hist
    full_ref[_CONV_PAD - hist:_CONV_PAD, :] = new_hist
    yield

    nbc = SSD_GROUPS * SSD_DSTATE
    row_ok = lax.broadcasted_iota(jnp.int32, (L, LANE), 0) < valid
    dtv = jnp.where(row_ok, jax.nn.softplus(dt_ref[...] + dtb_ref[...]), 0.0)
    a = dtv * (-jnp.exp(alog_ref[...]) * LOG2E)
    cum = _cumsum_rows(a)
    cum_t = cum.T
    tr = lax.broadcasted_iota(jnp.int32, (L, L), 0)
    tc = lax.broadcasted_iota(jnp.int32, (L, L), 1)
    tril = tc <= tr
    yield

    groups, heads = range(SSD_GROUPS), range(SSD_HEADS)
    gp = SSD_HPG * SSD_DH
    bg = [xc[:, MIX + g * SSD_DSTATE:MIX + (g + 1) * SSD_DSTATE] for g in groups]
    cg = [xc[:, MIX + nbc + g * SSD_DSTATE:MIX + nbc + (g + 1) * SSD_DSTATE] for g in groups]
    st = [so_ref[0, g * SSD_HPG:(g + 1) * SSD_HPG].reshape(gp, SSD_DSTATE) for g in groups]
    cb = [_dot_nt(cg[g], bg[g]) for g in groups]
    cst = [_dot_nt(cg[g], st[g]) for g in groups]
    yield
    c_col = [cum[:, h:h + 1] for h in heads]
    c_last = [cum[L - 1:L, h:h + 1] for h in heads]
    lm = [jnp.where(tril, jnp.exp2(jnp.where(tril, c_col[h] - cum_t[h:h + 1, :], 0.0)), 0.0) for h in heads]

    spread = _spread_heads(jnp.concatenate([dtv, jnp.exp2(cum), jnp.exp2(cum[L - 1:L, :] - cum)], axis=0), e_ref[...])
    xs = xc[:, :MIX]
    xdt = xs * spread[0:L]
    xw = xdt * spread[2 * L:3 * L]
    yield
    low_half = lax.broadcasted_iota(jnp.int32, (L, LANE), 1) < SSD_DH
    y_in = []
    for pair in range(SSD_HEADS // 2):
        g = (2 * pair) // SSD_HPG
        x_pair = xdt[:, pair * LANE:(pair + 1) * LANE]
        y_in.append(jnp.where(low_half, _dot(cb[g] * lm[2 * pair], x_pair), _dot(cb[g] * lm[2 * pair + 1], x_pair)))
    y_state = jnp.concatenate(cst, axis=1) * spread[L:2 * L]
    yield
    upd = [_dot_tn(xw[:, g * gp:(g + 1) * gp], bg[g]) for g in groups]
    for h in heads:
        g, hh = divmod(h, SSD_HPG)
        so_ref[0, h] = jnp.exp2(c_last[h]) * st[g][hh * SSD_DH:(hh + 1) * SSD_DH] + upd[g][hh * SSD_DH:(hh + 1) * SSD_DH]
    yc = (jnp.concatenate(y_in, axis=1) + y_state + dskip_ref[...] * xs) * _silu(z_ref[...])
    y_ref[...] = (yc * lax.rsqrt(jnp.mean(yc * yc, axis=1, keepdims=True) + RMS_EPS) * sn_ref[...]).astype(y_ref.dtype)


def _ssd(proj, conv_w, conv_b, dt_bias, a_log, d_skip, snorm, st0, l0, acc, layer, n_seq, t_pad, L, valid, y_dtype, S):
    nc = t_pad // L
    rowblk = lambda s, c: s * nc + c
    vec = lambda n: pl.BlockSpec((1, n), lambda s, c: (0, 0))
    spread = (jnp.arange(MIX)[None, :] // SSD_DH == jnp.arange(LANE)[:, None]).astype(BF16)
    return _Part(
        functools.partial(_ssd_kernel, L=L, valid=valid),
        operands=[proj, proj, proj, conv_w, conv_b, dt_bias, a_log, d_skip, snorm, spread, *st0, *acc],
        in_specs=[pl.BlockSpec((S * L, SSD_CONV_DIM), lambda s, c, cb=_colblock("xbc"): (rowblk(s, c), cb)),
                  pl.BlockSpec((S * L, MIX), lambda s, c, cb=_colblock("zc"): (rowblk(s, c), cb)),
                  pl.BlockSpec((S * L, LANE), lambda s, c, cb=_colblock("dt"): (rowblk(s, c), cb)),
                  pl.BlockSpec((SSD_CONV, SSD_CONV_DIM), lambda s, c: (0, 0)),
                  vec(SSD_CONV_DIM), vec(LANE), vec(LANE), vec(MIX), vec(MIX),
                  pl.BlockSpec((LANE, MIX), lambda s, c: (0, 0))]
        + [_state_spec(a, l0, S) for a in st0] + [_ANY] * len(acc),
        out_specs=[pl.BlockSpec((S * L, MIX), lambda s, c: (rowblk(s, c), 0))] + [_state_spec(a, layer, S) for a in acc],
        out_shape=[jax.ShapeDtypeStruct((n_seq * t_pad, MIX), y_dtype)]
        + [jax.ShapeDtypeStruct(a.shape, F32) for a in acc],
        n_alias=len(acc),
        kinds=["rows"] * 3 + ["whole"] * 7 + ["state"] * 2 + ["whole"] * 2 + ["rows"] + ["state"] * 2, rows=L,
        scratch=[pltpu.VMEM((_CONV_PAD + L, SSD_CONV_DIM), F32)])


_POOL_PAD = 2 * SUBLANE
assert all(w & (w - 1) == 0 for w in POOL_WINDOWS) and POOL_HIST < _POOL_PAD


def _pool_kernel(u_ref, z_ref, lin_ref, scale_ref, p0_ref, p_alias, y_ref, po_ref, full_ref, *, L, valid, n_hist,
                 first_chunk_init):
    del p_alias
    if first_chunk_init:
        full_ref[_POOL_PAD - POOL_HIST:_POOL_PAD, :] = p0_ref[0]
        return
    step = pl.program_id(1)

    full_ref[_POOL_PAD:_POOL_PAD + L, :] = u_ref[...]
    pos = n_hist + step * L + lax.broadcasted_iota(jnp.int32, (L, 1), 0)
    ys = []
    for g, w in enumerate(POOL_WINDOWS):
        sl = slice(g * POOL_GC, (g + 1) * POOL_GC)
        rows = full_ref[:, sl]
        tot, span = rows, 1
        while span < w:
            tot = tot + pltpu.roll(tot, span, axis=0)
            span *= 2
        cur = rows[_POOL_PAD:]
        cnt = jnp.minimum(pos + 1, w).astype(F32)
        d = tot[_POOL_PAD:] / cnt - cur
        ys.append(_dot(d, lin_ref[g]) * scale_ref[:, sl] * _silu(z_ref[:, sl]))
        yield
    y_ref[...] = jnp.concatenate(ys, axis=1).astype(y_ref.dtype)
    new_hist = full_ref[_POOL_PAD + valid - POOL_HIST:_POOL_PAD + valid, :]
    po_ref[0] = new_hist
    full_ref[_POOL_PAD - POOL_HIST:_POOL_PAD, :] = new_hist


def _pool(proj, lin, scale, st0, l0, acc, layer, n_seq, t_pad, L, valid, n_hist, y_dtype, S):
    nt = t_pad // L
    rowblk = lambda s, c: s * nt + c
    return _Part(
        functools.partial(_pool_kernel, L=L, valid=valid, n_hist=n_hist),
        operands=[proj, proj, lin, scale, *st0, *acc],
        in_specs=[pl.BlockSpec((S * L, MIX), lambda s, c, cb=_colblock("ud"): (rowblk(s, c), cb)),
                  pl.BlockSpec((S * L, MIX), lambda s, c, cb=_colblock("zd"): (rowblk(s, c), cb)),
                  pl.BlockSpec((len(POOL_WINDOWS), POOL_GC, POOL_GC), lambda s, c: (0, 0, 0)),
                  pl.BlockSpec((1, MIX), lambda s, c: (0, 0))] + [_state_spec(a, l0, S) for a in st0] + [_ANY] * len(acc),
        out_specs=[pl.BlockSpec((S * L, MIX), lambda s, c: (rowblk(s, c), 0))] + [_state_spec(a, layer, S) for a in acc],
        out_shape=[jax.ShapeDtypeStruct((n_seq * t_pad, MIX), y_dtype)]
        + [jax.ShapeDtypeStruct(a.shape, F32) for a in acc],
        n_alias=len(acc),
        kinds=["rows"] * 2 + ["whole"] * 2 + ["state", "whole", "rows", "state"], rows=L,
        scratch=[pltpu.VMEM((_POOL_PAD + L, MIX), F32)])


_REF_SIZES = (MIX, MIX, MIX, MIX, MIX, MLSTM_HEADS, MLSTM_HEADS,
              SWA_HEADS * SWA_DH, SWA_KV_HEADS * SWA_DH, SWA_KV_HEADS * SWA_DH, MIX,
              MIX, SSD_CONV_DIM, SSD_HEADS, MIX, MIX)
_REF_NAMES = ("qa", "ka", "va", "oa", "za", "ia", "fa", "qb", "kb", "vb", "zb", "zc", "xbc", "dt", "ud", "zd")


def _regroup_plan():
    ref_off, o = {}, 0
    for name, size in zip(_REF_NAMES, _REF_SIZES):
        ref_off[name] = o
        o += size
    src = []
    for name, (off, width) in _COL.items():
        start = ref_off["ia"] if name == "gate" else ref_off[name]
        src += [start + b * LANE for b in range(width // LANE)]
    assert len(src) == NPROJ // LANE and all(s % SUBLANE == 0 and s + LANE <= o for s in src)
    return src


_REGROUP_BLOCKS = 4


def _regroup_kernel(src_tab, *refs):
    del src_tab
    o_ref = refs[-1]
    for i, wt_ref in enumerate(refs[:-1]):
        o_ref[:, i * LANE:(i + 1) * LANE] = wt_ref[...].T.astype(BF16)


def _regroup_w_in(w_in):
    depth, d_in, _ = w_in.shape
    w_t = jnp.transpose(w_in, (0, 2, 1))
    src = jnp.asarray([s // SUBLANE for s in _regroup_plan()], jnp.int32)
    return pl.pallas_call(
        _regroup_kernel,
        grid_spec=pltpu.PrefetchScalarGridSpec(
            num_scalar_prefetch=1,
            grid=(depth, NPROJ // (_REGROUP_BLOCKS * LANE)),
            in_specs=[pl.BlockSpec((None, pl.Element(LANE), pl.Element(d_in)),
                                   lambda l, t, src, i=i: (l, src[t * _REGROUP_BLOCKS + i] * SUBLANE, 0))
                      for i in range(_REGROUP_BLOCKS)],
            out_specs=pl.BlockSpec((None, d_in, _REGROUP_BLOCKS * LANE), lambda l, t, src: (l, 0, t))),
        out_shape=jax.ShapeDtypeStruct((depth, d_in, NPROJ), BF16),
        compiler_params=_params(("arbitrary", "arbitrary"), 48),
        name="regroup",
    )(src, *([w_t] * _REGROUP_BLOCKS))


def _pad_lanes(v, n=LANE):
    return jnp.pad(v, (0, n - v.shape[0])).reshape(1, n)


def _rope_tables(pos):
    half = SWA_DH // 2
    inv = ROPE_THETA ** (-jnp.arange(half, dtype=F32) / half)
    ang = pos.astype(F32)[:, None] * inv[None, :]
    cos, sin = jnp.cos(ang), jnp.sin(ang)
    reps = LANE // SWA_DH
    return jnp.tile(jnp.concatenate([cos, cos], axis=1), (1, reps)), jnp.tile(jnp.concatenate([-sin, sin], axis=1), (1, reps))


class _Path:
    def __init__(self, n_seq, t_pad, valid, chunk, seqs, has_cache, n_hist, tm_in, tn_in, tm_out, tn_out, y_dtype):
        self.n_seq, self.t_pad, self.valid, self.chunk, self.seqs = n_seq, t_pad, valid, chunk, seqs
        self.has_cache, self.n_hist, self.y_dtype = has_cache, n_hist, y_dtype
        self.tm_in, self.tn_in, self.tm_out, self.tn_out = tm_in, tn_in, tm_out, tn_out


_MIXER_ORDER = (3, 0, 1, 2)


def _layer(x, layer, prm, states, l0, acc, tables, path):
    (g_pre, g_post, w_in, w_out, gate_bias, mnorm, sinks, conv_w, conv_b, dt_bias, a_log, d_skip, snorm,
     pool_lin, pool_scale) = prm
    c0, n0, m0, kc, vc, s0, conv0, p0 = states
    ca, na, ma, ka, va, sa, conva, pa = acc
    cos, sin = tables
    p = path
    proj = _inproj(x, g_pre, w_in, layer, p.tm_in, p.tn_in)
    L, valid, S = p.chunk, min(p.valid, p.chunk), p.seqs
    assert S == 1 or p.t_pad == L
    parts = [
        _mlstm(proj, gate_bias, mnorm, (c0, n0, m0), l0, (ca, na, ma), layer, p.n_seq, p.t_pad, L, valid, p.y_dtype, S),
        _swa(proj, cos, sin, sinks, (kc, vc), l0, (ka, va), layer, p.n_seq, p.t_pad, L, valid,
             p.has_cache, not p.has_cache, p.y_dtype, S),
        _ssd(proj, conv_w, conv_b, dt_bias, a_log, d_skip, snorm, (conv0, s0), l0, (conva, sa), layer,
             p.n_seq, p.t_pad, L, valid, p.y_dtype, S),
        _pool(proj, pool_lin, pool_scale, (p0,), l0, (pa,), layer, p.n_seq, p.t_pad, L, valid, p.n_hist, p.y_dtype, S)]
    outs = _fused_call([parts[i] for i in _MIXER_ORDER], (p.n_seq // S, p.t_pad // L), "mixers", 48, S)
    (ya, ca, na, ma), (yb, ka, va), (yc, conva, sa), (yd, pa) = [outs[_MIXER_ORDER.index(i)] for i in range(len(parts))]
    x = _outproj((ya, yb, yc, yd), w_out, layer, x, g_post, p.tm_out, p.tn_out)
    return x, (ca, na, ma, ka, va, sa, conva, pa)


def kernel(x_prompt, x_sample, state_mlstm_C, state_mlstm_n, state_mlstm_m, cache_swa_k, cache_swa_v, state_ssd,
           state_ssd_conv, state_pool, norm_pre, norm_post, w_in, w_out, mlstm_b_i, mlstm_b_f, mlstm_norm,
           swa_sinks, ssd_conv_w, ssd_conv_b, ssd_dt_bias, ssd_A_log, ssd_D, ssd_norm, pool_lin, pool_scale):
    bp, seq, _ = x_prompt.shape
    bs, dec_seq, _ = x_sample.shape
    t_s = SUBLANE * pl.cdiv(dec_seq, SUBLANE)

    assert WINDOW == SSD_CHUNK == 2 * MLSTM_CHUNK
    prompt = _Path(bp, seq, seq, WINDOW, 1, False, 0, 1024, 1280, 1024, 512, BF16)
    sample = _Path(bs, t_s, dec_seq, t_s, 4, True, POOL_HIST, bs * t_s, 1280, bs * t_s, 512, F32)

    w_in_r = _regroup_w_in(w_in)
    w_out_b = w_out.astype(BF16)

    xp = x_prompt.reshape(bp * seq, D_MODEL)
    xs = jnp.pad(x_sample, ((0, 0), (0, t_s - dec_seq), (0, 0))).reshape(bs * t_s, D_MODEL)

    tab_p = _rope_tables(jnp.arange(seq))
    tab_s = _rope_tables(PAST_LEN + jnp.arange(t_s))

    def state_shapes(layers, b):
        return ((layers, b, MLSTM_HEADS, MLSTM_DH, MLSTM_DH), (layers, b, MLSTM_HEADS, LANE), (layers, b, MLSTM_HEADS, LANE),
                (layers, b, WINDOW, LANE), (layers, b, WINDOW, LANE),
                (layers, b, SSD_HEADS, SSD_DH, SSD_DSTATE), (layers, b, SSD_CONV - 1, SSD_CONV_DIM),
                (layers, b, POOL_HIST, MIX))

    p_states0 = tuple(jnp.zeros(s, F32) for s in state_shapes(1, bp))
    s_states0 = (state_mlstm_C, state_mlstm_n,
                 jnp.broadcast_to(state_mlstm_m[..., None], (DEPTH, bs, MLSTM_HEADS, LANE)),
                 cache_swa_k.reshape(DEPTH, bs, WINDOW, LANE), cache_swa_v.reshape(DEPTH, bs, WINDOW, LANE),
                 state_ssd, state_ssd_conv, state_pool)
    p_acc = tuple(jnp.zeros(s, F32) for s in state_shapes(DEPTH, bp))
    s_acc = tuple(jnp.zeros(s, F32) for s in state_shapes(DEPTH, bs))

    for l in range(DEPTH):
        gate_bias = _pad_lanes(jnp.concatenate([mlstm_b_i[l], mlstm_b_f[l]]))
        prm = (norm_pre[l].reshape(1, D_MODEL), norm_post[l].reshape(1, D_MODEL), w_in_r, w_out_b,
               gate_bias, mlstm_norm[l].reshape(1, MIX), _pad_lanes(swa_sinks[l]),
               ssd_conv_w[l], ssd_conv_b[l].reshape(1, SSD_CONV_DIM), _pad_lanes(ssd_dt_bias[l]),
               _pad_lanes(ssd_A_log[l]), jnp.repeat(ssd_D[l], SSD_DH).reshape(1, MIX), ssd_norm[l].reshape(1, MIX),
               pool_lin[l], pool_scale[l].reshape(1, MIX))
        xp, p_acc = _layer(xp, l, prm, p_states0, 0, p_acc, tab_p, prompt)
        xs, s_acc = _layer(xs, l, prm, s_states0, l, s_acc, tab_s, sample)

    def finish(acc, b):
        c, n, m, k, v, s, conv, pool = acc
        kv_shape = (DEPTH, b, WINDOW, SWA_KV_HEADS, SWA_DH)
        return c, n, m[..., 0], k.reshape(kv_shape), v.reshape(kv_shape), s, conv, pool

    y_prompt = xp.reshape(bp, seq, D_MODEL)
    y_sample = xs.reshape(bs, t_s, D_MODEL)[:, :dec_seq]
    return (y_prompt, y_sample) + finish(p_acc, bp) + finish(s_acc, bs)
```
